```python
import jax, jax.numpy as jnp
from jax import lax
import numpy as np

D_MODEL = 1024
BATCH = 8
SEQ = 8192
DEPTH = 2
DEC_BATCH = 16
DEC_SEQ = 16
PAST_LEN = 1024

CHUNK = 64
Q_BLOCK = 128
HEAD_DIM = 64
N_HEADS = D_MODEL // HEAD_DIM
H_SB = N_HEADS // 2
H_DSA = N_HEADS // 2
KV_DSA = 2
H_FOX = N_HEADS
IDX_HEADS = 8
IDX_DIM = 64
DSA_TOPK = 256
ROT_DIM = HEAD_DIM // 4
ROPE_THETA = 500000.0
D_FF = (8 * D_MODEL // 3 + 127) // 128 * 128
N_EXPERTS = 8
TOP_K = 2
D_EXPERT = 7 * D_MODEL // 2
MOE_BLOCK = 128
PLE_DIM = 256
N_EVEN = (DEPTH + 1) // 2
N_ODD = DEPTH // 2
EPS = 1e-6
SPLIT_EVEN = (H_SB * HEAD_DIM, H_SB * HEAD_DIM, H_SB * HEAD_DIM,
              H_DSA * HEAD_DIM, KV_DSA * HEAD_DIM, KV_DSA * HEAD_DIM,
              IDX_HEADS * IDX_DIM, IDX_DIM, IDX_HEADS)
SPLIT_ODD = (H_FOX * HEAD_DIM, H_FOX * HEAD_DIM, H_FOX * HEAD_DIM, H_FOX)

kernel_name = "hybrid_stickbreak_dsa_fox_stream_step"


def _split(x, sizes):
    out, o = [], 0
    for s in sizes:
        out.append(x[..., o:o + s])
        o += s
    return out


def rmsnorm(x, g):
    xf = x.astype(jnp.float32)
    y = xf * lax.rsqrt(jnp.mean(xf * xf, axis=-1, keepdims=True) + EPS)
    return (y * g.astype(jnp.float32)).astype(x.dtype)


def partial_rotary(x, pos):
    half = ROT_DIM // 2
    inv = ROPE_THETA ** (-jnp.arange(half, dtype=jnp.float32) / half)
    ang = pos.astype(jnp.float32)[:, None] * inv[None, :]
    cos = jnp.cos(ang)[None, :, None, :]
    sin = jnp.sin(ang)[None, :, None, :]
    xr = x[..., :ROT_DIM].astype(jnp.float32)
    x1, x2 = xr[..., :half], xr[..., half:]
    rot = jnp.concatenate([x1 * cos - x2 * sin, x2 * cos + x1 * sin], axis=-1)
    return jnp.concatenate([rot.astype(x.dtype), x[..., ROT_DIM:]], axis=-1)


def sweep_query_blocks(fn, q_args, qpos):
    T = qpos.shape[0]
    if T <= Q_BLOCK:
        return fn(*q_args, qpos)
    nb = T // Q_BLOCK

    def to_blocks(a):
        return jnp.moveaxis(a.reshape(a.shape[0], nb, Q_BLOCK, *a.shape[2:]), 1, 0)

    xs = tuple(to_blocks(a) for a in q_args) + (qpos.reshape(nb, Q_BLOCK),)
    out = jnp.moveaxis(lax.map(lambda blk: fn(*blk), xs), 0, 1)
    return out.reshape(out.shape[0], T, *out.shape[3:])


def stick_breaking_block(q, qpos, k, v, kpos):
    z = jnp.einsum("bqhd,bkhd->bhqk", q, k, preferred_element_type=jnp.float32) * (HEAD_DIM ** -0.5)
    vis = kpos[None, :] < qpos[:, None]
    log_keep = jnp.where(vis, jax.nn.log_sigmoid(-z), 0.0)
    between = lax.cumsum(log_keep, axis=3, reverse=True) - log_keep
    w = jnp.where(vis, jnp.exp(jax.nn.log_sigmoid(z) + between), 0.0)
    return jnp.einsum("bhqk,bkhd->bqhd", w.astype(v.dtype), v)


def dsa_block(q, qi, wi, qpos, k, v, ki, kpos, n_sel):
    B, Tq = q.shape[:2]
    rel = jax.nn.relu(jnp.einsum("bqhe,bke->bqhk", qi, ki, preferred_element_type=jnp.float32) * (IDX_DIM ** -0.5))
    score = jnp.einsum("bqh,bqhk->bqk", wi.astype(jnp.float32) * (IDX_HEADS ** -0.5), rel)
    vis = (kpos[None, :] // CHUNK) <= (qpos[:, None] // CHUNK)
    score = jnp.where(vis[None], score, -jnp.inf)
    top_score, idx = lax.top_k(score, n_sel)
    valid = jnp.isfinite(top_score)
    bidx = jnp.arange(B)[:, None, None]
    kg = k[bidx, idx]
    vg = v[bidx, idx]
    qg = q.reshape(B, Tq, KV_DSA, H_DSA // KV_DSA, HEAD_DIM)
    logits = jnp.einsum("bqhgd,bqnhd->bqhgn", qg, kg, preferred_element_type=jnp.float32) * (HEAD_DIM ** -0.5)
    logits = jnp.where(valid[:, :, None, None, :], logits, -jnp.inf)
    p = jax.nn.softmax(logits, axis=-1)
    o = jnp.einsum("bqhgn,bqnhd->bqhgd", p.astype(v.dtype), vg)
    return o.reshape(B, Tq, H_DSA, HEAD_DIM)


def fox_block(q, fq, qpos, k, v, fk, kpos):
    logits = jnp.einsum("bqhd,bkhd->bhqk", q, k, preferred_element_type=jnp.float32) * (HEAD_DIM ** -0.5)
    logits = logits + jnp.swapaxes(fq, 1, 2)[..., None] - fk[:, :, None, :]
    vis = kpos[None, :] <= qpos[:, None]
    p = jax.nn.softmax(jnp.where(vis, logits, -jnp.inf), axis=-1)
    return jnp.einsum("bhqk,bkhd->bqhd", p.astype(v.dtype), v)


def even_mixer(h, pos, w_in, g_q, g_k, w_out, past):
    B, T, _ = h.shape
    qa, ka, va, qb, kb, vb, qi, ki, wi = _split(h @ w_in, SPLIT_EVEN)
    heads = lambda a, n: a.reshape(B, T, n, a.shape[-1] // n)
    qa, ka, va = heads(qa, H_SB), heads(ka, H_SB), heads(va, H_SB)
    qb = partial_rotary(rmsnorm(heads(qb, H_DSA), g_q), pos)
    kb = partial_rotary(rmsnorm(heads(kb, KV_DSA), g_k), pos)
    vb = heads(vb, KV_DSA)
    qi = heads(qi, IDX_HEADS)
    new = (ka, va, kb, vb, ki)
    full = new if past is None else tuple(jnp.concatenate([c, n], axis=1) for c, n in zip(past, new))
    ka_f, va_f, kb_f, vb_f, ki_f = full
    L = ka_f.shape[1]
    kpos = jnp.arange(L, dtype=jnp.int32)
    n_sel = min(DSA_TOPK, L // 4)
    oa = sweep_query_blocks(lambda q, qp: stick_breaking_block(q, qp, ka_f, va_f, kpos), (qa,), pos)
    ob = sweep_query_blocks(lambda q, qx, w, qp: dsa_block(q, qx, w, qp, kb_f, vb_f, ki_f, kpos, n_sel),
                            (qb, qi, wi), pos)
    out = jnp.concatenate([oa.reshape(B, T, -1), ob.reshape(B, T, -1)], axis=-1) @ w_out
    if past is None:
        new = tuple(a[:, -PAST_LEN:] for a in new)
    return out, new


def odd_mixer(h, pos, w_in, b_f, g_q, g_k, w_out, past):
    B, T, _ = h.shape
    q, k, v, fl = _split(h @ w_in, SPLIT_ODD)
    q = rmsnorm(q.reshape(B, T, H_FOX, HEAD_DIM), g_q)
    k = rmsnorm(k.reshape(B, T, H_FOX, HEAD_DIM), g_k)
    v = v.reshape(B, T, H_FOX, HEAD_DIM)
    logf = jax.nn.log_sigmoid(fl.astype(jnp.float32) + b_f.astype(jnp.float32))
    new = (k, v, logf)
    full = new if past is None else tuple(jnp.concatenate([c, n], axis=1) for c, n in zip(past, new))
    k_f, v_f, logf_f = full
    L = k_f.shape[1]
    kpos = jnp.arange(L, dtype=jnp.int32)
    F = jnp.cumsum(logf_f.astype(jnp.float32), axis=1)
    fk = jnp.swapaxes(F, 1, 2)
    o = sweep_query_blocks(lambda qq, fq, qp: fox_block(qq, fq, qp, k_f, v_f, fk, kpos), (q, F[:, L - T:]), pos)
    out = o.reshape(B, T, -1) @ w_out
    if past is None:
        new = tuple(a[:, -PAST_LEN:] for a in new)
    return out, new


def swiglu(x, w1, w3, w2):
    return (jax.nn.silu(x @ w1) * (x @ w3)) @ w2


def moe_swiglu(x, w_router, w1, w3, w2):
    N, D = x.shape
    logits = jnp.einsum("nd,de->ne", x.astype(jnp.float32), w_router.astype(jnp.float32))
    top_logit, top_e = lax.top_k(logits, TOP_K)
    gate = jax.nn.softmax(top_logit, axis=-1)
    A = N * TOP_K
    flat_e = top_e.reshape(A)
    order = jnp.argsort(flat_e, stable=True)
    se = flat_e[order]
    st = (order // TOP_K).astype(jnp.int32)
    sg = gate.reshape(A)[order]
    counts = jnp.bincount(flat_e, length=N_EXPERTS)
    start = jnp.cumsum(counts) - counts
    padded = (counts + MOE_BLOCK - 1) // MOE_BLOCK * MOE_BLOCK
    pend = jnp.cumsum(padded)
    pstart = pend - padded
    dest = pstart[se] + jnp.arange(A, dtype=jnp.int32) - start[se]
    n_blocks = -(-A // MOE_BLOCK) + N_EXPERTS
    P = n_blocks * MOE_BLOCK
    row_tok = jnp.zeros((P,), jnp.int32).at[dest].set(st)
    row_gate = jnp.zeros((P,), jnp.float32).at[dest].set(sg)
    block_e = jnp.minimum(jnp.searchsorted(pend, jnp.arange(n_blocks, dtype=pend.dtype) * MOE_BLOCK, side="right"),
                          N_EXPERTS - 1)

    def expert_block(args):
        tok, g, e = args
        xb = x[tok]
        hb = jax.nn.silu(xb @ w1[e]) * (xb @ w3[e])
        return (hb @ w2[e]) * g[:, None].astype(x.dtype)

    yb = lax.map(expert_block, (row_tok.reshape(n_blocks, MOE_BLOCK), row_gate.reshape(n_blocks, MOE_BLOCK), block_e))
    return jax.ops.segment_sum(yb.reshape(P, D), row_tok, num_segments=N)


def trunk(x, p, pos, past_even, past_odd, g_mix, g_ffn, g_ple, w_in_even, g_q_dsa, g_k_dsa, w_out_even,
          w_ff1, w_ff3, w_ff2, w_in_odd, b_forget, g_q_fox, g_k_fox, w_out_odd,
          w_router, w_exp1, w_exp3, w_exp2, w_ple_in, w_ple_gate):
    h = x
    new_even, new_odd = [], []
    for i in range(DEPTH):
        j = i // 2
        n = rmsnorm(h, g_mix[i])
        if i % 2 == 0:
            past = None if past_even is None else tuple(c[j] for c in past_even)
            mix, new = even_mixer(n, pos, w_in_even[j], g_q_dsa[j], g_k_dsa[j], w_out_even[j], past)
            new_even.append(new)
        else:
            past = None if past_odd is None else tuple(c[j] for c in past_odd)
            mix, new = odd_mixer(n, pos, w_in_odd[j], b_forget[j], g_q_fox[j], g_k_fox[j], w_out_odd[j], past)
            new_odd.append(new)
        h = h + mix
        n = rmsnorm(h, g_ffn[i])
        if i % 2 == 0:
            h = h + swiglu(n, w_ff1[j], w_ff3[j], w_ff2[j])
        else:
            B, T, D = n.shape
            h = h + moe_swiglu(n.reshape(B * T, D), w_router[j], w_exp1[j], w_exp3[j], w_exp2[j]).reshape(B, T, D)
        gate = jax.nn.sigmoid(rmsnorm(h, g_ple[i]) @ w_ple_gate[i])
        h = h + gate * (p[i] @ w_ple_in[i])
    even_state = tuple(jnp.stack([s[m] for s in new_even]) for m in range(5))
    odd_state = tuple(jnp.stack([s[m] for s in new_odd]) for m in range(3))
    return h, even_state, odd_state


def setup_inputs(seed: int = 0) -> dict:
    key = jax.random.key(seed)
    ks = list(jax.random.split(key, 40))
    cnt = [0]

    def nrm(shape, scale=1.0):
        cnt[0] += 1
        return jax.random.normal(ks[cnt[0] - 1], shape, jnp.float32) * scale

    def gain(shape):
        return 1.0 + 0.02 * nrm(shape)

    D = D_MODEL
    in_even = sum(SPLIT_EVEN)
    in_odd = sum(SPLIT_ODD)
    return {
        "x_prompt": nrm((BATCH, SEQ, D)),
        "x_sample": nrm((DEC_BATCH, DEC_SEQ, D)),
        "p_prompt": nrm((DEPTH, BATCH, SEQ, PLE_DIM)),
        "p_sample": nrm((DEPTH, DEC_BATCH, DEC_SEQ, PLE_DIM)),
        "cache_sb_k": nrm((N_EVEN, DEC_BATCH, PAST_LEN, H_SB, HEAD_DIM)),
        "cache_sb_v": nrm((N_EVEN, DEC_BATCH, PAST_LEN, H_SB, HEAD_DIM)),
        "cache_dsa_k": nrm((N_EVEN, DEC_BATCH, PAST_LEN, KV_DSA, HEAD_DIM)),
        "cache_dsa_v": nrm((N_EVEN, DEC_BATCH, PAST_LEN, KV_DSA, HEAD_DIM)),
        "cache_dsa_kidx": nrm((N_EVEN, DEC_BATCH, PAST_LEN, IDX_DIM)),
        "cache_fox_k": nrm((N_ODD, DEC_BATCH, PAST_LEN, H_FOX, HEAD_DIM)),
        "cache_fox_v": nrm((N_ODD, DEC_BATCH, PAST_LEN, H_FOX, HEAD_DIM)),
        "cache_fox_logf": jax.nn.log_sigmoid(2.0 + nrm((N_ODD, DEC_BATCH, PAST_LEN, H_FOX))),
        "g_mix": gain((DEPTH, D)),
        "g_ffn": gain((DEPTH, D)),
        "g_ple": gain((DEPTH, D)),
        "w_in_even": nrm((N_EVEN, D, in_even), D ** -0.5),
        "g_q_dsa": gain((N_EVEN, HEAD_DIM)),
        "g_k_dsa": gain((N_EVEN, HEAD_DIM)),
        "w_out_even": nrm((N_EVEN, (H_SB + H_DSA) * HEAD_DIM, D), ((H_SB + H_DSA) * HEAD_DIM) ** -0.5),
        "w_ff1": nrm((N_EVEN, D, D_FF), D ** -0.5),
        "w_ff3": nrm((N_EVEN, D, D_FF), D ** -0.5),
        "w_ff2": nrm((N_EVEN, D_FF, D), D_FF ** -0.5),
        "w_in_odd": nrm((N_ODD, D, in_odd), D ** -0.5),
        "b_forget": 2.0 + 0.1 * nrm((N_ODD, H_FOX)),
        "g_q_fox": gain((N_ODD, HEAD_DIM)),
        "g_k_fox": gain((N_ODD, HEAD_DIM)),
        "w_out_odd": nrm((N_ODD, H_FOX * HEAD_DIM, D), (H_FOX * HEAD_DIM) ** -0.5),
        "w_router": nrm((N_ODD, D, N_EXPERTS), D ** -0.5),
        "w_exp1": nrm((N_ODD, N_EXPERTS, D, D_EXPERT), D ** -0.5),
        "w_exp3": nrm((N_ODD, N_EXPERTS, D, D_EXPERT), D ** -0.5),
        "w_exp2": nrm((N_ODD, N_EXPERTS, D_EXPERT, D), D_EXPERT ** -0.5),
        "w_ple_in": nrm((DEPTH, PLE_DIM, D), PLE_DIM ** -0.5),
        "w_ple_gate": nrm((DEPTH, D, D), D ** -0.5),
    }


def reference(x_prompt, x_sample, p_prompt, p_sample,
              cache_sb_k, cache_sb_v, cache_dsa_k, cache_dsa_v, cache_dsa_kidx,
              cache_fox_k, cache_fox_v, cache_fox_logf,
              g_mix, g_ffn, g_ple, w_in_even, g_q_dsa, g_k_dsa, w_out_even, w_ff1, w_ff3, w_ff2,
              w_in_odd, b_forget, g_q_fox, g_k_fox, w_out_odd, w_router, w_exp1, w_exp3, w_exp2,
              w_ple_in, w_ple_gate):
    weights = (g_mix, g_ffn, g_ple, w_in_even, g_q_dsa, g_k_dsa, w_out_even, w_ff1, w_ff3, w_ff2,
               w_in_odd, b_forget, g_q_fox, g_k_fox, w_out_odd, w_router, w_exp1, w_exp3, w_exp2,
               w_ple_in, w_ple_gate)
    pos_p = jnp.arange(x_prompt.shape[1], dtype=jnp.int32)
    pos_s = cache_sb_k.shape[2] + jnp.arange(x_sample.shape[1], dtype=jnp.int32)
    y_prompt, even_p, odd_p = trunk(x_prompt, p_prompt, pos_p, None, None, *weights)
    y_sample, even_s, odd_s = trunk(x_sample, p_sample, pos_s,
                                    (cache_sb_k, cache_sb_v, cache_dsa_k, cache_dsa_v, cache_dsa_kidx),
                                    (cache_fox_k, cache_fox_v, cache_fox_logf), *weights)
    sb_k_p, sb_v_p, dsa_k_p, dsa_v_p, dsa_kidx_p = even_p
    fox_k_p, fox_v_p, fox_logf_p = odd_p
    sb_k_s, sb_v_s, dsa_k_s, dsa_v_s, dsa_kidx_s = even_s
    fox_k_s, fox_v_s, fox_logf_s = odd_s
    return (y_prompt, y_sample,
            sb_k_p, sb_v_p, dsa_k_p, dsa_v_p, dsa_kidx_p, fox_k_p, fox_v_p, fox_logf_p,
            sb_k_s, sb_v_s, dsa_k_s, dsa_v_s, dsa_kidx_s, fox_k_s, fox_v_s, fox_logf_s)
```

```python
import functools

import jax
import jax.numpy as jnp
from jax import lax
from jax.experimental import pallas as pl
from jax.experimental.pallas import tpu as pltpu

F32 = jnp.float32
BF16 = jnp.bfloat16
I32 = jnp.int32

EPS = 1e-6
HEAD_DIM = 64
CHUNK = 64
H_SB = 8
H_DSA = 8
KV_DSA = 2
IDX_HEADS = 8
IDX_DIM = 64
H_FOX = 16
DSA_TOPK = 256
ROT_DIM = HEAD_DIM // 4
ROPE_THETA = 500000.0
N_EXPERTS = 8
ATT_SCALE = HEAD_DIM ** -0.5

LANES = 128
MIB = 1024 * 1024
VMEM_LIMIT = 56 * MIB

ROW_TILE = 512
Q_TILE = 128
K_TILE = 256
MOE_TILE = 1024
MOE_CHUNK = 256
MOE_F_TILE = 896

NEG_BIG = -1e30
M_INIT = -1e29
INT_MIN = -(2 ** 31)
INT_MAX = 2 ** 31 - 1
KEY_NEG_INF = -2139095041

E_QA, E_KA, E_VA, E_QI, E_QB, E_KB, E_VB, E_KI, E_WI, E_END = (
    0, 512, 1024, 1536, 2048, 2560, 2688, 2816, 2944, 3072)
O_Q, O_K, O_V, O_F, O_END = 0, 1024, 2048, 3072, 3200
DSA_HEAD_PERM = (0, 4, 1, 5, 2, 6, 3, 7)


def _params(sem, vmem=VMEM_LIMIT):
    return pltpu.CompilerParams(dimension_semantics=sem, vmem_limit_bytes=vmem)


def _dot(a, b):
    return jnp.dot(a, b, preferred_element_type=F32)


def _dot_nt(a, b):
    return lax.dot_general(a, b, (((1,), (1,)), ((), ())), preferred_element_type=F32)


def _rms(x, g):
    return x * lax.rsqrt(jnp.mean(x * x, axis=-1, keepdims=True) + EPS) * g


def _split3(x):
    hi = x.astype(BF16)
    r1 = x - hi.astype(F32)
    mid = r1.astype(BF16)
    lo = (r1 - mid.astype(F32)).astype(BF16)
    return hi, mid, lo


def _sigmoid(x):
    return 1.0 / (1.0 + jnp.exp(-x))


def _log_sigmoid_neg(z):
    return -(jnp.maximum(z, 0.0) + jnp.log1p(jnp.exp(-jnp.abs(z))))


def _proj_body(x_ref, g_ref, w_ref, of_ref, ob_ref):
    y = _rms(x_ref[...], g_ref[...]).astype(BF16)
    r = _dot(y, w_ref[...])
    of_ref[...] = r
    ob_ref[...] = r.astype(BF16)


def _proj(x, g, w, tm):
    n, d = x.shape
    c = w.shape[1]
    return pl.pallas_call(
        _proj_body,
        grid=(n // tm,),
        in_specs=[pl.BlockSpec((tm, d), lambda i: (i, 0)),
                  pl.BlockSpec((1, d), lambda i: (0, 0)),
                  pl.BlockSpec((d, c), lambda i: (0, 0))],
        out_specs=[pl.BlockSpec((tm, c), lambda i: (i, 0)),
                   pl.BlockSpec((tm, c), lambda i: (i, 0))],
        out_shape=[jax.ShapeDtypeStruct((n, c), F32), jax.ShapeDtypeStruct((n, c), BF16)],
        compiler_params=_params(("parallel",)),
        name="proj",
    )(x, g.reshape(1, d), w)


def _headnorm_body(x_ref, g_ref, cos_ref, sin_ref, s_ref, of_ref, ob_ref, *, rotary):
    x = x_ref[...]
    hi, mid, lo = _split3(x * x)
    s = s_ref[...]
    ms = (_dot(hi, s) + _dot(mid, s) + _dot(lo, s)) * (1.0 / HEAD_DIM)
    y = x * lax.rsqrt(ms + EPS) * g_ref[0]
    if rotary:
        lane = lax.broadcasted_iota(I32, (1, LANES), 1) % HEAD_DIM
        partner = jnp.where(lane < ROT_DIM // 2,
                            pltpu.roll(y, LANES - ROT_DIM // 2, 1),
                            pltpu.roll(y, ROT_DIM // 2, 1))
        y = y * cos_ref[...] + partner * sin_ref[...]
    of_ref[...] = y
    ob_ref[...] = y.astype(BF16)


def _headnorm(x, col0, gains, cos, sin, tm):
    n = x.shape[0]
    nblk = gains.shape[0]
    cb0 = col0 // LANES
    rotary = cos is not None
    if not rotary:
        cos = jnp.zeros((8, LANES), F32)
        sin = cos
        tab_spec = pl.BlockSpec((8, LANES), lambda i, j: (0, 0))
    else:
        nt = cos.shape[0] // tm
        tab_spec = pl.BlockSpec((tm, LANES), lambda i, j: (i % nt, 0))
    r = lax.broadcasted_iota(I32, (LANES, LANES), 0) // HEAD_DIM
    c = lax.broadcasted_iota(I32, (LANES, LANES), 1) // HEAD_DIM
    seg = (r == c).astype(BF16)
    return pl.pallas_call(
        functools.partial(_headnorm_body, rotary=rotary),
        grid=(n // tm, nblk),
        in_specs=[pl.BlockSpec((tm, LANES), lambda i, j: (i, cb0 + j)),
                  pl.BlockSpec((1, 1, LANES), lambda i, j: (j, 0, 0)),
                  tab_spec, tab_spec,
                  pl.BlockSpec((LANES, LANES), lambda i, j: (0, 0))],
        out_specs=[pl.BlockSpec((tm, LANES), lambda i, j: (i, j)),
                   pl.BlockSpec((tm, LANES), lambda i, j: (i, j))],
        out_shape=[jax.ShapeDtypeStruct((n, nblk * LANES), F32),
                   jax.ShapeDtypeStruct((n, nblk * LANES), BF16)],
        compiler_params=_params(("parallel", "parallel")),
        name="headnorm",
    )(x, gains.reshape(nblk, 1, LANES), cos, sin, seg)


def _res_body(*refs, n_in):
    h_ref, o_ref = refs[0], refs[-1]
    acc = h_ref[...]
    for t in range(n_in):
        acc = acc + _dot(refs[1 + t][...], refs[1 + n_in + t][...])
    o_ref[...] = acc


def _residual_matmul(h, acts, ws, tm):
    n, d = h.shape
    n_in = len(acts)
    in_specs = [pl.BlockSpec((tm, d), lambda i: (i, 0))]
    in_specs += [pl.BlockSpec((tm, a.shape[1]), lambda i: (i, 0)) for a in acts]
    in_specs += [pl.BlockSpec(w.shape, lambda i: (0, 0)) for w in ws]
    return pl.pallas_call(
        functools.partial(_res_body, n_in=n_in),
        grid=(n // tm,),
        in_specs=in_specs,
        out_specs=pl.BlockSpec((tm, d), lambda i: (i, 0)),
        out_shape=jax.ShapeDtypeStruct((n, d), F32),
        compiler_params=_params(("parallel",)),
        name="residual_matmul",
    )(h, *acts, *ws)


def _ffn_body(h_ref, g_ref, w1_ref, w3_ref, w2_ref, o_ref):
    x = h_ref[...]
    xn = _rms(x, g_ref[...]).astype(BF16)
    a = _dot(xn, w1_ref[...])
    b = _dot(xn, w3_ref[...])
    hm = (a * _sigmoid(a) * b).astype(BF16)
    o_ref[...] = x + _dot(hm, w2_ref[...])


def _ffn(h, g, w1, w3, w2, tm):
    n, d = h.shape
    f = w1.shape[1]
    once = pl.Buffered(1)
    return pl.pallas_call(
        _ffn_body,
        grid=(n // tm,),
        in_specs=[pl.BlockSpec((tm, d), lambda i: (i, 0)),
                  pl.BlockSpec((1, d), lambda i: (0, 0)),
                  pl.BlockSpec((d, f), lambda i: (0, 0), pipeline_mode=once),
                  pl.BlockSpec((d, f), lambda i: (0, 0), pipeline_mode=once),
                  pl.BlockSpec((f, d), lambda i: (0, 0), pipeline_mode=once)],
        out_specs=pl.BlockSpec((tm, d), lambda i: (i, 0)),
        out_shape=jax.ShapeDtypeStruct((n, d), F32),
        compiler_params=_params(("parallel",)),
        name="ffn",
    )(h, g.reshape(1, d), w1, w3, w2)


def _ple_body(h_ref, g_ref, wg_ref, p_ref, wp_ref, o_ref):
    x = h_ref[...]
    xn = _rms(x, g_ref[...]).astype(BF16)
    gate = _sigmoid(_dot(xn, wg_ref[...]))
    o_ref[...] = x + gate * _dot(p_ref[...].astype(BF16), wp_ref[...])


def _ple(h, g, wg, p, wp, tm):
    n, d = h.shape
    e = p.shape[1]
    return pl.pallas_call(
        _ple_body,
        grid=(n // tm,),
        in_specs=[pl.BlockSpec((tm, d), lambda i: (i, 0)),
                  pl.BlockSpec((1, d), lambda i: (0, 0)),
                  pl.BlockSpec((d, d), lambda i: (0, 0)),
                  pl.BlockSpec((tm, e), lambda i: (i, 0)),
                  pl.BlockSpec((e, d), lambda i: (0, 0))],
        out_specs=pl.BlockSpec((tm, d), lambda i: (i, 0)),
        out_shape=jax.ShapeDtypeStruct((n, d), F32),
        compiler_params=_params(("parallel",)),
        name="ple",
    )(h, g.reshape(1, d), wg, p, wp)


def _sb_body(q_ref, k_ref, v_ref, o_ref, *, tq, tk, q_pos0, n_kb_total):
    row0 = q_pos0 + pl.program_id(2) * tq
    q = q_ref[0]
    lane = lax.broadcasted_iota(I32, (1, LANES), 1)
    lo_half = lane < HEAD_DIM
    zero = jnp.zeros_like(q)
    q_halves = (jnp.where(lo_half, q, zero), jnp.where(lo_half, zero, q))
    qpos = row0 + lax.broadcasted_iota(I32, (tq, 1), 0)
    nkb = jnp.minimum((row0 + tq - 1 + tk - 1) // tk, n_kb_total)
    later = (lax.broadcasted_iota(I32, (tk, tk), 0) >
             lax.broadcasted_iota(I32, (tk, tk), 1)).astype(BF16)

    def body(j, carry):
        accs, cs = carry
        ks = pl.multiple_of((nkb - 1 - j) * tk, tk)
        k = k_ref[0, pl.ds(ks, tk), :]
        v = v_ref[0, pl.ds(ks, tk), :]
        vis = (ks + lax.broadcasted_iota(I32, (1, tk), 1)) < qpos
        new_accs, new_cs = [], []
        for half in range(2):
            z = _dot_nt(q_halves[half], k) * ATT_SCALE
            lk = jnp.where(vis, _log_sigmoid_neg(z), 0.0)
            hi = lk.astype(BF16)
            lo = (lk - hi.astype(F32)).astype(BF16)
            between = _dot(hi, later) + _dot(lo, later)
            w = jnp.where(vis, jnp.exp(z + lk + between + cs[half]), 0.0)
            new_accs.append(accs[half] + _dot(w.astype(BF16), v))
            new_cs.append(cs[half] + jnp.sum(lk, axis=1, keepdims=True))
        return tuple(new_accs), tuple(new_cs)

    acc0 = jnp.zeros((tq, LANES), F32)
    c0 = jnp.zeros((tq, 1), F32)
    accs, _ = lax.fori_loop(0, nkb, body, ((acc0, acc0), (c0, c0)))
    o_ref[0] = jnp.where(lo_half, accs[0], accs[1]).astype(BF16)


def _sb_attention(q, q_cb, k, k_cb, v, v_cb, n_pairs, q_pos0, tq, tk):
    b, t = q.shape[:2]
    l = k.shape[1]
    return pl.pallas_call(
        functools.partial(_sb_body, tq=tq, tk=tk, q_pos0=q_pos0, n_kb_total=l // tk),
        grid=(b, n_pairs, t // tq),
        in_specs=[pl.BlockSpec((1, tq, LANES), lambda bi, p, i: (bi, i, q_cb + p)),
                  pl.BlockSpec((1, l, LANES), lambda bi, p, i: (bi, 0, k_cb + p)),
                  pl.BlockSpec((1, l, LANES), lambda bi, p, i: (bi, 0, v_cb + p))],
        out_specs=pl.BlockSpec((1, tq, LANES), lambda bi, p, i: (bi, i, p)),
        out_shape=jax.ShapeDtypeStruct((b, t, n_pairs * LANES), BF16),
        compiler_params=_params(("parallel", "parallel", "parallel")),
        name="sb_attention",
    )(q, k, v)


def _softmax_step(s, v, m, l, acc):
    m_new = jnp.maximum(m, jnp.max(s, axis=1, keepdims=True))
    alpha = jnp.exp(m - m_new)
    p = jnp.exp(s - m_new)
    l_new = alpha * l + jnp.sum(p, axis=1, keepdims=True)
    acc_new = alpha * acc + _dot(p.astype(BF16), v)
    return m_new, l_new, acc_new


def _fox_body(q_ref, k_ref, v_ref, fq_ref, fk_ref, o_ref, *, tq, tk, q_pos0, n_kb_total):
    row0 = q_pos0 + pl.program_id(2) * tq
    q = (q_ref[0].astype(F32) * ATT_SCALE).astype(BF16)
    lane = lax.broadcasted_iota(I32, (1, LANES), 1)
    lo_half = lane < HEAD_DIM
    zero = jnp.zeros_like(q)
    q_halves = (jnp.where(lo_half, q, zero), jnp.where(lo_half, zero, q))
    qpos = row0 + lax.broadcasted_iota(I32, (tq, 1), 0)
    fq = fq_ref[0, 0]
    nkb = jnp.minimum((row0 + tq + tk - 1) // tk, n_kb_total)

    def body(j, carry):
        ks = pl.multiple_of(j * tk, tk)
        k = k_ref[0, pl.ds(ks, tk), :]
        v = v_ref[0, pl.ds(ks, tk), :]
        vis = (ks + lax.broadcasted_iota(I32, (1, tk), 1)) <= qpos
        out = []
        for half in range(2):
            m, l, acc = carry[half]
            fk = fk_ref[0, 0, half:half + 1, pl.ds(ks, tk)]
            s = _dot_nt(q_halves[half], k) + (fq[:, half:half + 1] - fk)
            s = jnp.where(vis, s, NEG_BIG)
            out.append(_softmax_step(s, v, m, l, acc))
        return tuple(out)

    init = (jnp.full((tq, 1), M_INIT, F32), jnp.zeros((tq, 1), F32), jnp.zeros((tq, LANES), F32))
    (_, l0, a0), (_, l1, a1) = lax.fori_loop(0, nkb, body, (init, init))
    o_ref[0] = jnp.where(lo_half, a0 / l0, a1 / l1).astype(BF16)


def _fox_attention(q, q_cb, k, k_cb, v, v_cb, fq, fk, q_pos0, tq, tk):
    b, t = q.shape[:2]
    l = k.shape[1]
    n_pairs = H_FOX // 2
    return pl.pallas_call(
        functools.partial(_fox_body, tq=tq, tk=tk, q_pos0=q_pos0, n_kb_total=l // tk),
        grid=(b, n_pairs, t // tq),
        in_specs=[pl.BlockSpec((1, tq, LANES), lambda bi, p, i: (bi, i, q_cb + p)),
                  pl.BlockSpec((1, l, LANES), lambda bi, p, i: (bi, 0, k_cb + p)),
                  pl.BlockSpec((1, l, LANES), lambda bi, p, i: (bi, 0, v_cb + p)),
                  pl.BlockSpec((1, 1, tq, 2), lambda bi, p, i: (bi, p, i, 0)),
                  pl.BlockSpec((1, 1, 2, l), lambda bi, p, i: (bi, p, 0, 0))],
        out_specs=pl.BlockSpec((1, tq, LANES), lambda bi, p, i: (bi, i, p)),
        out_shape=jax.ShapeDtypeStruct((b, t, n_pairs * LANES), BF16),
        compiler_params=_params(("parallel", "parallel", "parallel")),
        name="fox_attention",
    )(q, k, v, fq, fk)


def _forget_body(raw_ref, b_ref, logf_ref, cum_ref, carry_ref, *, tb, n_given):
    j = pl.program_id(1)

    @pl.when(j == 0)
    def _():
        carry_ref[...] = jnp.zeros_like(carry_ref)

    raw = raw_ref[0]
    z = raw + b_ref[...]
    computed = _log_sigmoid_neg(-z)
    row = j * tb + lax.broadcasted_iota(I32, (tb, 1), 0)
    logf = jnp.where(row < n_given, raw, computed)
    incl = (lax.broadcasted_iota(I32, (tb, tb), 1) <=
            lax.broadcasted_iota(I32, (tb, tb), 0)).astype(BF16)
    hi, mid, lo = _split3(logf)
    cum = _dot(incl, hi) + _dot(incl, mid) + _dot(incl, lo) + carry_ref[0:1, :]
    logf_ref[0] = logf
    cum_ref[0] = cum
    carry_ref[0:1, :] = cum[tb - 1:tb, :]


def _forget_cumsum(raw, bias, n_given, tb):
    raw = jnp.pad(raw, ((0, 0), (0, 0), (0, LANES - raw.shape[2])))
    bias = jnp.pad(bias, (0, LANES - bias.shape[0]))
    b, l, h = raw.shape
    return pl.pallas_call(
        functools.partial(_forget_body, tb=tb, n_given=n_given),
        grid=(b, l // tb),
        in_specs=[pl.BlockSpec((1, tb, h), lambda bi, j: (bi, j, 0)),
                  pl.BlockSpec((1, h), lambda bi, j: (0, 0))],
        out_specs=[pl.BlockSpec((1, tb, h), lambda bi, j: (bi, j, 0)),
                   pl.BlockSpec((1, tb, h), lambda bi, j: (bi, j, 0))],
        out_shape=[jax.ShapeDtypeStruct((b, l, h), F32), jax.ShapeDtypeStruct((b, l, h), F32)],
        scratch_shapes=[pltpu.VMEM((8, h), F32)],
        compiler_params=_params(("parallel", "arbitrary")),
        name="forget_cumsum",
    )(raw, bias.reshape(1, h))


def _dsa_body(q_ref, qi_ref, wi_ref, k_ref, v_ref, ki_ref, o_ref,
              key_s, bias_s, qi_s, *, tq, tk, q_pos0, l_valid, n_kb_total, n_sel):
    row0 = q_pos0 + pl.program_id(1) * tq
    lane = lax.broadcasted_iota(I32, (1, LANES), 1)
    lo_half = lane < HEAD_DIM
    qpos = row0 + lax.broadcasted_iota(I32, (tq, 1), 0)
    qchunk = qpos // CHUNK
    kend = jnp.minimum(((row0 + tq - 1) // CHUNK + 1) * CHUNK, l_valid)
    nkb = jnp.minimum(jnp.maximum((kend + tk - 1) // tk, (n_sel + tk - 1) // tk), n_kb_total)

    for p in range(IDX_HEADS // 2):
        blk = qi_ref[0, :, p * LANES:(p + 1) * LANES]
        zero = jnp.zeros_like(blk)
        qi_s[2 * p] = jnp.where(lo_half, blk, zero)
        qi_s[2 * p + 1] = jnp.where(lo_half, zero, blk)
    wsc = wi_ref[0] * (IDX_DIM ** -0.5 * IDX_HEADS ** -0.5)

    def score_body(j, _):
        ks = pl.multiple_of(j * tk, tk)
        ki = ki_ref[0, pl.ds(ks, tk), :]
        score = jnp.zeros((tq, tk), F32)
        for h in range(IDX_HEADS):
            rel = jnp.maximum(_dot_nt(qi_s[h], ki), 0.0)
            score = score + wsc[:, h:h + 1] * rel
        kpos = ks + lax.broadcasted_iota(I32, (1, tk), 1)
        vis = ((kpos // CHUNK) <= qchunk) & (kpos < l_valid)
        score = jnp.where(vis, score, -jnp.inf)
        bits = lax.bitcast_convert_type(score, I32)
        key = bits ^ ((bits >> 31) & INT_MAX)
        key_s[:, pl.ds(ks, tk)] = jnp.where(bits == INT_MIN, 0, key)
        return 0

    lax.fori_loop(0, nkb, score_body, 0)

    def count(pred):
        def cbody(j, acc):
            blk = key_s[:, pl.ds(pl.multiple_of(j * tk, tk), tk)]
            hit = jnp.where(pred(blk, j), 1.0, 0.0)
            for c in range(tk // LANES):
                acc = acc + hit[:, c * LANES:(c + 1) * LANES]
            return acc
        acc = lax.fori_loop(0, nkb, cbody, jnp.zeros((tq, LANES), F32))
        return jnp.sum(acc, axis=1, keepdims=True)

    n_sel_f = float(n_sel)
    zero_col = jnp.zeros((tq, 1), I32)
    thr = jnp.where(count(lambda blk, j: blk >= zero_col) >= n_sel_f, 0, INT_MIN)

    def bit_body(b, thr):
        cand = thr | (1 << (30 - b))
        return jnp.where(count(lambda blk, j: blk >= cand) >= n_sel_f, cand, thr)

    thr = lax.fori_loop(0, 31, bit_body, thr)
    c_gt = count(lambda blk, j: blk > thr)
    c_ge = count(lambda blk, j: blk >= thr)
    need = n_sel_f - c_gt
    tied = (c_ge > n_sel_f) & (thr > KEY_NEG_INF)
    any_tied = jnp.max(jnp.where(tied, 1.0, 0.0)) > 0.0

    def tie_index_bound():
        def kidx(j):
            return j * tk + lax.broadcasted_iota(I32, (1, tk), 1)

        n_bits = (n_kb_total * tk - 1).bit_length()

        def jbody(b, jb):
            cand = jb | (1 << (n_bits - 1 - b))
            cnt = count(lambda blk, j: (blk == thr) & (kidx(j) < cand))
            return jnp.where(cnt < need, cand, jb)

        jb = lax.fori_loop(0, n_bits, jbody, jnp.zeros((tq, 1), I32))
        return jnp.where(tied, jb, INT_MAX)

    jbound = lax.cond(any_tied, tie_index_bound, lambda: jnp.full((tq, 1), INT_MAX, I32))

    def bias_body(j, _):
        ks = pl.multiple_of(j * tk, tk)
        blk = key_s[:, pl.ds(ks, tk)]
        kpos = ks + lax.broadcasted_iota(I32, (1, tk), 1)
        sel = (blk > thr) | ((blk == thr) & (kpos <= jbound))
        sel = sel & (blk > KEY_NEG_INF)
        bias_s[:, pl.ds(ks, tk)] = jnp.where(sel, 0.0, NEG_BIG)
        return 0

    lax.fori_loop(0, nkb, bias_body, 0)

    for p in range(H_DSA // 2):
        blk = (q_ref[0, :, p * LANES:(p + 1) * LANES].astype(F32) * ATT_SCALE).astype(BF16)
        zero = jnp.zeros_like(blk)
        q_halves = (jnp.where(lo_half, blk, zero), jnp.where(lo_half, zero, blk))
        outs = []
        for half in range(2):
            qh = q_halves[half]

            def att_body(j, carry, qh=qh):
                ks = pl.multiple_of(j * tk, tk)
                s = _dot_nt(qh, k_ref[0, pl.ds(ks, tk), :]) + bias_s[:, pl.ds(ks, tk)]
                return _softmax_step(s, v_ref[0, pl.ds(ks, tk), :], *carry)

            init = (jnp.full((tq, 1), M_INIT, F32), jnp.zeros((tq, 1), F32),
                    jnp.zeros((tq, LANES), F32))
            _, l, acc = lax.fori_loop(0, nkb, att_body, init)
            outs.append(acc / l)
        o_ref[0, :, p * LANES:(p + 1) * LANES] = jnp.where(lo_half, outs[0], outs[1]).astype(BF16)


def _dsa_attention(q, qi, qi_cb, wi, wi_cb, k, k_cb, v, v_cb, ki, ki_cb,
                   q_pos0, l_valid, n_sel, tq, tk):
    b, t = q.shape[:2]
    l = k.shape[1]
    width = H_DSA * HEAD_DIM
    return pl.pallas_call(
        functools.partial(_dsa_body, tq=tq, tk=tk, q_pos0=q_pos0, l_valid=l_valid,
                          n_kb_total=l // tk, n_sel=n_sel),
        grid=(b, t // tq),
        in_specs=[pl.BlockSpec((1, tq, width), lambda bi, i: (bi, i, 0)),
                  pl.BlockSpec((1, tq, width), lambda bi, i: (bi, i, qi_cb)),
                  pl.BlockSpec((1, tq, LANES), lambda bi, i: (bi, i, wi_cb)),
                  pl.BlockSpec((1, l, LANES), lambda bi, i: (bi, 0, k_cb)),
                  pl.BlockSpec((1, l, LANES), lambda bi, i: (bi, 0, v_cb)),
                  pl.BlockSpec((1, l, LANES), lambda bi, i: (bi, 0, ki_cb))],
        out_specs=pl.BlockSpec((1, tq, width), lambda bi, i: (bi, i, 0)),
        out_shape=jax.ShapeDtypeStruct((b, t, width), BF16),
        scratch_shapes=[pltpu.VMEM((tq, l), I32), pltpu.VMEM((tq, l), F32),
                        pltpu.VMEM((IDX_HEADS, tq, LANES), BF16)],
        compiler_params=_params(("parallel", "parallel")),
        name="dsa_attention",
    )(q, qi, wi, k, v, ki)


def _moe_body(h_ref, g_ref, wr_ref, w1_ref, w3_ref, w2_ref, o_ref,
              xn_s, xg_s, ye_s, rank_s, gate_s, rank_t_s, *, tm, ch, n_fc):
    e = pl.program_id(1)
    fc = pl.program_id(2)
    lane = lax.broadcasted_iota(I32, (1, LANES), 1)

    @pl.when((e == 0) & (fc == 0))
    def _route():
        x = h_ref[...]
        xn = _rms(x, g_ref[...])
        xn_s[...] = xn.astype(BF16)
        o_ref[...] = x
        x3 = _split3(xn)
        logits = jnp.zeros((tm, LANES), F32)
        for a, b in ((2, 0), (0, 2), (1, 1), (1, 0), (0, 1), (0, 0)):
            logits = logits + _dot(x3[a], wr_ref[b])
        lane_f = lane.astype(F32)
        logits = jnp.where(lane < N_EXPERTS, logits, -jnp.inf)
        m1 = jnp.max(logits, axis=1, keepdims=True)
        i1 = jnp.min(jnp.where(logits == m1, lane_f, float(LANES)), axis=1, keepdims=True)
        rest = jnp.where(lane_f == i1, -jnp.inf, logits)
        m2 = jnp.max(rest, axis=1, keepdims=True)
        i2 = jnp.min(jnp.where(rest == m2, lane_f, float(LANES)), axis=1, keepdims=True)
        e2 = jnp.exp(m2 - m1)
        g1 = 1.0 / (1.0 + e2)
        g2 = e2 / (1.0 + e2)
        sel1 = lane_f == i1
        sel2 = lane_f == i2
        gate_s[...] = jnp.where(sel1, g1, 0.0) + jnp.where(sel2, g2, 0.0)
        sel = jnp.where(sel1 | sel2, 1.0, 0.0)
        incl = (lax.broadcasted_iota(I32, (tm, tm), 1) <=
                lax.broadcasted_iota(I32, (tm, tm), 0)).astype(BF16)
        rank = _dot(incl, sel.astype(BF16)) * sel
        rank_s[...] = rank
        rank_t_s[...] = rank.T

    rank_row = rank_t_s[pl.ds(e, 1), :]
    cnt = jnp.max(rank_row).astype(I32)
    nch = (cnt + ch - 1) // ch

    @pl.when(fc == 0)
    def _gather():
        def gbody(c, _):
            base = pl.multiple_of(c * ch, ch)
            slot = (base + 1 + lax.broadcasted_iota(I32, (ch, 1), 0)).astype(F32)
            pick = jnp.where(rank_row == slot, 1.0, 0.0).astype(BF16)
            xg_s[pl.ds(base, ch), :] = _dot(pick, xn_s[...]).astype(BF16)
            return 0
        lax.fori_loop(0, nch, gbody, 0)

    def fbody(c, _):
        base = pl.multiple_of(c * ch, ch)
        xg = xg_s[pl.ds(base, ch), :]
        a = _dot(xg, w1_ref[0])
        b = _dot(xg, w3_ref[0])
        part = _dot((a * _sigmoid(a) * b).astype(BF16), w2_ref[0])

        @pl.when(fc == 0)
        def _():
            ye_s[pl.ds(base, ch), :] = part

        @pl.when(fc != 0)
        def _():
            ye_s[pl.ds(base, ch), :] += part
        return 0

    lax.fori_loop(0, nch, fbody, 0)

    @pl.when(fc == n_fc - 1)
    def _scatter():
        here = lane == e
        rank_col = jnp.sum(jnp.where(here, rank_s[...], 0.0), axis=1, keepdims=True)
        gate_col = jnp.sum(jnp.where(here, gate_s[...], 0.0), axis=1, keepdims=True)

        def sbody(c, _):
            base = pl.multiple_of(c * ch, ch)
            slot = (base + 1 + lax.broadcasted_iota(I32, (1, ch), 1)).astype(F32)
            place = jnp.where(rank_col == slot, 1.0, 0.0).astype(BF16)
            ye = ye_s[pl.ds(base, ch), :]
            hi = ye.astype(BF16)
            lo = (ye - hi.astype(F32)).astype(BF16)
            o_ref[...] += gate_col * (_dot(place, hi) + _dot(place, lo))
            return 0
        lax.fori_loop(0, nch, sbody, 0)


def _moe(h, g, wr3, w1, w3, w2, tm, ch, tf):
    n, d = h.shape
    ne, _, f = w1.shape
    n_fc = f // tf
    return pl.pallas_call(
        functools.partial(_moe_body, tm=tm, ch=ch, n_fc=n_fc),
        grid=(n // tm, ne, n_fc),
        in_specs=[pl.BlockSpec((tm, d), lambda i, e, c: (i, 0)),
                  pl.BlockSpec((1, d), lambda i, e, c: (0, 0)),
                  pl.BlockSpec((3, d, LANES), lambda i, e, c: (0, 0, 0)),
                  pl.BlockSpec((1, d, tf), lambda i, e, c: (e, 0, c)),
                  pl.BlockSpec((1, d, tf), lambda i, e, c: (e, 0, c)),
                  pl.BlockSpec((1, tf, d), lambda i, e, c: (e, c, 0))],
        out_specs=pl.BlockSpec((tm, d), lambda i, e, c: (i, 0)),
        out_shape=jax.ShapeDtypeStruct((n, d), F32),
        scratch_shapes=[pltpu.VMEM((tm, d), BF16), pltpu.VMEM((tm, d), BF16),
                        pltpu.VMEM((tm, d), F32), pltpu.VMEM((tm, LANES), F32),
                        pltpu.VMEM((tm, LANES), F32), pltpu.VMEM((LANES, tm), F32)],
        compiler_params=_params(("parallel", "arbitrary", "arbitrary")),
        name="moe",
    )(h, g.reshape(1, d), wr3, w1, w3, w2)


def _round_up(x, m):
    return (x + m - 1) // m * m


def _pad_rows(a, l_pad):
    return jnp.pad(a, ((0, 0), (0, l_pad - a.shape[1]), (0, 0)))


def _prep_weights(g_q_dsa, g_k_dsa, g_q_fox, g_k_fox, w_in_even, w_out_even, w_in_odd, w_router):
    d = w_in_even.shape[0]
    perm = jnp.asarray(DSA_HEAD_PERM)
    qa, ka, va, qb, kb, vb, qi, ki, wi = jnp.split(
        w_in_even, [512, 1024, 1536, 2048, 2176, 2304, 2816, 2880], axis=1)
    qb = qb.reshape(d, H_DSA, HEAD_DIM)[:, perm].reshape(d, H_DSA * HEAD_DIM)
    wi = jnp.pad(wi, ((0, 0), (0, LANES - IDX_HEADS)))
    w_even = jnp.concatenate([qa, ka, va, qi, qb, kb, vb, ki, ki, wi], axis=1).astype(BF16)
    w_out_sb = w_out_even[:H_SB * HEAD_DIM].astype(BF16)
    w_out_dsa = w_out_even[H_SB * HEAD_DIM:].reshape(H_DSA, HEAD_DIM, d)[perm]
    w_out_dsa = w_out_dsa.reshape(H_DSA * HEAD_DIM, d).astype(BF16)
    w_odd = jnp.pad(w_in_odd, ((0, 0), (0, O_END - w_in_odd.shape[1]))).astype(BF16)
    gains_even = jnp.concatenate([jnp.tile(g_q_dsa, (4, 2)), jnp.tile(g_k_dsa, (1, 2))], axis=0)
    gains_odd = jnp.concatenate([jnp.tile(g_q_fox, (8, 2)), jnp.tile(g_k_fox, (8, 2))], axis=0)
    wr = jnp.pad(w_router, ((0, 0), (0, LANES - N_EXPERTS)))
    wr_hi = wr.astype(BF16)
    wr_r1 = wr - wr_hi.astype(F32)
    wr_mid = wr_r1.astype(BF16)
    wr_lo = (wr_r1 - wr_mid.astype(F32)).astype(BF16)
    return dict(w_even=w_even, w_out_sb=w_out_sb, w_out_dsa=w_out_dsa, w_odd=w_odd,
                gains_even=gains_even, gains_odd=gains_odd,
                wr3=jnp.stack([wr_hi, wr_mid, wr_lo]))


def _rotary_tables(pos):
    half = ROT_DIM // 2
    inv = ROPE_THETA ** (-jnp.arange(half, dtype=F32) / half)
    ang = pos.astype(F32)[:, None] * inv[None, :]
    cos, sin = jnp.cos(ang), jnp.sin(ang)
    t = pos.shape[0]
    pad = HEAD_DIM - ROT_DIM
    cos_h = jnp.concatenate([cos, cos, jnp.ones((t, pad), F32)], axis=1)
    sin_h = jnp.concatenate([-sin, sin, jnp.zeros((t, pad), F32)], axis=1)
    return jnp.tile(cos_h, (1, 2)), jnp.tile(sin_h, (1, 2))


def _trunk(x, p, pos0, past_even, past_odd, w, wb):
    b, t, d = x.shape
    n = b * t
    tm = min(ROW_TILE, n)
    tq = min(Q_TILE, t)
    tk = K_TILE
    has_past = past_even is not None
    past_len = past_even[0].shape[1] if has_past else 0
    l_valid = past_len + t
    l_pad = _round_up(l_valid, tk)
    n_sel = min(DSA_TOPK, l_valid // 4)
    h = x.reshape(n, d)

    def keys(cache, new):
        if not has_past:
            return new
        full = jnp.concatenate([cache.reshape(b, past_len, -1).astype(BF16), new], axis=1)
        return _pad_rows(full, l_pad)

    pf, pb = _proj(h, w["g_mix"][0], wb["w_even"], tm)
    cos, sin = _rotary_tables(pos0 + jnp.arange(t, dtype=I32))
    if t % tm:
        cos, sin = jnp.tile(cos, (tm // t, 1)), jnp.tile(sin, (tm // t, 1))
    qkf, qkb = _headnorm(pf, E_QB, wb["gains_even"], cos, sin, tm)
    pf3, pb3 = pf.reshape(b, t, E_END), pb.reshape(b, t, E_END)
    qkf3, qkb3 = qkf.reshape(b, t, -1), qkb.reshape(b, t, -1)
    kb_new_f = qkf3[:, :, 512:640]
    if has_past:
        c_sbk, c_sbv, c_dk, c_dv, c_ki = past_even
        ka, ka_cb = keys(c_sbk, pb3[:, :, E_KA:E_VA]), 0
        va, va_cb = keys(c_sbv, pb3[:, :, E_VA:E_QI]), 0
        kb, kb_cb = keys(c_dk, qkb3[:, :, 512:640]), 0
        vb, vb_cb = keys(c_dv, pb3[:, :, E_VB:E_KI]), 0
        ki2 = jnp.concatenate([c_ki, c_ki], axis=-1)
        ki, ki_cb = keys(ki2, pb3[:, :, E_KI:E_WI]), 0
    else:
        ka, ka_cb = pb3, E_KA // LANES
        va, va_cb = pb3, E_VA // LANES
        kb, kb_cb = qkb3, 512 // LANES
        vb, vb_cb = pb3, E_VB // LANES
        ki, ki_cb = pb3, E_KI // LANES
    oa = _sb_attention(pb3, E_QA // LANES, ka, ka_cb, va, va_cb, H_SB // 2, pos0, tq, tk)
    ob = _dsa_attention(qkb3, pb3, E_QI // 512, pf3, E_WI // LANES, kb, kb_cb, vb, vb_cb,
                        ki, ki_cb, pos0, l_valid, n_sel, tq, tk)
    h = _residual_matmul(h, [oa.reshape(n, -1), ob.reshape(n, -1)],
                         [wb["w_out_sb"], wb["w_out_dsa"]], tm)
    h = _ffn(h, w["g_ffn"][0], wb["w_ff1"], wb["w_ff3"], wb["w_ff2"], tm)
    h = _ple(h, w["g_ple"][0], wb["w_ple_gate"][0], p[0].reshape(n, -1), wb["w_ple_in"][0], tm)
    even_state = (pf3[:, :, E_KA:E_VA].reshape(b, t, H_SB, HEAD_DIM),
                  pf3[:, :, E_VA:E_QI].reshape(b, t, H_SB, HEAD_DIM),
                  kb_new_f.reshape(b, t, KV_DSA, HEAD_DIM),
                  pf3[:, :, E_VB:E_KI].reshape(b, t, KV_DSA, HEAD_DIM),
                  pf3[:, :, E_KI:E_KI + IDX_DIM])

    pf, pb = _proj(h, w["g_mix"][1], wb["w_odd"], tm)
    qkf, qkb = _headnorm(pf, O_Q, wb["gains_odd"], None, None, tm)
    pf3, pb3 = pf.reshape(b, t, O_END), pb.reshape(b, t, O_END)
    qkf3, qkb3 = qkf.reshape(b, t, -1), qkb.reshape(b, t, -1)
    gate_pre = pf3[:, :, O_F:O_F + H_FOX]
    if has_past:
        c_fk, c_fv, c_lf = past_odd
        kf, kf_cb = keys(c_fk, qkb3[:, :, 1024:2048]), 0
        vf, vf_cb = keys(c_fv, pb3[:, :, O_V:O_F]), 0
        raw = _pad_rows(jnp.concatenate([c_lf, gate_pre], axis=1), l_pad)
    else:
        kf, kf_cb = qkb3, 1024 // LANES
        vf, vf_cb = pb3, O_V // LANES
        raw = gate_pre
    logf, cum = _forget_cumsum(raw, w["b_forget"][0], past_len, min(tk, l_pad))
    logf, cum = logf[:, :, :H_FOX], cum[:, :, :H_FOX]
    fq = cum[:, past_len:l_valid].reshape(b, t, H_FOX // 2, 2).transpose(0, 2, 1, 3)
    fk = cum.reshape(b, l_pad, H_FOX // 2, 2).transpose(0, 2, 3, 1)
    of = _fox_attention(qkb3, 0, kf, kf_cb, vf, vf_cb, fq, fk, pos0, tq, tk)
    h = _residual_matmul(h, [of.reshape(n, -1)], [wb["w_out_odd"]], tm)
    h = _moe(h, w["g_ffn"][1], wb["wr3"], wb["w_exp1"], wb["w_exp3"], wb["w_exp2"],
             min(MOE_TILE, n), min(MOE_CHUNK, n), MOE_F_TILE)
    h = _ple(h, w["g_ple"][1], wb["w_ple_gate"][1], p[1].reshape(n, -1), wb["w_ple_in"][1], tm)
    odd_state = (qkf3[:, :, 1024:2048].reshape(b, t, H_FOX, HEAD_DIM),
                 pf3[:, :, O_V:O_F].reshape(b, t, H_FOX, HEAD_DIM),
                 logf[:, past_len:l_valid])
    return h.reshape(b, t, d), even_state, odd_state


def kernel(x_prompt, x_sample, p_prompt, p_sample, cache_sb_k, cache_sb_v, cache_dsa_k, cache_dsa_v, cache_dsa_kidx, cache_fox_k, cache_fox_v, cache_fox_logf, g_mix, g_ffn, g_ple, w_in_even, g_q_dsa, g_k_dsa, w_out_even, w_ff1, w_ff3, w_ff2, w_in_odd, b_forget, g_q_fox, g_k_fox, w_out_odd, w_router, w_exp1, w_exp3, w_exp2, w_ple_in, w_ple_gate):
    assert g_mix.shape[0] == 2, "two layers: one even (stick-breaking + DSA), one odd (FoX + experts)"
    past_len = cache_sb_k.shape[2]
    w = dict(g_mix=g_mix, g_ffn=g_ffn, g_ple=g_ple, b_forget=b_forget)
    wb = _prep_weights(g_q_dsa[0], g_k_dsa[0], g_q_fox[0], g_k_fox[0],
                       w_in_even[0], w_out_even[0], w_in_odd[0], w_router[0])
    wb.update(w_ff1=w_ff1[0].astype(BF16), w_ff3=w_ff3[0].astype(BF16), w_ff2=w_ff2[0].astype(BF16),
              w_out_odd=w_out_odd[0].astype(BF16),
              w_exp1=w_exp1[0].astype(BF16), w_exp3=w_exp3[0].astype(BF16),
              w_exp2=w_exp2[0].astype(BF16),
              w_ple_in=w_ple_in.astype(BF16), w_ple_gate=w_ple_gate.astype(BF16))

    y_p, even_p, odd_p = _trunk(x_prompt, p_prompt, 0, None, None, w, wb)
    y_s, even_s, odd_s = _trunk(
        x_sample, p_sample, past_len,
        (cache_sb_k[0], cache_sb_v[0], cache_dsa_k[0], cache_dsa_v[0], cache_dsa_kidx[0]),
        (cache_fox_k[0], cache_fox_v[0], cache_fox_logf[0]), w, wb)
    tail = lambda a: a[:, -past_len:][None]
    whole = lambda a: a[None]
    return (y_p, y_s,
            *(tail(a) for a in even_p), *(tail(a) for a in odd_p),
            *(whole(a) for a in even_s), *(whole(a) for a in odd_s))
```

```python
import functools

import jax
import jax.numpy as jnp
from jax import lax
from jax.experimental import pallas as pl
from jax.experimental.pallas import tpu as pltpu

F32 = jnp.float32
BF16 = jnp.bfloat16
I32 = jnp.int32

EPS = 1e-6
HEAD_DIM = 64
CHUNK = 64
H_SB = 8
H_DSA = 8
KV_DSA = 2
IDX_HEADS = 8
IDX_DIM = 64
H_FOX = 16
DSA_TOPK = 256
ROT_DIM = HEAD_DIM // 4
ROPE_THETA = 500000.0
N_EXPERTS = 8
ATT_SCALE = HEAD_DIM ** -0.5

LANES = 128
MIB = 1024 * 1024
VMEM_LIMIT = 56 * MIB

ROW_TILE = 512
SB_Q_TILE, SB_K_TILE = 512, 256
FOX_Q_TILE, FOX_K_TILE = 512, 512
DSA_Q_TILE, DSA_K_TILE = 128, 512
SCORE_SUB = 256
KEY_PAD = 512
MOE_TILE = 1024
MOE_CHUNK = 256
MOE_F_TILE = 896

NEG_BIG = -1e30
M_INIT = -1e29
INT_MIN = -(2 ** 31)
INT_MAX = 2 ** 31 - 1
KEY_NEG_INF = -2139095041

E_QA, E_KA, E_VA, E_QI, E_QB, E_KB, E_VB, E_KI, E_WI, E_END = (
    0, 512, 1024, 1536, 2048, 2560, 2688, 2816, 2944, 3072)
O_Q, O_K, O_V, O_F, O_END = 0, 1024, 2048, 3072, 3200
DSA_HEAD_PERM = (0, 4, 1, 5, 2, 6, 3, 7)


def _params(sem, vmem=VMEM_LIMIT):
    return pltpu.CompilerParams(dimension_semantics=sem, vmem_limit_bytes=vmem)


def _dot(a, b):
    return jnp.dot(a, b, preferred_element_type=F32)


def _dot_nt(a, b):
    return lax.dot_general(a, b, (((1,), (1,)), ((), ())), preferred_element_type=F32)


def _rms(x, g):
    return x * lax.rsqrt(jnp.mean(x * x, axis=-1, keepdims=True) + EPS) * g


def _split3(x):
    hi = x.astype(BF16)
    r1 = x - hi.astype(F32)
    mid = r1.astype(BF16)
    lo = (r1 - mid.astype(F32)).astype(BF16)
    return hi, mid, lo


def _sigmoid(x):
    return 1.0 / (1.0 + jnp.exp(-x))


def _log_sigmoid_neg(z):
    return -(jnp.maximum(z, 0.0) + jnp.log(1.0 + jnp.exp(-jnp.abs(z))))


def _proj_body(x_ref, g_ref, w_ref, of_ref, ob_ref):
    y = _rms(x_ref[...], g_ref[...]).astype(BF16)
    r = _dot(y, w_ref[...])
    of_ref[...] = r
    ob_ref[...] = r.astype(BF16)


def _proj(x, g, w, tm):
    n, d = x.shape
    c = w.shape[1]
    return pl.pallas_call(
        _proj_body,
        grid=(n // tm,),
        in_specs=[pl.BlockSpec((tm, d), lambda i: (i, 0)),
                  pl.BlockSpec((1, d), lambda i: (0, 0)),
                  pl.BlockSpec((d, c), lambda i: (0, 0))],
        out_specs=[pl.BlockSpec((tm, c), lambda i: (i, 0)),
                   pl.BlockSpec((tm, c), lambda i: (i, 0))],
        out_shape=[jax.ShapeDtypeStruct((n, c), F32), jax.ShapeDtypeStruct((n, c), BF16)],
        compiler_params=_params(("parallel",)),
        name="proj",
    )(x, g.reshape(1, d), w)


def _headnorm_body(x_ref, g_ref, cos_ref, sin_ref, s_ref, of_ref, ob_ref, *, rotary):
    x = x_ref[...]
    hi, mid, lo = _split3(x * x)
    s = s_ref[...]
    ms = (_dot(hi, s) + _dot(mid, s) + _dot(lo, s)) * (1.0 / HEAD_DIM)
    y = x * lax.rsqrt(ms + EPS) * g_ref[0]
    if rotary:
        lane = lax.broadcasted_iota(I32, (1, LANES), 1) % HEAD_DIM
        partner = jnp.where(lane < ROT_DIM // 2,
                            pltpu.roll(y, LANES - ROT_DIM // 2, 1),
                            pltpu.roll(y, ROT_DIM // 2, 1))
        y = y * cos_ref[...] + partner * sin_ref[...]
    of_ref[...] = y
    ob_ref[...] = y.astype(BF16)


def _headnorm(x, col0, gains, cos, sin, tm):
    n = x.shape[0]
    nblk = gains.shape[0]
    cb0 = col0 // LANES
    rotary = cos is not None
    if not rotary:
        cos = jnp.zeros((8, LANES), F32)
        sin = cos
        tab_spec = pl.BlockSpec((8, LANES), lambda i, j: (0, 0))
    else:
        nt = cos.shape[0] // tm
        tab_spec = pl.BlockSpec((tm, LANES), lambda i, j: (i % nt, 0))
    r = lax.broadcasted_iota(I32, (LANES, LANES), 0) // HEAD_DIM
    c = lax.broadcasted_iota(I32, (LANES, LANES), 1) // HEAD_DIM
    seg = (r == c).astype(BF16)
    return pl.pallas_call(
        functools.partial(_headnorm_body, rotary=rotary),
        grid=(n // tm, nblk),
        in_specs=[pl.BlockSpec((tm, LANES), lambda i, j: (i, cb0 + j)),
                  pl.BlockSpec((1, 1, LANES), lambda i, j: (j, 0, 0)),
                  tab_spec, tab_spec,
                  pl.BlockSpec((LANES, LANES), lambda i, j: (0, 0))],
        out_specs=[pl.BlockSpec((tm, LANES), lambda i, j: (i, j)),
                   pl.BlockSpec((tm, LANES), lambda i, j: (i, j))],
        out_shape=[jax.ShapeDtypeStruct((n, nblk * LANES), F32),
                   jax.ShapeDtypeStruct((n, nblk * LANES), BF16)],
        compiler_params=_params(("parallel", "parallel")),
        name="headnorm",
    )(x, gains.reshape(nblk, 1, LANES), cos, sin, seg)


def _res_body(*refs, n_in):
    h_ref, o_ref = refs[0], refs[-1]
    acc = h_ref[...]
    for t in range(n_in):
        acc = acc + _dot(refs[1 + t][...], refs[1 + n_in + t][...])
    o_ref[...] = acc


def _residual_matmul(h, acts, ws, tm):
    n, d = h.shape
    n_in = len(acts)
    in_specs = [pl.BlockSpec((tm, d), lambda i: (i, 0))]
    in_specs += [pl.BlockSpec((tm, a.shape[1]), lambda i: (i, 0)) for a in acts]
    in_specs += [pl.BlockSpec(w.shape, lambda i: (0, 0)) for w in ws]
    return pl.pallas_call(
        functools.partial(_res_body, n_in=n_in),
        grid=(n // tm,),
        in_specs=in_specs,
        out_specs=pl.BlockSpec((tm, d), lambda i: (i, 0)),
        out_shape=jax.ShapeDtypeStruct((n, d), F32),
        compiler_params=_params(("parallel",)),
        name="residual_matmul",
    )(h, *acts, *ws)


def _ffn_body(h_ref, g_ref, w1_ref, w3_ref, w2_ref, o_ref):
    x = h_ref[...]
    xn = _rms(x, g_ref[...]).astype(BF16)
    a = _dot(xn, w1_ref[...])
    b = _dot(xn, w3_ref[...])
    hm = (a * _sigmoid(a) * b).astype(BF16)
    o_ref[...] = x + _dot(hm, w2_ref[...])


def _ffn(h, g, w1, w3, w2, tm):
    n, d = h.shape
    f = w1.shape[1]
    once = pl.Buffered(1)
    return pl.pallas_call(
        _ffn_body,
        grid=(n // tm,),
        in_specs=[pl.BlockSpec((tm, d), lambda i: (i, 0)),
                  pl.BlockSpec((1, d), lambda i: (0, 0)),
                  pl.BlockSpec((d, f), lambda i: (0, 0), pipeline_mode=once),
                  pl.BlockSpec((d, f), lambda i: (0, 0), pipeline_mode=once),
                  pl.BlockSpec((f, d), lambda i: (0, 0), pipeline_mode=once)],
        out_specs=pl.BlockSpec((tm, d), lambda i: (i, 0)),
        out_shape=jax.ShapeDtypeStruct((n, d), F32),
        compiler_params=_params(("parallel",)),
        name="ffn",
    )(h, g.reshape(1, d), w1, w3, w2)


def _ple_body(h_ref, g_ref, wg_ref, p_ref, wp_ref, o_ref):
    x = h_ref[...]
    xn = _rms(x, g_ref[...]).astype(BF16)
    gate = _sigmoid(_dot(xn, wg_ref[...]))
    o_ref[...] = x + gate * _dot(p_ref[...].astype(BF16), wp_ref[...])


def _ple(h, g, wg, p, wp, tm):
    n, d = h.shape
    e = p.shape[1]
    return pl.pallas_call(
        _ple_body,
        grid=(n // tm,),
        in_specs=[pl.BlockSpec((tm, d), lambda i: (i, 0)),
                  pl.BlockSpec((1, d), lambda i: (0, 0)),
                  pl.BlockSpec((d, d), lambda i: (0, 0)),
                  pl.BlockSpec((tm, e), lambda i: (i, 0)),
                  pl.BlockSpec((e, d), lambda i: (0, 0))],
        out_specs=pl.BlockSpec((tm, d), lambda i: (i, 0)),
        out_shape=jax.ShapeDtypeStruct((n, d), F32),
        compiler_params=_params(("parallel",)),
        name="ple",
    )(h, g.reshape(1, d), wg, p, wp)


def _sb_body(q_ref, k_ref, v_ref, o_ref, *, tq, tk, q_pos0, n_kb_total):
    row0 = q_pos0 + pl.program_id(2) * tq
    q = (q_ref[0].astype(F32) * ATT_SCALE).astype(BF16)
    lane = lax.broadcasted_iota(I32, (1, LANES), 1)
    lo_half = lane < HEAD_DIM
    zero = jnp.zeros_like(q)
    q_halves = (jnp.where(lo_half, q, zero), jnp.where(lo_half, zero, q))
    qpos = row0 + lax.broadcasted_iota(I32, (tq, 1), 0)
    nkb = jnp.minimum((row0 + tq - 1 + tk - 1) // tk, n_kb_total)
    n_full = jnp.minimum(row0 // tk, nkb)
    later = (lax.broadcasted_iota(I32, (tk, tk), 0) >
             lax.broadcasted_iota(I32, (tk, tk), 1)).astype(BF16)

    def block(half, kb, c, masked):
        ks = pl.multiple_of(kb * tk, tk)
        z = _dot_nt(q_halves[half], k_ref[0, pl.ds(ks, tk), :])
        lk = _log_sigmoid_neg(z)
        if masked:
            vis = (ks + lax.broadcasted_iota(I32, (1, tk), 1)) < qpos
            lk = jnp.where(vis, lk, 0.0)
        hi = lk.astype(BF16)
        lo = (lk - hi.astype(F32)).astype(BF16)
        between = _dot(hi, later) + _dot(lo, later)
        w = jnp.exp(z + lk + between + c)
        if masked:
            w = jnp.where(vis, w, 0.0)
        return _dot(w.astype(BF16), v_ref[0, pl.ds(ks, tk), :]), jnp.sum(lk, axis=1, keepdims=True)

    def run(kbs, carry, masked):
        accs, cs = carry
        new_accs, new_cs = [], []
        for half in range(2):
            acc, c = accs[half], cs[half]
            for kb in kbs:
                pv, rs = block(half, kb, c, masked)
                acc, c = acc + pv, c + rs
            new_accs.append(acc)
            new_cs.append(c)
        return tuple(new_accs), tuple(new_cs)

    acc0 = jnp.zeros((tq, LANES), F32)
    c0 = jnp.zeros((tq, 1), F32)
    carry = ((acc0, acc0), (c0, c0))
    carry = lax.fori_loop(0, nkb - n_full,
                          lambda j, cr: run([nkb - 1 - j], cr, True), carry)
    carry = lax.fori_loop(0, n_full // 2,
                          lambda j, cr: run([n_full - 1 - 2 * j, n_full - 2 - 2 * j], cr, False),
                          carry)
    carry = lax.fori_loop(0, n_full % 2, lambda j, cr: run([0], cr, False), carry)
    accs, _ = carry
    o_ref[0] = jnp.where(lo_half, accs[0], accs[1]).astype(BF16)


def _sb_attention(q, q_cb, k, k_cb, v, v_cb, n_pairs, q_pos0, tq, tk):
    b, t = q.shape[:2]
    l = k.shape[1]
    return pl.pallas_call(
        functools.partial(_sb_body, tq=tq, tk=tk, q_pos0=q_pos0, n_kb_total=l // tk),
        grid=(b, n_pairs, t // tq),
        in_specs=[pl.BlockSpec((1, tq, LANES), lambda bi, p, i: (bi, i, q_cb + p)),
                  pl.BlockSpec((1, l, LANES), lambda bi, p, i: (bi, 0, k_cb + p)),
                  pl.BlockSpec((1, l, LANES), lambda bi, p, i: (bi, 0, v_cb + p))],
        out_specs=pl.BlockSpec((1, tq, LANES), lambda bi, p, i: (bi, i, p)),
        out_shape=jax.ShapeDtypeStruct((b, t, n_pairs * LANES), BF16),
        compiler_params=_params(("parallel", "parallel", "parallel")),
        name="sb_attention",
    )(q, k, v)


def _softmax_step(s, v, m, l, acc):
    m_new = jnp.maximum(m, jnp.max(s, axis=1, keepdims=True))
    alpha = jnp.exp(m - m_new)
    p = jnp.exp(s - m_new)
    l_new = alpha * l + jnp.sum(p, axis=1, keepdims=True)
    acc_new = alpha * acc + _dot(p.astype(BF16), v)
    return m_new, l_new, acc_new


def _fox_body(q_ref, k_ref, v_ref, fq_ref, fk_ref, o_ref, *, tq, tk, q_pos0, n_kb_total):
    row0 = q_pos0 + pl.program_id(2) * tq
    q = (q_ref[0].astype(F32) * ATT_SCALE).astype(BF16)
    lane = lax.broadcasted_iota(I32, (1, LANES), 1)
    lo_half = lane < HEAD_DIM
    zero = jnp.zeros_like(q)
    q_halves = (jnp.where(lo_half, q, zero), jnp.where(lo_half, zero, q))
    qpos = row0 + lax.broadcasted_iota(I32, (tq, 1), 0)
    fq = fq_ref[0, 0]
    nkb = jnp.minimum((row0 + tq + tk - 1) // tk, n_kb_total)
    n_full = jnp.minimum((row0 + 1) // tk, nkb)

    def body(j, carry, masked):
        ks = pl.multiple_of(j * tk, tk)
        k = k_ref[0, pl.ds(ks, tk), :]
        v = v_ref[0, pl.ds(ks, tk), :]
        out = []
        for half in range(2):
            m, l, acc = carry[half]
            fk = fk_ref[0, 0, half:half + 1, pl.ds(ks, tk)]
            s = _dot_nt(q_halves[half], k) + (fq[:, half:half + 1] - fk)
            if masked:
                vis = (ks + lax.broadcasted_iota(I32, (1, tk), 1)) <= qpos
                s = jnp.where(vis, s, NEG_BIG)
            out.append(_softmax_step(s, v, m, l, acc))
        return tuple(out)

    init = (jnp.full((tq, 1), M_INIT, F32), jnp.zeros((tq, 1), F32), jnp.zeros((tq, LANES), F32))
    carry = lax.fori_loop(0, n_full, functools.partial(body, masked=False), (init, init))
    carry = lax.fori_loop(n_full, nkb, functools.partial(body, masked=True), carry)
    (_, l0, a0), (_, l1, a1) = carry
    o_ref[0] = jnp.where(lo_half, a0 / l0, a1 / l1).astype(BF16)


def _fox_attention(q, q_cb, k, k_cb, v, v_cb, fq, fk, q_pos0, tq, tk):
    b, t = q.shape[:2]
    l = k.shape[1]
    n_pairs = H_FOX // 2
    return pl.pallas_call(
        functools.partial(_fox_body, tq=tq, tk=tk, q_pos0=q_pos0, n_kb_total=l // tk),
        grid=(b, n_pairs, t // tq),
        in_specs=[pl.BlockSpec((1, tq, LANES), lambda bi, p, i: (bi, i, q_cb + p)),
                  pl.BlockSpec((1, l, LANES), lambda bi, p, i: (bi, 0, k_cb + p)),
                  pl.BlockSpec((1, l, LANES), lambda bi, p, i: (bi, 0, v_cb + p)),
                  pl.BlockSpec((1, 1, tq, 2), lambda bi, p, i: (bi, p, i, 0)),
                  pl.BlockSpec((1, 1, 2, l), lambda bi, p, i: (bi, p, 0, 0))],
        out_specs=pl.BlockSpec((1, tq, LANES), lambda bi, p, i: (bi, i, p)),
        out_shape=jax.ShapeDtypeStruct((b, t, n_pairs * LANES), BF16),
        compiler_params=_params(("parallel", "parallel", "parallel")),
        name="fox_attention",
    )(q, k, v, fq, fk)


def _forget_body(raw_ref, b_ref, logf_ref, cum_ref, carry_ref, *, tb, n_given):
    j = pl.program_id(1)

    @pl.when(j == 0)
    def _():
        carry_ref[...] = jnp.zeros_like(carry_ref)

    raw = raw_ref[0]
    z = raw + b_ref[...]
    computed = _log_sigmoid_neg(-z)
    row = j * tb + lax.broadcasted_iota(I32, (tb, 1), 0)
    logf = jnp.where(row < n_given, raw, computed)
    incl = (lax.broadcasted_iota(I32, (tb, tb), 1) <=
            lax.broadcasted_iota(I32, (tb, tb), 0)).astype(BF16)
    hi, mid, lo = _split3(logf)
    cum = _dot(incl, hi) + _dot(incl, mid) + _dot(incl, lo) + carry_ref[0:1, :]
    logf_ref[0] = logf
    cum_ref[0] = cum
    carry_ref[0:1, :] = cum[tb - 1:tb, :]


def _forget_cumsum(raw, bias, n_given, tb):
    raw = jnp.pad(raw, ((0, 0), (0, 0), (0, LANES - raw.shape[2])))
    bias = jnp.pad(bias, (0, LANES - bias.shape[0]))
    b, l, h = raw.shape
    return pl.pallas_call(
        functools.partial(_forget_body, tb=tb, n_given=n_given),
        grid=(b, l // tb),
        in_specs=[pl.BlockSpec((1, tb, h), lambda bi, j: (bi, j, 0)),
                  pl.BlockSpec((1, h), lambda bi, j: (0, 0))],
        out_specs=[pl.BlockSpec((1, tb, h), lambda bi, j: (bi, j, 0)),
                   pl.BlockSpec((1, tb, h), lambda bi, j: (bi, j, 0))],
        out_shape=[jax.ShapeDtypeStruct((b, l, h), F32), jax.ShapeDtypeStruct((b, l, h), F32)],
        scratch_shapes=[pltpu.VMEM((8, h), F32)],
        compiler_params=_params(("parallel", "arbitrary")),
        name="forget_cumsum",
    )(raw, bias.reshape(1, h))


def _dsa_body(q_ref, qi_ref, wi_ref, k_ref, v_ref, ki_ref, o_ref,
              key_s, bias_s, qi_s, q_s, *, tq, tk, q_pos0, l_valid, n_kb_total, n_sel):
    row0 = q_pos0 + pl.program_id(1) * tq
    lane = lax.broadcasted_iota(I32, (1, LANES), 1)
    lo_half = lane < HEAD_DIM
    qpos = row0 + lax.broadcasted_iota(I32, (tq, 1), 0)
    qchunk = qpos // CHUNK
    kend = jnp.minimum(((row0 + tq - 1) // CHUNK + 1) * CHUNK, l_valid)
    nkb = jnp.minimum(jnp.maximum((kend + tk - 1) // tk, (n_sel + tk - 1) // tk), n_kb_total)

    n_stack = IDX_HEADS // 2
    for p in range(n_stack):
        blk = qi_ref[0, :, p * LANES:(p + 1) * LANES]
        zero = jnp.zeros_like(blk)
        qi_s[0, p * tq:(p + 1) * tq, :] = jnp.where(lo_half, blk, zero)
        qi_s[1, p * tq:(p + 1) * tq, :] = jnp.where(lo_half, zero, blk)
    wsc = wi_ref[0] * (IDX_DIM ** -0.5 * IDX_HEADS ** -0.5)

    def score_body(j, _):
        for u in range(tk // SCORE_SUB):
            ks = pl.multiple_of(j * tk + u * SCORE_SUB, SCORE_SUB)
            ki = ki_ref[0, pl.ds(ks, SCORE_SUB), :]
            score = jnp.zeros((tq, SCORE_SUB), F32)
            for half in range(2):
                rel = jnp.maximum(_dot_nt(qi_s[half], ki), 0.0)
                for p in range(n_stack):
                    h = 2 * p + half
                    score = score + wsc[:, h:h + 1] * rel[p * tq:(p + 1) * tq]
            kpos = ks + lax.broadcasted_iota(I32, (1, SCORE_SUB), 1)
            vis = ((kpos // CHUNK) <= qchunk) & (kpos < l_valid)
            score = jnp.where(vis, score, -jnp.inf)
            bits = lax.bitcast_convert_type(score, I32)
            key = bits ^ ((bits >> 31) & INT_MAX)
            key_s[:, pl.ds(ks, SCORE_SUB)] = jnp.where(bits == INT_MIN, 0, key)
        return 0

    lax.fori_loop(0, nkb, score_body, 0)

    def count(pred):
        def cbody(j, acc):
            blk = key_s[:, pl.ds(pl.multiple_of(j * tk, tk), tk)]
            hit = jnp.where(pred(blk, j), 1.0, 0.0)
            for c in range(tk // LANES):
                acc = acc + hit[:, c * LANES:(c + 1) * LANES]
            return acc
        acc = lax.fori_loop(0, nkb, cbody, jnp.zeros((tq, LANES), F32))
        return jnp.sum(acc, axis=1, keepdims=True)

    n_sel_f = float(n_sel)
    zero_col = jnp.zeros((tq, 1), I32)
    thr = jnp.where(count(lambda blk, j: blk >= zero_col) >= n_sel_f, 0, INT_MIN)

    def bit_body(b, thr):
        cand = thr | (1 << (30 - b))
        return jnp.where(count(lambda blk, j: blk >= cand) >= n_sel_f, cand, thr)

    thr = lax.fori_loop(0, 31, bit_body, thr)
    c_gt = count(lambda blk, j: blk > thr)
    c_ge = count(lambda blk, j: blk >= thr)
    need = n_sel_f - c_gt
    tied = (c_ge > n_sel_f) & (thr > KEY_NEG_INF)
    any_tied = jnp.max(jnp.where(tied, 1.0, 0.0)) > 0.0

    def tie_index_bound():
        def kidx(j):
            return j * tk + lax.broadcasted_iota(I32, (1, tk), 1)

        n_bits = (n_kb_total * tk - 1).bit_length()

        def jbody(b, jb):
            cand = jb | (1 << (n_bits - 1 - b))
            cnt = count(lambda blk, j: (blk == thr) & (kidx(j) < cand))
            return jnp.where(cnt < need, cand, jb)

        jb = lax.fori_loop(0, n_bits, jbody, jnp.zeros((tq, 1), I32))
        return jnp.where(tied, jb, INT_MAX)

    jbound = lax.cond(any_tied, tie_index_bound, lambda: jnp.full((tq, 1), INT_MAX, I32))

    def bias_body(j, _):
        ks = pl.multiple_of(j * tk, tk)
        blk = key_s[:, pl.ds(ks, tk)]
        kpos = ks + lax.broadcasted_iota(I32, (1, tk), 1)
        sel = (blk > thr) | ((blk == thr) & (kpos <= jbound))
        sel = sel & (blk > KEY_NEG_INF)
        bias_s[:, pl.ds(ks, tk)] = jnp.where(sel, 0.0, NEG_BIG)
        return 0

    lax.fori_loop(0, nkb, bias_body, 0)

    n_grp = H_DSA // KV_DSA
    for p in range(n_grp):
        blk = (q_ref[0, :, p * LANES:(p + 1) * LANES].astype(F32) * ATT_SCALE).astype(BF16)
        zero = jnp.zeros_like(blk)
        q_s[0, p * tq:(p + 1) * tq, :] = jnp.where(lo_half, blk, zero)
        q_s[1, p * tq:(p + 1) * tq, :] = jnp.where(lo_half, zero, blk)

    def att_body(j, carry):
        ks = pl.multiple_of(j * tk, tk)
        k = k_ref[0, pl.ds(ks, tk), :]
        v = v_ref[0, pl.ds(ks, tk), :]
        bias = bias_s[:, pl.ds(ks, tk)]
        bias = jnp.concatenate([bias] * n_grp, axis=0)
        return tuple(_softmax_step(_dot_nt(q_s[half], k) + bias, v, *carry[half])
                     for half in range(2))

    rows = n_grp * tq
    init = (jnp.full((rows, 1), M_INIT, F32), jnp.zeros((rows, 1), F32),
            jnp.zeros((rows, LANES), F32))
    (_, l0, a0), (_, l1, a1) = lax.fori_loop(0, nkb, att_body, (init, init))
    o0, o1 = a0 / l0, a1 / l1
    for p in range(n_grp):
        o_ref[0, :, p * LANES:(p + 1) * LANES] = jnp.where(
            lo_half, o0[p * tq:(p + 1) * tq], o1[p * tq:(p + 1) * tq]).astype(BF16)


def _dsa_attention(q, qi, qi_cb, wi, wi_cb, k, k_cb, v, v_cb, ki, ki_cb,
                   q_pos0, l_valid, n_sel, tq, tk):
    b, t = q.shape[:2]
    l = k.shape[1]
    width = H_DSA * HEAD_DIM
    return pl.pallas_call(
        functools.partial(_dsa_body, tq=tq, tk=tk, q_pos0=q_pos0, l_valid=l_valid,
                          n_kb_total=l // tk, n_sel=n_sel),
        grid=(b, t // tq),
        in_specs=[pl.BlockSpec((1, tq, width), lambda bi, i: (bi, i, 0)),
                  pl.BlockSpec((1, tq, width), lambda bi, i: (bi, i, qi_cb)),
                  pl.BlockSpec((1, tq, LANES), lambda bi, i: (bi, i, wi_cb)),
                  pl.BlockSpec((1, l, LANES), lambda bi, i: (bi, 0, k_cb)),
                  pl.BlockSpec((1, l, LANES), lambda bi, i: (bi, 0, v_cb)),
                  pl.BlockSpec((1, l, LANES), lambda bi, i: (bi, 0, ki_cb))],
        out_specs=pl.BlockSpec((1, tq, width), lambda bi, i: (bi, i, 0)),
        out_shape=jax.ShapeDtypeStruct((b, t, width), BF16),
        scratch_shapes=[pltpu.VMEM((tq, l), I32), pltpu.VMEM((tq, l), F32),
                        pltpu.VMEM((2, IDX_HEADS // 2 * tq, LANES), BF16),
                        pltpu.VMEM((2, H_DSA // KV_DSA * tq, LANES), BF16)],
        compiler_params=_params(("parallel", "parallel")),
        name="dsa_attention",
    )(q, qi, wi, k, v, ki)


def _moe_body(h_ref, g_ref, wr_ref, w1_ref, w3_ref, w2_ref, o_ref,
              xn_s, xg_s, ye_s, rank_s, gate_s, rank_t_s, *, tm, ch, n_fc):
    e = pl.program_id(1)
    fc = pl.program_id(2)
    lane = lax.broadcasted_iota(I32, (1, LANES), 1)

    @pl.when((e == 0) & (fc == 0))
    def _route():
        x = h_ref[...]
        xn = _rms(x, g_ref[...])
        xn_s[...] = xn.astype(BF16)
        o_ref[...] = x
        x3 = _split3(xn)
        logits = jnp.zeros((tm, LANES), F32)
        for a, b in ((2, 0), (0, 2), (1, 1), (1, 0), (0, 1), (0, 0)):
            logits = logits + _dot(x3[a], wr_ref[b])
        lane_f = lane.astype(F32)
        logits = jnp.where(lane < N_EXPERTS, logits, -jnp.inf)
        m1 = jnp.max(logits, axis=1, keepdims=True)
        i1 = jnp.min(jnp.where(logits == m1, lane_f, float(LANES)), axis=1, keepdims=True)
        rest = jnp.where(lane_f == i1, -jnp.inf, logits)
        m2 = jnp.max(rest, axis=1, keepdims=True)
        i2 = jnp.min(jnp.where(rest == m2, lane_f, float(LANES)), axis=1, keepdims=True)
        e2 = jnp.exp(m2 - m1)
        g1 = 1.0 / (1.0 + e2)
        g2 = e2 / (1.0 + e2)
        sel1 = lane_f == i1
        sel2 = lane_f == i2
        gate_s[...] = jnp.where(sel1, g1, 0.0) + jnp.where(sel2, g2, 0.0)
        sel = jnp.where(sel1 | sel2, 1.0, 0.0)
        incl = (lax.broadcasted_iota(I32, (tm, tm), 1) <=
                lax.broadcasted_iota(I32, (tm, tm), 0)).astype(BF16)
        rank = _dot(incl, sel.astype(BF16)) * sel
        rank_s[...] = rank
        rank_t_s[...] = rank.T

    rank_row = rank_t_s[pl.ds(e, 1), :]
    cnt = jnp.max(rank_row).astype(I32)
    nch = (cnt + ch - 1) // ch

    @pl.when(fc == 0)
    def _gather():
        def gbody(c, _):
            base = pl.multiple_of(c * ch, ch)
            slot = (base + 1 + lax.broadcasted_iota(I32, (ch, 1), 0)).astype(F32)
            pick = jnp.where(rank_row == slot, 1.0, 0.0).astype(BF16)
            xg_s[pl.ds(base, ch), :] = _dot(pick, xn_s[...]).astype(BF16)
            return 0
        lax.fori_loop(0, nch, gbody, 0)

    def fbody(c, _):
        base = pl.multiple_of(c * ch, ch)
        xg = xg_s[pl.ds(base, ch), :]
        a = _dot(xg, w1_ref[0])
        b = _dot(xg, w3_ref[0])
        part = _dot((a * _sigmoid(a) * b).astype(BF16), w2_ref[0])

        @pl.when(fc == 0)
        def _():
            ye_s[pl.ds(base, ch), :] = part

        @pl.when(fc != 0)
        def _():
            ye_s[pl.ds(base, ch), :] += part
        return 0

    lax.fori_loop(0, nch, fbody, 0)

    @pl.when(fc == n_fc - 1)
    def _scatter():
        here = lane == e
        rank_col = jnp.sum(jnp.where(here, rank_s[...], 0.0), axis=1, keepdims=True)
        gate_col = jnp.sum(jnp.where(here, gate_s[...], 0.0), axis=1, keepdims=True)

        def sbody(c, _):
            base = pl.multiple_of(c * ch, ch)
            slot = (base + 1 + lax.broadcasted_iota(I32, (1, ch), 1)).astype(F32)
            place = jnp.where(rank_col == slot, 1.0, 0.0).astype(BF16)
            ye = ye_s[pl.ds(base, ch), :]
            hi = ye.astype(BF16)
            lo = (ye - hi.astype(F32)).astype(BF16)
            o_ref[...] += gate_col * (_dot(place, hi) + _dot(place, lo))
            return 0
        lax.fori_loop(0, nch, sbody, 0)


def _moe(h, g, wr3, w1, w3, w2, tm, ch, tf):
    n, d = h.shape
    ne, _, f = w1.shape
    n_fc = f // tf
    return pl.pallas_call(
        functools.partial(_moe_body, tm=tm, ch=ch, n_fc=n_fc),
        grid=(n // tm, ne, n_fc),
        in_specs=[pl.BlockSpec((tm, d), lambda i, e, c: (i, 0)),
                  pl.BlockSpec((1, d), lambda i, e, c: (0, 0)),
                  pl.BlockSpec((3, d, LANES), lambda i, e, c: (0, 0, 0)),
                  pl.BlockSpec((1, d, tf), lambda i, e, c: (e, 0, c)),
                  pl.BlockSpec((1, d, tf), lambda i, e, c: (e, 0, c)),
                  pl.BlockSpec((1, tf, d), lambda i, e, c: (e, c, 0))],
        out_specs=pl.BlockSpec((tm, d), lambda i, e, c: (i, 0)),
        out_shape=jax.ShapeDtypeStruct((n, d), F32),
        scratch_shapes=[pltpu.VMEM((tm, d), BF16), pltpu.VMEM((tm, d), BF16),
                        pltpu.VMEM((tm, d), F32), pltpu.VMEM((tm, LANES), F32),
                        pltpu.VMEM((tm, LANES), F32), pltpu.VMEM((LANES, tm), F32)],
        compiler_params=_params(("parallel", "arbitrary", "arbitrary")),
        name="moe",
    )(h, g.reshape(1, d), wr3, w1, w3, w2)


def _round_up(x, m):
    return (x + m - 1) // m * m


def _pad_rows(a, l_pad):
    return jnp.pad(a, ((0, 0), (0, l_pad - a.shape[1]), (0, 0)))


def _prep_weights(g_q_dsa, g_k_dsa, g_q_fox, g_k_fox, w_in_even, w_out_even, w_in_odd, w_router):
    d = w_in_even.shape[0]
    perm = jnp.asarray(DSA_HEAD_PERM)
    qa, ka, va, qb, kb, vb, qi, ki, wi = jnp.split(
        w_in_even, [512, 1024, 1536, 2048, 2176, 2304, 2816, 2880], axis=1)
    qb = qb.reshape(d, H_DSA, HEAD_DIM)[:, perm].reshape(d, H_DSA * HEAD_DIM)
    wi = jnp.pad(wi, ((0, 0), (0, LANES - IDX_HEADS)))
    w_even = jnp.concatenate([qa, ka, va, qi, qb, kb, vb, ki, ki, wi], axis=1).astype(BF16)
    w_out_sb = w_out_even[:H_SB * HEAD_DIM].astype(BF16)
    w_out_dsa = w_out_even[H_SB * HEAD_DIM:].reshape(H_DSA, HEAD_DIM, d)[perm]
    w_out_dsa = w_out_dsa.reshape(H_DSA * HEAD_DIM, d).astype(BF16)
    w_odd = jnp.pad(w_in_odd, ((0, 0), (0, O_END - w_in_odd.shape[1]))).astype(BF16)
    gains_even = jnp.concatenate([jnp.tile(g_q_dsa, (4, 2)), jnp.tile(g_k_dsa, (1, 2))], axis=0)
    gains_odd = jnp.concatenate([jnp.tile(g_q_fox, (8, 2)), jnp.tile(g_k_fox, (8, 2))], axis=0)
    wr = jnp.pad(w_router, ((0, 0), (0, LANES - N_EXPERTS)))
    wr_hi = wr.astype(BF16)
    wr_r1 = wr - wr_hi.astype(F32)
    wr_mid = wr_r1.astype(BF16)
    wr_lo = (wr_r1 - wr_mid.astype(F32)).astype(BF16)
    return dict(w_even=w_even, w_out_sb=w_out_sb, w_out_dsa=w_out_dsa, w_odd=w_odd,
                gains_even=gains_even, gains_odd=gains_odd,
                wr3=jnp.stack([wr_hi, wr_mid, wr_lo]))


def _rotary_tables(pos):
    half = ROT_DIM // 2
    inv = ROPE_THETA ** (-jnp.arange(half, dtype=F32) / half)
    ang = pos.astype(F32)[:, None] * inv[None, :]
    cos, sin = jnp.cos(ang), jnp.sin(ang)
    t = pos.shape[0]
    pad = HEAD_DIM - ROT_DIM
    cos_h = jnp.concatenate([cos, cos, jnp.ones((t, pad), F32)], axis=1)
    sin_h = jnp.concatenate([-sin, sin, jnp.zeros((t, pad), F32)], axis=1)
    return jnp.tile(cos_h, (1, 2)), jnp.tile(sin_h, (1, 2))


def _trunk(x, p, pos0, past_even, past_odd, w, wb):
    b, t, d = x.shape
    n = b * t
    tm = min(ROW_TILE, n)
    has_past = past_even is not None
    past_len = past_even[0].shape[1] if has_past else 0
    l_valid = past_len + t
    l_pad = _round_up(l_valid, KEY_PAD)
    n_sel = min(DSA_TOPK, l_valid // 4)
    h = x.reshape(n, d)

    def keys(cache, new):
        if not has_past:
            return new
        full = jnp.concatenate([cache.reshape(b, past_len, -1).astype(BF16), new], axis=1)
        return _pad_rows(full, l_pad)

    pf, pb = _proj(h, w["g_mix"][0], wb["w_even"], tm)
    cos, sin = _rotary_tables(pos0 + jnp.arange(t, dtype=I32))
    if t % tm:
        cos, sin = jnp.tile(cos, (tm // t, 1)), jnp.tile(sin, (tm // t, 1))
    qkf, qkb = _headnorm(pf, E_QB, wb["gains_even"], cos, sin, tm)
    pf3, pb3 = pf.reshape(b, t, E_END), pb.reshape(b, t, E_END)
    qkf3, qkb3 = qkf.reshape(b, t, -1), qkb.reshape(b, t, -1)
    kb_new_f = qkf3[:, :, 512:640]
    if has_past:
        c_sbk, c_sbv, c_dk, c_dv, c_ki = past_even
        ka, ka_cb = keys(c_sbk, pb3[:, :, E_KA:E_VA]), 0
        va, va_cb = keys(c_sbv, pb3[:, :, E_VA:E_QI]), 0
        kb, kb_cb = keys(c_dk, qkb3[:, :, 512:640]), 0
        vb, vb_cb = keys(c_dv, pb3[:, :, E_VB:E_KI]), 0
        ki2 = jnp.concatenate([c_ki, c_ki], axis=-1)
        ki, ki_cb = keys(ki2, pb3[:, :, E_KI:E_WI]), 0
    else:
        ka, ka_cb = pb3, E_KA // LANES
        va, va_cb = pb3, E_VA // LANES
        kb, kb_cb = qkb3, 512 // LANES
        vb, vb_cb = pb3, E_VB // LANES
        ki, ki_cb = pb3, E_KI // LANES
    oa = _sb_attention(pb3, E_QA // LANES, ka, ka_cb, va, va_cb, H_SB // 2, pos0,
                       min(SB_Q_TILE, t), SB_K_TILE)
    ob = _dsa_attention(qkb3, pb3, E_QI // 512, pf3, E_WI // LANES, kb, kb_cb, vb, vb_cb,
                        ki, ki_cb, pos0, l_valid, n_sel, min(DSA_Q_TILE, t), DSA_K_TILE)
    h = _residual_matmul(h, [oa.reshape(n, -1), ob.reshape(n, -1)],
                         [wb["w_out_sb"], wb["w_out_dsa"]], tm)
    h = _ffn(h, w["g_ffn"][0], wb["w_ff1"], wb["w_ff3"], wb["w_ff2"], tm)
    h = _ple(h, w["g_ple"][0], wb["w_ple_gate"][0], p[0].reshape(n, -1), wb["w_ple_in"][0], tm)
    even_state = (pf3[:, :, E_KA:E_VA].reshape(b, t, H_SB, HEAD_DIM),
                  pf3[:, :, E_VA:E_QI].reshape(b, t, H_SB, HEAD_DIM),
                  kb_new_f.reshape(b, t, KV_DSA, HEAD_DIM),
                  pf3[:, :, E_VB:E_KI].reshape(b, t, KV_DSA, HEAD_DIM),
                  pf3[:, :, E_KI:E_KI + IDX_DIM])

    pf, pb = _proj(h, w["g_mix"][1], wb["w_odd"], tm)
    qkf, qkb = _headnorm(pf, O_Q, wb["gains_odd"], None, None, tm)
    pf3, pb3 = pf.reshape(b, t, O_END), pb.reshape(b, t, O_END)
    qkf3, qkb3 = qkf.reshape(b, t, -1), qkb.reshape(b, t, -1)
    gate_pre = pf3[:, :, O_F:O_F + H_FOX]
    if has_past:
        c_fk, c_fv, c_lf = past_odd
        kf, kf_cb = keys(c_fk, qkb3[:, :, 1024:2048]), 0
        vf, vf_cb = keys(c_fv, pb3[:, :, O_V:O_F]), 0
        raw = _pad_rows(jnp.concatenate([c_lf, gate_pre], axis=1), l_pad)
    else:
        kf, kf_cb = qkb3, 1024 // LANES
        vf, vf_cb = pb3, O_V // LANES
        raw = gate_pre
    logf, cum = _forget_cumsum(raw, w["b_forget"][0], past_len, min(KEY_PAD, l_pad))
    logf, cum = logf[:, :, :H_FOX], cum[:, :, :H_FOX]
    fq = cum[:, past_len:l_valid].reshape(b, t, H_FOX // 2, 2).transpose(0, 2, 1, 3)
    fk = cum.reshape(b, l_pad, H_FOX // 2, 2).transpose(0, 2, 3, 1)
    of = _fox_attention(qkb3, 0, kf, kf_cb, vf, vf_cb, fq, fk, pos0,
                        min(FOX_Q_TILE, t), FOX_K_TILE)
    h = _residual_matmul(h, [of.reshape(n, -1)], [wb["w_out_odd"]], tm)
    h = _moe(h, w["g_ffn"][1], wb["wr3"], wb["w_exp1"], wb["w_exp3"], wb["w_exp2"],
             min(MOE_TILE, n), min(MOE_CHUNK, n), MOE_F_TILE)
    h = _ple(h, w["g_ple"][1], wb["w_ple_gate"][1], p[1].reshape(n, -1), wb["w_ple_in"][1], tm)
    odd_state = (qkf3[:, :, 1024:2048].reshape(b, t, H_FOX, HEAD_DIM),
                 pf3[:, :, O_V:O_F].reshape(b, t, H_FOX, HEAD_DIM),
                 logf[:, past_len:l_valid])
    return h.reshape(b, t, d), even_state, odd_state


def kernel(x_prompt, x_sample, p_prompt, p_sample, cache_sb_k, cache_sb_v, cache_dsa_k, cache_dsa_v, cache_dsa_kidx, cache_fox_k, cache_fox_v, cache_fox_logf, g_mix, g_ffn, g_ple, w_in_even, g_q_dsa, g_k_dsa, w_out_even, w_ff1, w_ff3, w_ff2, w_in_odd, b_forget, g_q_fox, g_k_fox, w_out_odd, w_router, w_exp1, w_exp3, w_exp2, w_ple_in, w_ple_gate):
    assert g_mix.shape[0] == 2, "two layers: one even (stick-breaking + DSA), one odd (FoX + experts)"
    past_len = cache_sb_k.shape[2]
    w = dict(g_mix=g_mix, g_ffn=g_ffn, g_ple=g_ple, b_forget=b_forget)
    wb = _prep_weights(g_q_dsa[0], g_k_dsa[0], g_q_fox[0], g_k_fox[0],
                       w_in_even[0], w_out_even[0], w_in_odd[0], w_router[0])
    wb.update(w_ff1=w_ff1[0].astype(BF16), w_ff3=w_ff3[0].astype(BF16), w_ff2=w_ff2[0].astype(BF16),
              w_out_odd=w_out_odd[0].astype(BF16),
              w_exp1=w_exp1[0].astype(BF16), w_exp3=w_exp3[0].astype(BF16),
              w_exp2=w_exp2[0].astype(BF16),
              w_ple_in=w_ple_in.astype(BF16), w_ple_gate=w_ple_gate.astype(BF16))

    y_p, even_p, odd_p = _trunk(x_prompt, p_prompt, 0, None, None, w, wb)
    y_s, even_s, odd_s = _trunk(
        x_sample, p_sample, past_len,
        (cache_sb_k[0], cache_sb_v[0], cache_dsa_k[0], cache_dsa_v[0], cache_dsa_kidx[0]),
        (cache_fox_k[0], cache_fox_v[0], cache_fox_logf[0]), w, wb)
    tail = lambda a: a[:, -past_len:][None]
    whole = lambda a: a[None]
    return (y_p, y_s,
            *(tail(a) for a in even_p), *(tail(a) for a in odd_p),
            *(whole(a) for a in even_s), *(whole(a) for a in odd_s))
```

```python
import functools

import jax
import jax.numpy as jnp
from jax import lax
from jax.experimental import pallas as pl
from jax.experimental.pallas import tpu as pltpu

F32 = jnp.float32
BF16 = jnp.bfloat16
I32 = jnp.int32

EPS = 1e-6
HEAD_DIM = 64
CHUNK = 64
H_SB = 8
H_DSA = 8
KV_DSA = 2
IDX_HEADS = 8
IDX_DIM = 64
H_FOX = 16
DSA_TOPK = 256
ROT_DIM = HEAD_DIM // 4
ROPE_THETA = 500000.0
N_EXPERTS = 8
ATT_SCALE = HEAD_DIM ** -0.5
LOG2E = 1.4426950408889634
Q_SCALE = ATT_SCALE * LOG2E

LANES = 128
MIB = 1024 * 1024
VMEM_LIMIT = 56 * MIB

ROW_TILE = 512
SB_Q_TILE, SB_K_TILE = 512, 256
FOX_Q_TILE, FOX_K_TILE = 512, 512
DSA_Q_TILE, DSA_K_TILE = 128, 512
SCORE_SUB = 256
KEY_PAD = 512
MOE_TILE = 1024
MOE_CHUNK = 304
MOE_F_TILE = 896

NEG_BIG = -1e30
M_INIT = -1e29
INT_MIN = -(2 ** 31)
INT_MAX = 2 ** 31 - 1
KEY_NEG_INF = -2139095041

E_QB, E_KB, E_VB, E_KI, E_WI, E_QI, E_QA, E_KA, E_VA, E_END = (
    0, 512, 640, 768, 896, 1024, 1536, 2048, 2560, 3072)
E_NORM = E_VB - E_QB
O_Q, O_K, O_V, O_F, O_END = 0, 1024, 2048, 3072, 3200
DSA_HEAD_PERM = (0, 4, 1, 5, 2, 6, 3, 7)


def _params(sem, vmem=VMEM_LIMIT):
    return pltpu.CompilerParams(dimension_semantics=sem, vmem_limit_bytes=vmem)


def _dot(a, b):
    return jnp.dot(a, b, preferred_element_type=F32)


def _dot_nt(a, b):
    return lax.dot_general(a, b, (((1,), (1,)), ((), ())), preferred_element_type=F32)


def _rms(x, g):
    return x * lax.rsqrt(jnp.mean(x * x, axis=-1, keepdims=True) + EPS) * g


def _split3(x):
    hi = x.astype(BF16)
    r1 = x - hi.astype(F32)
    mid = r1.astype(BF16)
    lo = (r1 - mid.astype(F32)).astype(BF16)
    return hi, mid, lo


def _sigmoid(x):
    return 1.0 / (1.0 + jnp.exp(-x))


def _log_sigmoid_neg(z):
    return -(jnp.maximum(z, 0.0) + jnp.log(1.0 + jnp.exp(-jnp.abs(z))))


def _proj_body(x_ref, g_ref, w_ref, sc_ref, of_ref, ob_ref):
    y = _rms(x_ref[...], g_ref[...]).astype(BF16)
    r = _dot(y, w_ref[...])
    of_ref[...] = r
    ob_ref[...] = (r * sc_ref[...]).astype(BF16)


def _proj(x, g, w, scales, tm):
    n, d = x.shape
    c = w.shape[1]
    return pl.pallas_call(
        _proj_body,
        grid=(n // tm,),
        in_specs=[pl.BlockSpec((tm, d), lambda i: (i, 0)),
                  pl.BlockSpec((1, d), lambda i: (0, 0)),
                  pl.BlockSpec((d, c), lambda i: (0, 0)),
                  pl.BlockSpec((1, c), lambda i: (0, 0))],
        out_specs=[pl.BlockSpec((tm, c), lambda i: (i, 0)),
                   pl.BlockSpec((tm, c), lambda i: (i, 0))],
        out_shape=[jax.ShapeDtypeStruct((n, c), F32), jax.ShapeDtypeStruct((n, c), BF16)],
        compiler_params=_params(("parallel",)),
        name="proj",
    )(x, g.reshape(1, d), w, scales.reshape(1, c))


def _headnorm_body(x_ref, g_ref, sc_ref, cos_ref, sin_ref, s_ref, of_ref, ob_ref, *, rotary, nblk):
    s = s_ref[...]
    for j in range(nblk):
        cols = slice(j * LANES, (j + 1) * LANES)
        x = x_ref[:, cols]
        hi, mid, lo = _split3(x * x)
        ms = (_dot(hi, s) + _dot(mid, s) + _dot(lo, s)) * (1.0 / HEAD_DIM)
        y = x * lax.rsqrt(ms + EPS) * g_ref[:, cols]
        if rotary:
            lane = lax.broadcasted_iota(I32, (1, LANES), 1) % HEAD_DIM
            partner = jnp.where(lane < ROT_DIM // 2,
                                pltpu.roll(y, LANES - ROT_DIM // 2, 1),
                                pltpu.roll(y, ROT_DIM // 2, 1))
            y = y * cos_ref[...] + partner * sin_ref[...]
        of_ref[:, cols] = y
        ob_ref[:, cols] = (y * sc_ref[:, cols]).astype(BF16)


def _headnorm(x, col0, gains, scales, cos, sin, tm):
    n = x.shape[0]
    width = gains.shape[0]
    nblk = width // LANES
    assert col0 % width == 0
    rotary = cos is not None
    if not rotary:
        cos = jnp.zeros((8, LANES), F32)
        sin = cos
        tab_spec = pl.BlockSpec((8, LANES), lambda i: (0, 0))
    else:
        nt = cos.shape[0] // tm
        tab_spec = pl.BlockSpec((tm, LANES), lambda i: (i % nt, 0))
    r = lax.broadcasted_iota(I32, (LANES, LANES), 0) // HEAD_DIM
    c = lax.broadcasted_iota(I32, (LANES, LANES), 1) // HEAD_DIM
    seg = (r == c).astype(BF16)
    return pl.pallas_call(
        functools.partial(_headnorm_body, rotary=rotary, nblk=nblk),
        grid=(n // tm,),
        in_specs=[pl.BlockSpec((tm, width), lambda i: (i, col0 // width)),
                  pl.BlockSpec((1, width), lambda i: (0, 0)),
                  pl.BlockSpec((1, width), lambda i: (0, 0)),
                  tab_spec, tab_spec,
                  pl.BlockSpec((LANES, LANES), lambda i: (0, 0))],
        out_specs=[pl.BlockSpec((tm, width), lambda i: (i, 0)),
                   pl.BlockSpec((tm, width), lambda i: (i, 0))],
        out_shape=[jax.ShapeDtypeStruct((n, width), F32),
                   jax.ShapeDtypeStruct((n, width), BF16)],
        compiler_params=_params(("parallel",)),
        name="headnorm",
    )(x, gains.reshape(1, width), scales.reshape(1, width), cos, sin, seg)


def _res_body(*refs, n_in):
    h_ref, o_ref = refs[0], refs[-1]
    acc = h_ref[...]
    for t in range(n_in):
        acc = acc + _dot(refs[1 + t][...], refs[1 + n_in + t][...])
    o_ref[...] = acc


def _residual_matmul(h, acts, ws, tm):
    n, d = h.shape
    n_in = len(acts)
    in_specs = [pl.BlockSpec((tm, d), lambda i: (i, 0))]
    in_specs += [pl.BlockSpec((tm, a.shape[1]), lambda i: (i, 0)) for a in acts]
    in_specs += [pl.BlockSpec(w.shape, lambda i: (0, 0)) for w in ws]
    return pl.pallas_call(
        functools.partial(_res_body, n_in=n_in),
        grid=(n // tm,),
        in_specs=in_specs,
        out_specs=pl.BlockSpec((tm, d), lambda i: (i, 0)),
        out_shape=jax.ShapeDtypeStruct((n, d), F32),
        compiler_params=_params(("parallel",)),
        name="residual_matmul",
    )(h, *acts, *ws)


def _ffn_body(h_ref, g_ref, w1_ref, w3_ref, w2_ref, o_ref):
    x = h_ref[...]
    xn = _rms(x, g_ref[...]).astype(BF16)
    a = _dot(xn, w1_ref[...])
    b = _dot(xn, w3_ref[...])
    hm = (a * _sigmoid(a) * b).astype(BF16)
    o_ref[...] = x + _dot(hm, w2_ref[...])


def _ffn(h, g, w1, w3, w2, tm):
    n, d = h.shape
    f = w1.shape[1]
    once = pl.Buffered(1)
    return pl.pallas_call(
        _ffn_body,
        grid=(n // tm,),
        in_specs=[pl.BlockSpec((tm, d), lambda i: (i, 0)),
                  pl.BlockSpec((1, d), lambda i: (0, 0)),
                  pl.BlockSpec((d, f), lambda i: (0, 0), pipeline_mode=once),
                  pl.BlockSpec((d, f), lambda i: (0, 0), pipeline_mode=once),
                  pl.BlockSpec((f, d), lambda i: (0, 0), pipeline_mode=once)],
        out_specs=pl.BlockSpec((tm, d), lambda i: (i, 0)),
        out_shape=jax.ShapeDtypeStruct((n, d), F32),
        compiler_params=_params(("parallel",)),
        name="ffn",
    )(h, g.reshape(1, d), w1, w3, w2)


def _ple_body(h_ref, g_ref, wg_ref, p_ref, wp_ref, o_ref):
    x = h_ref[...]
    xn = _rms(x, g_ref[...]).astype(BF16)
    gate = _sigmoid(_dot(xn, wg_ref[...]))
    o_ref[...] = x + gate * _dot(p_ref[...].astype(BF16), wp_ref[...])


def _ple(h, g, wg, p, wp, tm):
    n, d = h.shape
    e = p.shape[1]
    return pl.pallas_call(
        _ple_body,
        grid=(n // tm,),
        in_specs=[pl.BlockSpec((tm, d), lambda i: (i, 0)),
                  pl.BlockSpec((1, d), lambda i: (0, 0)),
                  pl.BlockSpec((d, d), lambda i: (0, 0)),
                  pl.BlockSpec((tm, e), lambda i: (i, 0)),
                  pl.BlockSpec((e, d), lambda i: (0, 0))],
        out_specs=pl.BlockSpec((tm, d), lambda i: (i, 0)),
        out_shape=jax.ShapeDtypeStruct((n, d), F32),
        compiler_params=_params(("parallel",)),
        name="ple",
    )(h, g.reshape(1, d), wg, p, wp)


def _sb_body(q_ref, k_ref, v_ref, o_ref, *, tq, tk, q_pos0, n_kb_total):
    row0 = q_pos0 + pl.program_id(2) * tq
    q = q_ref[0]
    lane = lax.broadcasted_iota(I32, (1, LANES), 1)
    lo_half = lane < HEAD_DIM
    zero = jnp.zeros_like(q)
    q_halves = (jnp.where(lo_half, q, zero), jnp.where(lo_half, zero, q))
    qpos = row0 + lax.broadcasted_iota(I32, (tq, 1), 0)
    nkb = jnp.minimum((row0 + tq - 1 + tk - 1) // tk, n_kb_total)
    n_full = jnp.minimum(row0 // tk, nkb)
    later = (lax.broadcasted_iota(I32, (tk, tk), 0) >
             lax.broadcasted_iota(I32, (tk, tk), 1)).astype(BF16)

    def block(half, kb, c, masked):
        ks = pl.multiple_of(kb * tk, tk)
        z = _dot_nt(q_halves[half], k_ref[0, pl.ds(ks, tk), :])
        sp = jnp.maximum(z, 0.0) + jnp.log2(1.0 + jnp.exp2(-jnp.abs(z)))
        if masked:
            vis = (ks + lax.broadcasted_iota(I32, (1, tk), 1)) < qpos
            sp = jnp.where(vis, sp, 0.0)
        between = _dot(sp.astype(BF16), later)
        w = jnp.exp2(z - sp - between - c)
        if masked:
            w = jnp.where(vis, w, 0.0)
        row_sum = between[:, 0:1] + sp[:, 0:1]
        return _dot(w.astype(BF16), v_ref[0, pl.ds(ks, tk), :]), row_sum

    def run(kbs, carry, masked):
        accs, cs = carry
        new_accs, new_cs = [], []
        for half in range(2):
            acc, c = accs[half], cs[half]
            for kb in kbs:
                pv, rs = block(half, kb, c, masked)
                acc, c = acc + pv, c + rs
            new_accs.append(acc)
            new_cs.append(c)
        return tuple(new_accs), tuple(new_cs)

    acc0 = jnp.zeros((tq, LANES), F32)
    c0 = jnp.zeros((tq, 1), F32)
    carry = ((acc0, acc0), (c0, c0))
    carry = lax.fori_loop(0, nkb - n_full,
                          lambda j, cr: run([nkb - 1 - j], cr, True), carry)
    carry = lax.fori_loop(0, n_full // 2,
                          lambda j, cr: run([n_full - 1 - 2 * j, n_full - 2 - 2 * j], cr, False),
                          carry)
    carry = lax.fori_loop(0, n_full % 2, lambda j, cr: run([0], cr, False), carry)
    accs, _ = carry
    o_ref[0] = jnp.where(lo_half, accs[0], accs[1]).astype(BF16)


def _sb_attention(q, q_cb, k, k_cb, v, v_cb, n_pairs, q_pos0, tq, tk):
    b, t = q.shape[:2]
    l = k.shape[1]
    return pl.pallas_call(
        functools.partial(_sb_body, tq=tq, tk=tk, q_pos0=q_pos0, n_kb_total=l // tk),
        grid=(b, n_pairs, t // tq),
        in_specs=[pl.BlockSpec((1, tq, LANES), lambda bi, p, i: (bi, i, q_cb + p)),
                  pl.BlockSpec((1, l, LANES), lambda bi, p, i: (bi, 0, k_cb + p)),
                  pl.BlockSpec((1, l, LANES), lambda bi, p, i: (bi, 0, v_cb + p))],
        out_specs=pl.BlockSpec((1, tq, LANES), lambda bi, p, i: (bi, i, p)),
        out_shape=jax.ShapeDtypeStruct((b, t, n_pairs * LANES), BF16),
        compiler_params=_params(("parallel", "parallel", "parallel")),
        name="sb_attention",
    )(q, k, v)


def _softmax_step(s, v, m, l, acc):
    m_new = jnp.maximum(m, jnp.max(s, axis=1, keepdims=True))
    alpha = jnp.exp2(m - m_new)
    p = jnp.exp2(s - m_new)
    l_new = alpha * l + jnp.sum(p, axis=1, keepdims=True)
    acc_new = alpha * acc + _dot(p.astype(BF16), v)
    return m_new, l_new, acc_new


def _fox_body(q_ref, k_ref, v_ref, fk_ref, o_ref, q_s, *, tq, tk, q_pos0, n_kb_total):
    row0 = q_pos0 + pl.program_id(2) * tq
    q = q_ref[0]
    lane = lax.broadcasted_iota(I32, (1, LANES), 1)
    lo_half = lane < HEAD_DIM
    zero = jnp.zeros_like(q)
    q_s[0] = jnp.where(lo_half, q, zero)
    q_s[1] = jnp.where(lo_half, zero, q)
    qpos = row0 + lax.broadcasted_iota(I32, (tq, 1), 0)
    nkb = jnp.minimum((row0 + tq + tk - 1) // tk, n_kb_total)
    n_full = jnp.minimum((row0 + 1) // tk, nkb)

    def body(j, carry, masked):
        ks = pl.multiple_of(j * tk, tk)
        k = k_ref[0, pl.ds(ks, tk), :]
        v = v_ref[0, pl.ds(ks, tk), :]
        out = []
        for half in range(2):
            m, l, acc = carry[half]
            fk = fk_ref[0, 0, half:half + 1, pl.ds(ks, tk)] * LOG2E
            s = _dot_nt(q_s[half], k) - fk
            if masked:
                vis = (ks + lax.broadcasted_iota(I32, (1, tk), 1)) <= qpos
                s = jnp.where(vis, s, NEG_BIG)
            out.append(_softmax_step(s, v, m, l, acc))
        return tuple(out)

    init = (jnp.full((tq, 1), M_INIT, F32), jnp.zeros((tq, 1), F32), jnp.zeros((tq, LANES), F32))
    carry = lax.fori_loop(0, n_full, functools.partial(body, masked=False), (init, init))
    carry = lax.fori_loop(n_full, nkb, functools.partial(body, masked=True), carry)
    (_, l0, a0), (_, l1, a1) = carry
    o_ref[0] = jnp.where(lo_half, a0 / l0, a1 / l1).astype(BF16)


def _fox_attention(q, q_cb, k, k_cb, v, v_cb, fk, q_pos0, tq, tk):
    b, t = q.shape[:2]
    l = k.shape[1]
    n_pairs = H_FOX // 2
    return pl.pallas_call(
        functools.partial(_fox_body, tq=tq, tk=tk, q_pos0=q_pos0, n_kb_total=l // tk),
        grid=(b, n_pairs, t // tq),
        in_specs=[pl.BlockSpec((1, tq, LANES), lambda bi, p, i: (bi, i, q_cb + p)),
                  pl.BlockSpec((1, l, LANES), lambda bi, p, i: (bi, 0, k_cb + p)),
                  pl.BlockSpec((1, l, LANES), lambda bi, p, i: (bi, 0, v_cb + p)),
                  pl.BlockSpec((1, 1, 2, l), lambda bi, p, i: (bi, p, 0, 0))],
        out_specs=pl.BlockSpec((1, tq, LANES), lambda bi, p, i: (bi, i, p)),
        out_shape=jax.ShapeDtypeStruct((b, t, n_pairs * LANES), BF16),
        scratch_shapes=[pltpu.VMEM((2, tq, LANES), BF16)],
        compiler_params=_params(("parallel", "parallel", "parallel")),
        name="fox_attention",
    )(q, k, v, fk)


def _forget_body(raw_ref, b_ref, logf_ref, cum_ref, carry_ref, *, tb, n_given):
    j = pl.program_id(1)

    @pl.when(j == 0)
    def _():
        carry_ref[...] = jnp.zeros_like(carry_ref)

    raw = raw_ref[0]
    z = raw + b_ref[...]
    computed = _log_sigmoid_neg(-z)
    row = j * tb + lax.broadcasted_iota(I32, (tb, 1), 0)
    logf = jnp.where(row < n_given, raw, computed)
    incl = (lax.broadcasted_iota(I32, (tb, tb), 1) <=
            lax.broadcasted_iota(I32, (tb, tb), 0)).astype(BF16)
    hi, mid, lo = _split3(logf)
    cum = _dot(incl, hi) + _dot(incl, mid) + _dot(incl, lo) + carry_ref[0:1, :]
    logf_ref[0] = logf
    cum_ref[0] = cum
    carry_ref[0:1, :] = cum[tb - 1:tb, :]


def _forget_cumsum(raw, bias, n_given, tb):
    raw = jnp.pad(raw, ((0, 0), (0, 0), (0, LANES - raw.shape[2])))
    bias = jnp.pad(bias, (0, LANES - bias.shape[0]))
    b, l, h = raw.shape
    return pl.pallas_call(
        functools.partial(_forget_body, tb=tb, n_given=n_given),
        grid=(b, l // tb),
        in_specs=[pl.BlockSpec((1, tb, h), lambda bi, j: (bi, j, 0)),
                  pl.BlockSpec((1, h), lambda bi, j: (0, 0))],
        out_specs=[pl.BlockSpec((1, tb, h), lambda bi, j: (bi, j, 0)),
                   pl.BlockSpec((1, tb, h), lambda bi, j: (bi, j, 0))],
        out_shape=[jax.ShapeDtypeStruct((b, l, h), F32), jax.ShapeDtypeStruct((b, l, h), F32)],
        scratch_shapes=[pltpu.VMEM((8, h), F32)],
        compiler_params=_params(("parallel", "arbitrary")),
        name="forget_cumsum",
    )(raw, bias.reshape(1, h))


def _dsa_body(q_ref, qi_ref, wi_ref, k_ref, v_ref, ki_ref, o_ref,
              key_s, bias_s, qi_s, q_s, *, tq, tk, q_pos0, l_valid, n_kb_total, n_sel):
    row0 = q_pos0 + pl.program_id(1) * tq
    lane = lax.broadcasted_iota(I32, (1, LANES), 1)
    lo_half = lane < HEAD_DIM
    qpos = row0 + lax.broadcasted_iota(I32, (tq, 1), 0)
    qchunk = qpos // CHUNK
    kend = jnp.minimum(((row0 + tq - 1) // CHUNK + 1) * CHUNK, l_valid)
    nkb = jnp.minimum(jnp.maximum((kend + tk - 1) // tk, (n_sel + tk - 1) // tk), n_kb_total)

    n_stack = IDX_HEADS // 2
    for p in range(n_stack):
        blk = qi_ref[0, :, p * LANES:(p + 1) * LANES]
        zero = jnp.zeros_like(blk)
        qi_s[0, p * tq:(p + 1) * tq, :] = jnp.where(lo_half, blk, zero)
        qi_s[1, p * tq:(p + 1) * tq, :] = jnp.where(lo_half, zero, blk)
    wsc = wi_ref[0] * (IDX_DIM ** -0.5 * IDX_HEADS ** -0.5)

    def score_body(j, _):
        for u in range(tk // SCORE_SUB):
            ks = pl.multiple_of(j * tk + u * SCORE_SUB, SCORE_SUB)
            ki = ki_ref[0, pl.ds(ks, SCORE_SUB), :]
            score = jnp.zeros((tq, SCORE_SUB), F32)
            for half in range(2):
                rel = jnp.maximum(_dot_nt(qi_s[half], ki), 0.0)
                for p in range(n_stack):
                    h = 2 * p + half
                    score = score + wsc[:, h:h + 1] * rel[p * tq:(p + 1) * tq]
            kpos = ks + lax.broadcasted_iota(I32, (1, SCORE_SUB), 1)
            vis = ((kpos // CHUNK) <= qchunk) & (kpos < l_valid)
            score = jnp.where(vis, score, -jnp.inf)
            bits = lax.bitcast_convert_type(score, I32)
            key = bits ^ ((bits >> 31) & INT_MAX)
            key_s[:, pl.ds(ks, SCORE_SUB)] = jnp.where(bits == INT_MIN, 0, key)
        return 0

    lax.fori_loop(0, nkb, score_body, 0)

    def count(pred):
        def cbody(j, acc):
            blk = key_s[:, pl.ds(pl.multiple_of(j * tk, tk), tk)]
            hit = jnp.where(pred(blk, j), 1.0, 0.0)
            for c in range(tk // LANES):
                acc = acc + hit[:, c * LANES:(c + 1) * LANES]
            return acc
        acc = lax.fori_loop(0, nkb, cbody, jnp.zeros((tq, LANES), F32))
        return jnp.sum(acc, axis=1, keepdims=True)

    n_sel_f = float(n_sel)
    zero_col = jnp.zeros((tq, 1), I32)
    c_zero = count(lambda blk, j: blk >= zero_col)
    thr = jnp.where(c_zero >= n_sel_f, 0, INT_MIN)
    c_ge = jnp.where(c_zero >= n_sel_f, c_zero, (nkb * tk).astype(F32))

    def unsettled(c_ge):
        return jnp.max(jnp.where(c_ge == n_sel_f, 0.0, 1.0)) > 0.0

    def bit_cond(state):
        b, _, _, more = state
        return (b < 31) & more

    def bit_body(state):
        b, thr, c_ge, _ = state
        cand = thr | (1 << (30 - b))
        c = count(lambda blk, j: blk >= cand)
        keep = c >= n_sel_f
        c_ge = jnp.where(keep, c, c_ge)
        return b + 1, jnp.where(keep, cand, thr), c_ge, unsettled(c_ge)

    _, thr, c_ge, _ = lax.while_loop(bit_cond, bit_body, (0, thr, c_ge, unsettled(c_ge)))
    tied = (c_ge > n_sel_f) & (thr > KEY_NEG_INF)
    any_tied = jnp.max(jnp.where(tied, 1.0, 0.0)) > 0.0

    def tie_index_bound():
        need = n_sel_f - count(lambda blk, j: blk > thr)

        def kidx(j):
            return j * tk + lax.broadcasted_iota(I32, (1, tk), 1)

        n_bits = (n_kb_total * tk - 1).bit_length()

        def jbody(b, jb):
            cand = jb | (1 << (n_bits - 1 - b))
            cnt = count(lambda blk, j: (blk == thr) & (kidx(j) < cand))
            return jnp.where(cnt < need, cand, jb)

        jb = lax.fori_loop(0, n_bits, jbody, jnp.zeros((tq, 1), I32))
        return jnp.where(tied, jb, INT_MAX)

    jbound = lax.cond(any_tied, tie_index_bound, lambda: jnp.full((tq, 1), INT_MAX, I32))

    def bias_body(j, _):
        ks = pl.multiple_of(j * tk, tk)
        blk = key_s[:, pl.ds(ks, tk)]
        kpos = ks + lax.broadcasted_iota(I32, (1, tk), 1)
        sel = (blk > thr) | ((blk == thr) & (kpos <= jbound))
        sel = sel & (blk > KEY_NEG_INF)
        bias_s[:, pl.ds(ks, tk)] = jnp.where(sel, 0.0, NEG_BIG)
        return 0

    lax.fori_loop(0, nkb, bias_body, 0)

    n_grp = H_DSA // KV_DSA
    for p in range(n_grp):
        blk = q_ref[0, :, p * LANES:(p + 1) * LANES]
        zero = jnp.zeros_like(blk)
        q_s[0, p * tq:(p + 1) * tq, :] = jnp.where(lo_half, blk, zero)
        q_s[1, p * tq:(p + 1) * tq, :] = jnp.where(lo_half, zero, blk)

    def att_body(j, carry):
        ks = pl.multiple_of(j * tk, tk)
        k = k_ref[0, pl.ds(ks, tk), :]
        v = v_ref[0, pl.ds(ks, tk), :]
        bias = bias_s[:, pl.ds(ks, tk)]
        bias = jnp.concatenate([bias] * n_grp, axis=0)
        return tuple(_softmax_step(_dot_nt(q_s[half], k) + bias, v, *carry[half])
                     for half in range(2))

    rows = n_grp * tq
    init = (jnp.full((rows, 1), M_INIT, F32), jnp.zeros((rows, 1), F32),
            jnp.zeros((rows, LANES), F32))
    (_, l0, a0), (_, l1, a1) = lax.fori_loop(0, nkb, att_body, (init, init))
    o0, o1 = a0 / l0, a1 / l1
    for p in range(n_grp):
        o_ref[0, :, p * LANES:(p + 1) * LANES] = jnp.where(
            lo_half, o0[p * tq:(p + 1) * tq], o1[p * tq:(p + 1) * tq]).astype(BF16)


def _dsa_attention(q, qi, qi_cb, wi, wi_cb, k, k_cb, v, v_cb, ki, ki_cb,
                   q_pos0, l_valid, n_sel, tq, tk):
    b, t = q.shape[:2]
    l = k.shape[1]
    width = H_DSA * HEAD_DIM
    return pl.pallas_call(
        functools.partial(_dsa_body, tq=tq, tk=tk, q_pos0=q_pos0, l_valid=l_valid,
                          n_kb_total=l // tk, n_sel=n_sel),
        grid=(b, t // tq),
        in_specs=[pl.BlockSpec((1, tq, width), lambda bi, i: (bi, i, 0)),
                  pl.BlockSpec((1, tq, width), lambda bi, i: (bi, i, qi_cb)),
                  pl.BlockSpec((1, tq, LANES), lambda bi, i: (bi, i, wi_cb)),
                  pl.BlockSpec((1, l, LANES), lambda bi, i: (bi, 0, k_cb)),
                  pl.BlockSpec((1, l, LANES), lambda bi, i: (bi, 0, v_cb)),
                  pl.BlockSpec((1, l, LANES), lambda bi, i: (bi, 0, ki_cb))],
        out_specs=pl.BlockSpec((1, tq, width), lambda bi, i: (bi, i, 0)),
        out_shape=jax.ShapeDtypeStruct((b, t, width), BF16),
        scratch_shapes=[pltpu.VMEM((tq, l), I32), pltpu.VMEM((tq, l), F32),
                        pltpu.VMEM((2, IDX_HEADS // 2 * tq, LANES), BF16),
                        pltpu.VMEM((2, H_DSA // KV_DSA * tq, LANES), BF16)],
        compiler_params=_params(("parallel", "parallel")),
        name="dsa_attention",
    )(q, qi, wi, k, v, ki)


def _moe_body(h_ref, g_ref, wr_ref, w1_ref, w3_ref, w2_ref, o_ref,
              xn_s, xg_s, ye_s, rank_s, gate_s, rank_t_s, *, tm, ch, n_fc):
    e = pl.program_id(1)
    fc = pl.program_id(2)
    lane = lax.broadcasted_iota(I32, (1, LANES), 1)

    @pl.when((e == 0) & (fc == 0))
    def _route():
        x = h_ref[...]
        xn = _rms(x, g_ref[...])
        xn_s[...] = xn.astype(BF16)
        o_ref[...] = x
        x3 = _split3(xn)
        logits = jnp.zeros((tm, LANES), F32)
        for a, b in ((2, 0), (0, 2), (1, 1), (1, 0), (0, 1), (0, 0)):
            logits = logits + _dot(x3[a], wr_ref[b])
        lane_f = lane.astype(F32)
        logits = jnp.where(lane < N_EXPERTS, logits, -jnp.inf)
        m1 = jnp.max(logits, axis=1, keepdims=True)
        i1 = jnp.min(jnp.where(logits == m1, lane_f, float(LANES)), axis=1, keepdims=True)
        rest = jnp.where(lane_f == i1, -jnp.inf, logits)
        m2 = jnp.max(rest, axis=1, keepdims=True)
        i2 = jnp.min(jnp.where(rest == m2, lane_f, float(LANES)), axis=1, keepdims=True)
        e2 = jnp.exp(m2 - m1)
        g1 = 1.0 / (1.0 + e2)
        g2 = e2 / (1.0 + e2)
        sel1 = lane_f == i1
        sel2 = lane_f == i2
        gate_s[...] = jnp.where(sel1, g1, 0.0) + jnp.where(sel2, g2, 0.0)
        sel = jnp.where(sel1 | sel2, 1.0, 0.0)
        incl = (lax.broadcasted_iota(I32, (tm, tm), 1) <=
                lax.broadcasted_iota(I32, (tm, tm), 0)).astype(BF16)
        rank = _dot(incl, sel.astype(BF16)) * sel
        rank_s[...] = rank
        rank_t_s[...] = rank.T

    rank_row = rank_t_s[pl.ds(e, 1), :]
    cnt = jnp.max(rank_row).astype(I32)
    nch = (cnt + ch - 1) // ch

    @pl.when(fc == 0)
    def _gather():
        def gbody(c, _):
            base = pl.multiple_of(c * ch, ch)
            slot = (base + 1 + lax.broadcasted_iota(I32, (ch, 1), 0)).astype(F32)
            pick = jnp.where(rank_row == slot, 1.0, 0.0).astype(BF16)
            xg_s[pl.ds(base, ch), :] = _dot(pick, xn_s[...]).astype(BF16)
            return 0
        lax.fori_loop(0, nch, gbody, 0)

    def fbody(c, _):
        base = pl.multiple_of(c * ch, ch)
        xg = xg_s[pl.ds(base, ch), :]
        a = _dot(xg, w1_ref[0])
        b = _dot(xg, w3_ref[0])
        part = _dot((a * _sigmoid(a) * b).astype(BF16), w2_ref[0])

        @pl.when(fc == 0)
        def _():
            ye_s[pl.ds(base, ch), :] = part

        @pl.when(fc != 0)
        def _():
            ye_s[pl.ds(base, ch), :] += part
        return 0

    lax.fori_loop(0, nch, fbody, 0)

    @pl.when(fc == n_fc - 1)
    def _scatter():
        here = lane == e
        rank_col = jnp.sum(jnp.where(here, rank_s[...], 0.0), axis=1, keepdims=True)
        gate_col = jnp.sum(jnp.where(here, gate_s[...], 0.0), axis=1, keepdims=True)

        def sbody(c, _):
            base = pl.multiple_of(c * ch, ch)
            slot = (base + 1 + lax.broadcasted_iota(I32, (1, ch), 1)).astype(F32)
            place = jnp.where(rank_col == slot, 1.0, 0.0).astype(BF16)
            ye = ye_s[pl.ds(base, ch), :].astype(BF16)
            o_ref[...] += gate_col * _dot(place, ye)
            return 0
        lax.fori_loop(0, nch, sbody, 0)


def _moe(h, g, wr3, w1, w3, w2, tm, ch, tf):
    n, d = h.shape
    ne, _, f = w1.shape
    n_fc = f // tf
    return pl.pallas_call(
        functools.partial(_moe_body, tm=tm, ch=ch, n_fc=n_fc),
        grid=(n // tm, ne, n_fc),
        in_specs=[pl.BlockSpec((tm, d), lambda i, e, c: (i, 0)),
                  pl.BlockSpec((1, d), lambda i, e, c: (0, 0)),
                  pl.BlockSpec((3, d, LANES), lambda i, e, c: (0, 0, 0)),
                  pl.BlockSpec((1, d, tf), lambda i, e, c: (e, 0, c)),
                  pl.BlockSpec((1, d, tf), lambda i, e, c: (e, 0, c)),
                  pl.BlockSpec((1, tf, d), lambda i, e, c: (e, c, 0))],
        out_specs=pl.BlockSpec((tm, d), lambda i, e, c: (i, 0)),
        out_shape=jax.ShapeDtypeStruct((n, d), F32),
        scratch_shapes=[pltpu.VMEM((tm, d), BF16), pltpu.VMEM((_round_up(tm, ch), d), BF16),
                        pltpu.VMEM((_round_up(tm, ch), d), F32), pltpu.VMEM((tm, LANES), F32),
                        pltpu.VMEM((tm, LANES), F32), pltpu.VMEM((LANES, tm), F32)],
        compiler_params=_params(("parallel", "arbitrary", "arbitrary")),
        name="moe",
    )(h, g.reshape(1, d), wr3, w1, w3, w2)


def _round_up(x, m):
    return (x + m - 1) // m * m


def _pad_rows(a, l_pad):
    return jnp.pad(a, ((0, 0), (0, l_pad - a.shape[1]), (0, 0)))


def _prep_weights(g_q_dsa, g_k_dsa, g_q_fox, g_k_fox, w_in_even, w_out_even, w_in_odd, w_router):
    d = w_in_even.shape[0]
    perm = jnp.asarray(DSA_HEAD_PERM)
    qa, ka, va, qb, kb, vb, qi, ki, wi = jnp.split(
        w_in_even, [512, 1024, 1536, 2048, 2176, 2304, 2816, 2880], axis=1)
    qb = qb.reshape(d, H_DSA, HEAD_DIM)[:, perm].reshape(d, H_DSA * HEAD_DIM)
    wi = jnp.pad(wi, ((0, 0), (0, LANES - IDX_HEADS)))
    w_even = jnp.concatenate([qb, kb, vb, ki, ki, wi, qi, qa, ka, va], axis=1).astype(BF16)
    ones = lambda k: jnp.ones((k,), F32)
    q_scale = lambda k: jnp.full((k,), Q_SCALE, F32)
    scale_even = jnp.concatenate([ones(E_QA), q_scale(E_KA - E_QA), ones(E_END - E_KA)])
    scale_odd = ones(O_END)
    scale_norm_even = jnp.concatenate([q_scale(E_KB - E_QB), ones(E_VB - E_KB)])
    scale_norm_odd = jnp.concatenate([q_scale(O_K - O_Q), ones(O_V - O_K)])
    w_out_sb = w_out_even[:H_SB * HEAD_DIM].astype(BF16)
    w_out_dsa = w_out_even[H_SB * HEAD_DIM:].reshape(H_DSA, HEAD_DIM, d)[perm]
    w_out_dsa = w_out_dsa.reshape(H_DSA * HEAD_DIM, d).astype(BF16)
    w_odd = jnp.pad(w_in_odd, ((0, 0), (0, O_END - w_in_odd.shape[1]))).astype(BF16)
    gains_even = jnp.concatenate([jnp.tile(g_q_dsa, H_DSA), jnp.tile(g_k_dsa, KV_DSA)])
    gains_odd = jnp.concatenate([jnp.tile(g_q_fox, H_FOX), jnp.tile(g_k_fox, H_FOX)])
    wr = jnp.pad(w_router, ((0, 0), (0, LANES - N_EXPERTS)))
    wr_hi = wr.astype(BF16)
    wr_r1 = wr - wr_hi.astype(F32)
    wr_mid = wr_r1.astype(BF16)
    wr_lo = (wr_r1 - wr_mid.astype(F32)).astype(BF16)
    return dict(w_even=w_even, w_out_sb=w_out_sb, w_out_dsa=w_out_dsa, w_odd=w_odd,
                gains_even=gains_even, gains_odd=gains_odd,
                scale_even=scale_even, scale_odd=scale_odd,
                scale_norm_even=scale_norm_even, scale_norm_odd=scale_norm_odd,
                wr3=jnp.stack([wr_hi, wr_mid, wr_lo]))


def _rotary_tables(pos):
    half = ROT_DIM // 2
    inv = ROPE_THETA ** (-jnp.arange(half, dtype=F32) / half)
    ang = pos.astype(F32)[:, None] * inv[None, :]
    cos, sin = jnp.cos(ang), jnp.sin(ang)
    t = pos.shape[0]
    pad = HEAD_DIM - ROT_DIM
    cos_h = jnp.concatenate([cos, cos, jnp.ones((t, pad), F32)], axis=1)
    sin_h = jnp.concatenate([-sin, sin, jnp.zeros((t, pad), F32)], axis=1)
    return jnp.tile(cos_h, (1, 2)), jnp.tile(sin_h, (1, 2))


def _trunk(x, p, pos0, past_even, past_odd, w, wb):
    b, t, d = x.shape
    n = b * t
    tm = min(ROW_TILE, n)
    has_past = past_even is not None
    past_len = past_even[0].shape[1] if has_past else 0
    l_valid = past_len + t
    l_pad = _round_up(l_valid, KEY_PAD)
    n_sel = min(DSA_TOPK, l_valid // 4)
    h = x.reshape(n, d)

    def keys(cache, new):
        if not has_past:
            return new
        full = jnp.concatenate([cache.reshape(b, past_len, -1).astype(BF16), new], axis=1)
        return _pad_rows(full, l_pad)

    pf, pb = _proj(h, w["g_mix"][0], wb["w_even"], wb["scale_even"], tm)
    cos, sin = _rotary_tables(pos0 + jnp.arange(t, dtype=I32))
    if t % tm:
        cos, sin = jnp.tile(cos, (tm // t, 1)), jnp.tile(sin, (tm // t, 1))
    qkf, qkb = _headnorm(pf, E_QB, wb["gains_even"], wb["scale_norm_even"], cos, sin, tm)
    pf3, pb3 = pf.reshape(b, t, E_END), pb.reshape(b, t, E_END)
    qkf3, qkb3 = qkf.reshape(b, t, E_NORM), qkb.reshape(b, t, E_NORM)
    cols = lambda a, c0, width: a[:, :, c0:c0 + width]
    sb_w, kv_w = H_SB * HEAD_DIM, KV_DSA * HEAD_DIM
    if has_past:
        c_sbk, c_sbv, c_dk, c_dv, c_ki = past_even
        ka, ka_cb = keys(c_sbk, cols(pb3, E_KA, sb_w)), 0
        va, va_cb = keys(c_sbv, cols(pb3, E_VA, sb_w)), 0
        kb, kb_cb = keys(c_dk, cols(qkb3, E_KB, kv_w)), 0
        vb, vb_cb = keys(c_dv, cols(pb3, E_VB, kv_w)), 0
        ki2 = jnp.concatenate([c_ki, c_ki], axis=-1)
        ki, ki_cb = keys(ki2, cols(pb3, E_KI, LANES)), 0
    else:
        ka, ka_cb = pb3, E_KA // LANES
        va, va_cb = pb3, E_VA // LANES
        kb, kb_cb = qkb3, E_KB // LANES
        vb, vb_cb = pb3, E_VB // LANES
        ki, ki_cb = pb3, E_KI // LANES
    oa = _sb_attention(pb3, E_QA // LANES, ka, ka_cb, va, va_cb, H_SB // 2, pos0,
                       min(SB_Q_TILE, t), SB_K_TILE)
    ob = _dsa_attention(qkb3, pb3, E_QI // 512, pf3, E_WI // LANES, kb, kb_cb, vb, vb_cb,
                        ki, ki_cb, pos0, l_valid, n_sel, min(DSA_Q_TILE, t), DSA_K_TILE)
    h = _residual_matmul(h, [oa.reshape(n, -1), ob.reshape(n, -1)],
                         [wb["w_out_sb"], wb["w_out_dsa"]], tm)
    h = _ffn(h, w["g_ffn"][0], wb["w_ff1"], wb["w_ff3"], wb["w_ff2"], tm)
    h = _ple(h, w["g_ple"][0], wb["w_ple_gate"][0], p[0].reshape(n, -1), wb["w_ple_in"][0], tm)
    even_state = (cols(pf3, E_KA, sb_w).reshape(b, t, H_SB, HEAD_DIM),
                  cols(pf3, E_VA, sb_w).reshape(b, t, H_SB, HEAD_DIM),
                  cols(qkf3, E_KB, kv_w).reshape(b, t, KV_DSA, HEAD_DIM),
                  cols(pf3, E_VB, kv_w).reshape(b, t, KV_DSA, HEAD_DIM),
                  cols(pf3, E_KI, IDX_DIM))

    fox_w = H_FOX * HEAD_DIM
    pf, pb = _proj(h, w["g_mix"][1], wb["w_odd"], wb["scale_odd"], tm)
    qkf, qkb = _headnorm(pf, O_Q, wb["gains_odd"], wb["scale_norm_odd"], None, None, tm)
    pf3, pb3 = pf.reshape(b, t, O_END), pb.reshape(b, t, O_END)
    qkf3, qkb3 = qkf.reshape(b, t, O_V), qkb.reshape(b, t, O_V)
    gate_pre = cols(pf3, O_F, H_FOX)
    if has_past:
        c_fk, c_fv, c_lf = past_odd
        kf, kf_cb = keys(c_fk, cols(qkb3, O_K, fox_w)), 0
        vf, vf_cb = keys(c_fv, cols(pb3, O_V, fox_w)), 0
        raw = _pad_rows(jnp.concatenate([c_lf, gate_pre], axis=1), l_pad)
    else:
        kf, kf_cb = qkb3, O_K // LANES
        vf, vf_cb = pb3, O_V // LANES
        raw = gate_pre
    logf, cum = _forget_cumsum(raw, w["b_forget"][0], past_len, min(KEY_PAD, l_pad))
    logf, cum = logf[:, :, :H_FOX], cum[:, :, :H_FOX]
    fk = cum.reshape(b, l_pad, H_FOX // 2, 2).transpose(0, 2, 3, 1)
    of = _fox_attention(qkb3, O_Q // LANES, kf, kf_cb, vf, vf_cb, fk, pos0,
                        min(FOX_Q_TILE, t), FOX_K_TILE)
    h = _residual_matmul(h, [of.reshape(n, -1)], [wb["w_out_odd"]], tm)
    h = _moe(h, w["g_ffn"][1], wb["wr3"], wb["w_exp1"], wb["w_exp3"], wb["w_exp2"],
             min(MOE_TILE, n), min(MOE_CHUNK, n), MOE_F_TILE)
    h = _ple(h, w["g_ple"][1], wb["w_ple_gate"][1], p[1].reshape(n, -1), wb["w_ple_in"][1], tm)
    odd_state = (cols(qkf3, O_K, fox_w).reshape(b, t, H_FOX, HEAD_DIM),
                 cols(pf3, O_V, fox_w).reshape(b, t, H_FOX, HEAD_DIM),
                 logf[:, past_len:l_valid])
    return h.reshape(b, t, d), even_state, odd_state


def kernel(x_prompt, x_sample, p_prompt, p_sample, cache_sb_k, cache_sb_v, cache_dsa_k, cache_dsa_v, cache_dsa_kidx, cache_fox_k, cache_fox_v, cache_fox_logf, g_mix, g_ffn, g_ple, w_in_even, g_q_dsa, g_k_dsa, w_out_even, w_ff1, w_ff3, w_ff2, w_in_odd, b_forget, g_q_fox, g_k_fox, w_out_odd, w_router, w_exp1, w_exp3, w_exp2, w_ple_in, w_ple_gate):
    assert g_mix.shape[0] == 2, "two layers: one even (stick-breaking + DSA), one odd (FoX + experts)"
    past_len = cache_sb_k.shape[2]
    w = dict(g_mix=g_mix, g_ffn=g_ffn, g_ple=g_ple, b_forget=b_forget)
    wb = _prep_weights(g_q_dsa[0], g_k_dsa[0], g_q_fox[0], g_k_fox[0],
                       w_in_even[0], w_out_even[0], w_in_odd[0], w_router[0])
    wb.update(w_ff1=w_ff1[0].astype(BF16), w_ff3=w_ff3[0].astype(BF16), w_ff2=w_ff2[0].astype(BF16),
              w_out_odd=w_out_odd[0].astype(BF16),
              w_exp1=w_exp1[0].astype(BF16), w_exp3=w_exp3[0].astype(BF16),
              w_exp2=w_exp2[0].astype(BF16),
              w_ple_in=w_ple_in.astype(BF16), w_ple_gate=w_ple_gate.astype(BF16))

    y_p, even_p, odd_p = _trunk(x_prompt, p_prompt, 0, None, None, w, wb)
    y_s, even_s, odd_s = _trunk(
        x_sample, p_sample, past_len,
        (cache_sb_k[0], cache_sb_v[0], cache_dsa_k[0], cache_dsa_v[0], cache_dsa_kidx[0]),
        (cache_fox_k[0], cache_fox_v[0], cache_fox_logf[0]), w, wb)
    tail = lambda a: a[:, -past_len:][None]
    whole = lambda a: a[None]
    return (y_p, y_s,
            *(tail(a) for a in even_p), *(tail(a) for a in odd_p),
            *(whole(a) for a in even_s), *(whole(a) for a in odd_s))
```

```python
import functools

import jax
import jax.numpy as jnp
from jax import lax
from jax.experimental import pallas as pl
from jax.experimental.pallas import tpu as pltpu

F32 = jnp.float32
BF16 = jnp.bfloat16
I32 = jnp.int32

EPS = 1e-6
HEAD_DIM = 64
CHUNK = 64
H_SB = 8
H_DSA = 8
KV_DSA = 2
IDX_HEADS = 8
IDX_DIM = 64
H_FOX = 16
DSA_TOPK = 256
ROT_DIM = HEAD_DIM // 4
ROPE_THETA = 500000.0
N_EXPERTS = 8
ATT_SCALE = HEAD_DIM ** -0.5
LOG2E = 1.4426950408889634
Q_SCALE = ATT_SCALE * LOG2E

LANES = 128
MIB = 1024 * 1024
VMEM_LIMIT = 56 * MIB

ROW_TILE = 512
SB_Q_TILE, SB_K_TILE = 512, 256
SB_BLOCKS_PER_STEP = 2
EXP2_ZERO = 150.0
FOX_Q_TILE, FOX_K_TILE = 512, 512
FOX_PAIRS = 1
DSA_Q_TILE, DSA_K_TILE = 128, 512
SCORE_SUB = 256
KEY_PAD = 512
MOE_TILE = 1024
MOE_CHUNK = 304
MOE_F_TILE = 896

NEG_BIG = -1e30
M_INIT = -1e29
INT_MIN = -(2 ** 31)
INT_MAX = 2 ** 31 - 1
KEY_NEG_INF = -2139095041

E_QB, E_KB, E_VB, E_KI, E_WI, E_QI, E_QA, E_KA, E_VA, E_END = (
    0, 512, 640, 768, 896, 1024, 1536, 2048, 2560, 3072)
E_NORM = E_VB - E_QB
O_Q, O_K, O_V, O_F, O_END = 0, 1024, 2048, 3072, 3200
DSA_HEAD_PERM = (0, 4, 1, 5, 2, 6, 3, 7)


def _params(sem, vmem=VMEM_LIMIT):
    return pltpu.CompilerParams(dimension_semantics=sem, vmem_limit_bytes=vmem)


def _dot(a, b):
    return jnp.dot(a, b, preferred_element_type=F32)


def _dot_nt(a, b):
    return lax.dot_general(a, b, (((1,), (1,)), ((), ())), preferred_element_type=F32)


def _rms(x, g):
    return x * lax.rsqrt(jnp.mean(x * x, axis=-1, keepdims=True) + EPS) * g


def _split3(x):
    hi = x.astype(BF16)
    r1 = x - hi.astype(F32)
    mid = r1.astype(BF16)
    lo = (r1 - mid.astype(F32)).astype(BF16)
    return hi, mid, lo


def _sigmoid(x):
    return 1.0 / (1.0 + jnp.exp(-x))


def _log_sigmoid_neg(z):
    return -(jnp.maximum(z, 0.0) + jnp.log(1.0 + jnp.exp(-jnp.abs(z))))


def _proj_body(x_ref, g_ref, w_ref, sc_ref, of_ref, ob_ref):
    y = _rms(x_ref[...], g_ref[...]).astype(BF16)
    r = _dot(y, w_ref[...])
    of_ref[...] = r
    ob_ref[...] = (r * sc_ref[...]).astype(BF16)


def _proj(x, g, w, scales, tm):
    n, d = x.shape
    c = w.shape[1]
    return pl.pallas_call(
        _proj_body,
        grid=(n // tm,),
        in_specs=[pl.BlockSpec((tm, d), lambda i: (i, 0)),
                  pl.BlockSpec((1, d), lambda i: (0, 0)),
                  pl.BlockSpec((d, c), lambda i: (0, 0)),
                  pl.BlockSpec((1, c), lambda i: (0, 0))],
        out_specs=[pl.BlockSpec((tm, c), lambda i: (i, 0)),
                   pl.BlockSpec((tm, c), lambda i: (i, 0))],
        out_shape=[jax.ShapeDtypeStruct((n, c), F32), jax.ShapeDtypeStruct((n, c), BF16)],
        compiler_params=_params(("parallel",)),
        name="proj",
    )(x, g.reshape(1, d), w, scales.reshape(1, c))


def _headnorm_body(x_ref, g_ref, sc_ref, cos_ref, sin_ref, s_ref, of_ref, ob_ref, *, rotary, nblk):
    s = s_ref[...]
    for j in range(nblk):
        cols = slice(j * LANES, (j + 1) * LANES)
        x = x_ref[:, cols]
        hi, mid, lo = _split3(x * x)
        ms = (_dot(hi, s) + _dot(mid, s) + _dot(lo, s)) * (1.0 / HEAD_DIM)
        y = x * lax.rsqrt(ms + EPS) * g_ref[:, cols]
        if rotary:
            lane = lax.broadcasted_iota(I32, (1, LANES), 1) % HEAD_DIM
            partner = jnp.where(lane < ROT_DIM // 2,
                                pltpu.roll(y, LANES - ROT_DIM // 2, 1),
                                pltpu.roll(y, ROT_DIM // 2, 1))
            y = y * cos_ref[...] + partner * sin_ref[...]
        of_ref[:, cols] = y
        ob_ref[:, cols] = (y * sc_ref[:, cols]).astype(BF16)


def _headnorm(x, col0, gains, scales, cos, sin, tm):
    n = x.shape[0]
    width = gains.shape[0]
    nblk = width // LANES
    assert col0 % width == 0
    rotary = cos is not None
    if not rotary:
        cos = jnp.zeros((8, LANES), F32)
        sin = cos
        tab_spec = pl.BlockSpec((8, LANES), lambda i: (0, 0))
    else:
        nt = cos.shape[0] // tm
        tab_spec = pl.BlockSpec((tm, LANES), lambda i: (i % nt, 0))
    r = lax.broadcasted_iota(I32, (LANES, LANES), 0) // HEAD_DIM
    c = lax.broadcasted_iota(I32, (LANES, LANES), 1) // HEAD_DIM
    seg = (r == c).astype(BF16)
    return pl.pallas_call(
        functools.partial(_headnorm_body, rotary=rotary, nblk=nblk),
        grid=(n // tm,),
        in_specs=[pl.BlockSpec((tm, width), lambda i: (i, col0 // width)),
                  pl.BlockSpec((1, width), lambda i: (0, 0)),
                  pl.BlockSpec((1, width), lambda i: (0, 0)),
                  tab_spec, tab_spec,
                  pl.BlockSpec((LANES, LANES), lambda i: (0, 0))],
        out_specs=[pl.BlockSpec((tm, width), lambda i: (i, 0)),
                   pl.BlockSpec((tm, width), lambda i: (i, 0))],
        out_shape=[jax.ShapeDtypeStruct((n, width), F32),
                   jax.ShapeDtypeStruct((n, width), BF16)],
        compiler_params=_params(("parallel",)),
        name="headnorm",
    )(x, gains.reshape(1, width), scales.reshape(1, width), cos, sin, seg)


def _res_body(*refs, n_in):
    h_ref, o_ref = refs[0], refs[-1]
    acc = h_ref[...]
    for t in range(n_in):
        acc = acc + _dot(refs[1 + t][...], refs[1 + n_in + t][...])
    o_ref[...] = acc


def _residual_matmul(h, acts, ws, tm):
    n, d = h.shape
    n_in = len(acts)
    in_specs = [pl.BlockSpec((tm, d), lambda i: (i, 0))]
    in_specs += [pl.BlockSpec((tm, a.shape[1]), lambda i: (i, 0)) for a in acts]
    in_specs += [pl.BlockSpec(w.shape, lambda i: (0, 0)) for w in ws]
    return pl.pallas_call(
        functools.partial(_res_body, n_in=n_in),
        grid=(n // tm,),
        in_specs=in_specs,
        out_specs=pl.BlockSpec((tm, d), lambda i: (i, 0)),
        out_shape=jax.ShapeDtypeStruct((n, d), F32),
        compiler_params=_params(("parallel",)),
        name="residual_matmul",
    )(h, *acts, *ws)


def _ffn_body(h_ref, g_ref, w1_ref, w3_ref, w2_ref, o_ref):
    x = h_ref[...]
    xn = _rms(x, g_ref[...]).astype(BF16)
    a = _dot(xn, w1_ref[...])
    b = _dot(xn, w3_ref[...])
    hm = (a * _sigmoid(a) * b).astype(BF16)
    o_ref[...] = x + _dot(hm, w2_ref[...])


def _ffn(h, g, w1, w3, w2, tm):
    n, d = h.shape
    f = w1.shape[1]
    once = pl.Buffered(1)
    return pl.pallas_call(
        _ffn_body,
        grid=(n // tm,),
        in_specs=[pl.BlockSpec((tm, d), lambda i: (i, 0)),
                  pl.BlockSpec((1, d), lambda i: (0, 0)),
                  pl.BlockSpec((d, f), lambda i: (0, 0), pipeline_mode=once),
                  pl.BlockSpec((d, f), lambda i: (0, 0), pipeline_mode=once),
                  pl.BlockSpec((f, d), lambda i: (0, 0), pipeline_mode=once)],
        out_specs=pl.BlockSpec((tm, d), lambda i: (i, 0)),
        out_shape=jax.ShapeDtypeStruct((n, d), F32),
        compiler_params=_params(("parallel",)),
        name="ffn",
    )(h, g.reshape(1, d), w1, w3, w2)


def _ple_body(h_ref, g_ref, wg_ref, p_ref, wp_ref, o_ref):
    x = h_ref[...]
    xn = _rms(x, g_ref[...]).astype(BF16)
    gate = _sigmoid(_dot(xn, wg_ref[...]))
    o_ref[...] = x + gate * _dot(p_ref[...].astype(BF16), wp_ref[...])


def _ple(h, g, wg, p, wp, tm):
    n, d = h.shape
    e = p.shape[1]
    return pl.pallas_call(
        _ple_body,
        grid=(n // tm,),
        in_specs=[pl.BlockSpec((tm, d), lambda i: (i, 0)),
                  pl.BlockSpec((1, d), lambda i: (0, 0)),
                  pl.BlockSpec((d, d), lambda i: (0, 0)),
                  pl.BlockSpec((tm, e), lambda i: (i, 0)),
                  pl.BlockSpec((e, d), lambda i: (0, 0))],
        out_specs=pl.BlockSpec((tm, d), lambda i: (i, 0)),
        out_shape=jax.ShapeDtypeStruct((n, d), F32),
        compiler_params=_params(("parallel",)),
        name="ple",
    )(h, g.reshape(1, d), wg, p, wp)


def _sb_body(q_ref, k_ref, v_ref, o_ref, *, tq, tk, q_pos0, n_kb_total):
    row0 = q_pos0 + pl.program_id(2) * tq
    q = q_ref[0]
    lane = lax.broadcasted_iota(I32, (1, LANES), 1)
    lo_half = lane < HEAD_DIM
    zero = jnp.zeros_like(q)
    q_halves = (jnp.where(lo_half, q, zero), jnp.where(lo_half, zero, q))
    qpos = row0 + lax.broadcasted_iota(I32, (tq, 1), 0)
    nkb = jnp.minimum((row0 + tq - 1 + tk - 1) // tk, n_kb_total)
    n_full = jnp.minimum(row0 // tk, nkb)
    later = (lax.broadcasted_iota(I32, (tk, tk), 0) >
             lax.broadcasted_iota(I32, (tk, tk), 1)).astype(BF16)

    def block(half, kb, c, masked):
        ks = pl.multiple_of(kb * tk, tk)
        z = _dot_nt(q_halves[half], k_ref[0, pl.ds(ks, tk), :])
        sp = jnp.maximum(z, 0.0) + jnp.log2(1.0 + jnp.exp2(-jnp.abs(z)))
        if masked:
            vis = (ks + lax.broadcasted_iota(I32, (1, tk), 1)) < qpos
            sp = jnp.where(vis, sp, 0.0)
        between = _dot(sp.astype(BF16), later)
        w = jnp.exp2(z - sp - between - c)
        if masked:
            w = jnp.where(vis, w, 0.0)
        row_sum = between[:, 0:1] + sp[:, 0:1]
        return _dot(w.astype(BF16), v_ref[0, pl.ds(ks, tk), :]), row_sum

    def run(kbs, carry, masked):
        accs, cs = carry
        new_accs, new_cs = [], []
        for half in range(2):
            acc, c = accs[half], cs[half]
            for kb in kbs:
                pv, rs = block(half, kb, c, masked)
                acc, c = acc + pv, c + rs
            new_accs.append(acc)
            new_cs.append(c)
        return tuple(new_accs), tuple(new_cs)

    acc0 = jnp.zeros((tq, LANES), F32)
    c0 = jnp.zeros((tq, 1), F32)
    carry = ((acc0, acc0), (c0, c0))
    carry = lax.fori_loop(0, nkb - n_full,
                          lambda j, cr: run([nkb - 1 - j], cr, True), carry)
    def live(cr):
        return jnp.minimum(jnp.min(cr[1][0]), jnp.min(cr[1][1])) < EXP2_ZERO

    def steps(n_steps, kbs_of, carry):
        def cond(state):
            j, _, more = state
            return (j < n_steps) & more

        def body(state):
            j, cr, _ = state
            cr = run(kbs_of(j), cr, False)
            return j + 1, cr, live(cr)

        return lax.while_loop(cond, body, (0, carry, live(carry)))[1]

    grp = SB_BLOCKS_PER_STEP
    carry = steps(n_full // grp, lambda j: [n_full - 1 - grp * j - u for u in range(grp)], carry)
    rem = n_full % grp
    carry = steps(rem, lambda j: [rem - 1 - j], carry)
    accs, _ = carry
    o_ref[0] = jnp.where(lo_half, accs[0], accs[1]).astype(BF16)


def _sb_attention(q, q_cb, k, k_cb, v, v_cb, n_pairs, q_pos0, tq, tk):
    b, t = q.shape[:2]
    l = k.shape[1]
    return pl.pallas_call(
        functools.partial(_sb_body, tq=tq, tk=tk, q_pos0=q_pos0, n_kb_total=l // tk),
        grid=(b, n_pairs, t // tq),
        in_specs=[pl.BlockSpec((1, tq, LANES), lambda bi, p, i: (bi, i, q_cb + p)),
                  pl.BlockSpec((1, l, LANES), lambda bi, p, i: (bi, 0, k_cb + p)),
                  pl.BlockSpec((1, l, LANES), lambda bi, p, i: (bi, 0, v_cb + p))],
        out_specs=pl.BlockSpec((1, tq, LANES), lambda bi, p, i: (bi, i, p)),
        out_shape=jax.ShapeDtypeStruct((b, t, n_pairs * LANES), BF16),
        compiler_params=_params(("parallel", "parallel", "parallel")),
        name="sb_attention",
    )(q, k, v)


def _softmax_step(s, v, m, l, acc):
    m_new = jnp.maximum(m, jnp.max(s, axis=1, keepdims=True))
    alpha = jnp.exp2(m - m_new)
    p = jnp.exp2(s - m_new)
    l_new = alpha * l + jnp.sum(p, axis=1, keepdims=True)
    acc_new = alpha * acc + _dot(p.astype(BF16), v)
    return m_new, l_new, acc_new


def _fox_body(q_ref, k_ref, v_ref, fk_ref, thr_ref, o_ref, q_s,
              *, tq, tk, q_pos0, n_kb_total, npp):
    row0 = q_pos0 + pl.program_id(2) * tq
    lane = lax.broadcasted_iota(I32, (1, LANES), 1)
    lo_half = lane < HEAD_DIM
    for pp in range(npp):
        q = q_ref[0, :, pp * LANES:(pp + 1) * LANES]
        zero = jnp.zeros_like(q)
        q_s[2 * pp] = jnp.where(lo_half, q, zero)
        q_s[2 * pp + 1] = jnp.where(lo_half, zero, q)
    qpos = row0 + lax.broadcasted_iota(I32, (tq, 1), 0)
    nkb = jnp.minimum((row0 + tq + tk - 1) // tk, n_kb_total)
    n_full = jnp.minimum((row0 + 1) // tk, nkb)

    def body(j, carry, masked, width):
        ks = pl.multiple_of(j * tk, tk)
        out = []
        for h in range(2 * npp):
            pp, half = divmod(h, 2)
            k = k_ref[0, pl.ds(ks, width), pp * LANES:(pp + 1) * LANES]
            v = v_ref[0, pl.ds(ks, width), pp * LANES:(pp + 1) * LANES]
            m, l, acc = carry[h]
            fk = fk_ref[0, pp, half:half + 1, pl.ds(ks, width)] * LOG2E
            s = _dot_nt(q_s[h], k) - fk
            if masked:
                vis = (ks + lax.broadcasted_iota(I32, (1, width), 1)) <= qpos
                s = jnp.where(vis, s, NEG_BIG)
            out.append(_softmax_step(s, v, m, l, acc))
        return tuple(out)

    win = max(tq, LANES)
    l_keys = n_kb_total * tk
    pos = lax.broadcasted_iota(I32, (1, l_keys), 1).astype(F32)
    first_needed = []
    for h in range(2 * npp):
        pp, half = divmod(h, 2)
        f_tile = fk_ref[0, pp, half:half + 1, pl.ds(pl.multiple_of(row0, LANES), win)] * LOG2E
        f_all = fk_ref[0, pp, half:half + 1, :] * LOG2E
        gone = (f_all - jnp.max(f_tile, axis=1, keepdims=True)) > thr_ref[0, pp, half:half + 1, 0:1]
        first_needed.append(jnp.min(jnp.where(gone, float(l_keys), pos)))
    j0 = jnp.minimum(functools.reduce(jnp.minimum, first_needed).astype(I32) // tk, n_full)

    init = (jnp.full((tq, 1), M_INIT, F32), jnp.zeros((tq, 1), F32), jnp.zeros((tq, LANES), F32))
    n_wide = (n_full - j0) // 2
    carry = lax.fori_loop(0, n_wide,
                          lambda j, c: body(j0 + 2 * j, c, masked=False, width=2 * tk),
                          (init,) * (2 * npp))
    carry = lax.fori_loop(j0 + 2 * n_wide, n_full,
                          functools.partial(body, masked=False, width=tk), carry)
    carry = lax.fori_loop(n_full, nkb, functools.partial(body, masked=True, width=tk), carry)
    for pp in range(npp):
        (_, l0, a0), (_, l1, a1) = carry[2 * pp], carry[2 * pp + 1]
        o_ref[0, :, pp * LANES:(pp + 1) * LANES] = jnp.where(lo_half, a0 / l0, a1 / l1).astype(BF16)


def _fox_attention(q, q_cb, k, k_cb, v, v_cb, fk, thr, q_pos0, tq, tk):
    b, t = q.shape[:2]
    l = k.shape[1]
    n_pairs = H_FOX // 2
    npp = FOX_PAIRS
    w = npp * LANES
    assert q_cb % npp == 0 and k_cb % npp == 0 and v_cb % npp == 0
    assert q_pos0 % LANES == 0 and (tq % LANES == 0 or t == tq)
    return pl.pallas_call(
        functools.partial(_fox_body, tq=tq, tk=tk, q_pos0=q_pos0, n_kb_total=l // tk, npp=npp),
        grid=(b, n_pairs // npp, t // tq),
        in_specs=[pl.BlockSpec((1, tq, w), lambda bi, p, i: (bi, i, q_cb // npp + p)),
                  pl.BlockSpec((1, l, w), lambda bi, p, i: (bi, 0, k_cb // npp + p)),
                  pl.BlockSpec((1, l, w), lambda bi, p, i: (bi, 0, v_cb // npp + p)),
                  pl.BlockSpec((1, npp, 2, l), lambda bi, p, i: (bi, p, 0, 0)),
                  pl.BlockSpec((1, npp, 2, LANES), lambda bi, p, i: (bi, p, 0, 0))],
        out_specs=pl.BlockSpec((1, tq, w), lambda bi, p, i: (bi, i, p)),
        out_shape=jax.ShapeDtypeStruct((b, t, n_pairs * LANES), BF16),
        scratch_shapes=[pltpu.VMEM((2 * npp, tq, LANES), BF16)],
        compiler_params=_params(("parallel", "parallel", "parallel")),
        name="fox_attention",
    )(q, k, v, fk, thr)


def _forget_body(raw_ref, b_ref, logf_ref, cum_ref, carry_ref, *, tb, n_given):
    j = pl.program_id(1)

    @pl.when(j == 0)
    def _():
        carry_ref[...] = jnp.zeros_like(carry_ref)

    raw = raw_ref[0]
    z = raw + b_ref[...]
    computed = _log_sigmoid_neg(-z)
    row = j * tb + lax.broadcasted_iota(I32, (tb, 1), 0)
    logf = jnp.where(row < n_given, raw, computed)
    incl = (lax.broadcasted_iota(I32, (tb, tb), 1) <=
            lax.broadcasted_iota(I32, (tb, tb), 0)).astype(BF16)
    hi, mid, lo = _split3(logf)
    cum = _dot(incl, hi) + _dot(incl, mid) + _dot(incl, lo) + carry_ref[0:1, :]
    logf_ref[0] = logf
    cum_ref[0] = cum
    carry_ref[0:1, :] = cum[tb - 1:tb, :]


def _forget_cumsum(raw, bias, n_given, tb):
    raw = jnp.pad(raw, ((0, 0), (0, 0), (0, LANES - raw.shape[2])))
    bias = jnp.pad(bias, (0, LANES - bias.shape[0]))
    b, l, h = raw.shape
    return pl.pallas_call(
        functools.partial(_forget_body, tb=tb, n_given=n_given),
        grid=(b, l // tb),
        in_specs=[pl.BlockSpec((1, tb, h), lambda bi, j: (bi, j, 0)),
                  pl.BlockSpec((1, h), lambda bi, j: (0, 0))],
        out_specs=[pl.BlockSpec((1, tb, h), lambda bi, j: (bi, j, 0)),
                   pl.BlockSpec((1, tb, h), lambda bi, j: (bi, j, 0))],
        out_shape=[jax.ShapeDtypeStruct((b, l, h), F32), jax.ShapeDtypeStruct((b, l, h), F32)],
        scratch_shapes=[pltpu.VMEM((8, h), F32)],
        compiler_params=_params(("parallel", "arbitrary")),
        name="forget_cumsum",
    )(raw, bias.reshape(1, h))


def _dsa_body(q_ref, qi_ref, wi_ref, k_ref, v_ref, ki_ref, o_ref,
              key_s, bias_s, qi_s, q_s, *, tq, tk, q_pos0, l_valid, n_kb_total, n_sel):
    row0 = q_pos0 + pl.program_id(1) * tq
    lane = lax.broadcasted_iota(I32, (1, LANES), 1)
    lo_half = lane < HEAD_DIM
    qpos = row0 + lax.broadcasted_iota(I32, (tq, 1), 0)
    qchunk = qpos // CHUNK
    kend = jnp.minimum(((row0 + tq - 1) // CHUNK + 1) * CHUNK, l_valid)
    nkb = jnp.minimum(jnp.maximum((kend + tk - 1) // tk, (n_sel + tk - 1) // tk), n_kb_total)

    n_stack = IDX_HEADS // 2
    for p in range(n_stack):
        blk = qi_ref[0, :, p * LANES:(p + 1) * LANES]
        zero = jnp.zeros_like(blk)
        qi_s[0, p * tq:(p + 1) * tq, :] = jnp.where(lo_half, blk, zero)
        qi_s[1, p * tq:(p + 1) * tq, :] = jnp.where(lo_half, zero, blk)
    wsc = wi_ref[0] * (IDX_DIM ** -0.5 * IDX_HEADS ** -0.5)

    def wide_then_single(step, init):
        carry = lax.fori_loop(0, nkb // 2, lambda j, c: step(2 * j, c, 2 * tk), init)
        return lax.fori_loop(nkb // 2 * 2, nkb, lambda j, c: step(j, c, tk), carry)

    def score_body(j, _, width):
        for u in range(width // SCORE_SUB):
            ks = pl.multiple_of(j * tk + u * SCORE_SUB, SCORE_SUB)
            ki = ki_ref[0, pl.ds(ks, SCORE_SUB), :]
            score = jnp.zeros((tq, SCORE_SUB), F32)
            for half in range(2):
                rel = jnp.maximum(_dot_nt(qi_s[half], ki), 0.0)
                for p in range(n_stack):
                    h = 2 * p + half
                    score = score + wsc[:, h:h + 1] * rel[p * tq:(p + 1) * tq]
            kpos = ks + lax.broadcasted_iota(I32, (1, SCORE_SUB), 1)
            vis = ((kpos // CHUNK) <= qchunk) & (kpos < l_valid)
            score = jnp.where(vis, score, -jnp.inf)
            bits = lax.bitcast_convert_type(score, I32)
            key = bits ^ ((bits >> 31) & INT_MAX)
            key_s[:, pl.ds(ks, SCORE_SUB)] = jnp.where(bits == INT_MIN, 0, key)
        return 0

    wide_then_single(score_body, 0)

    def count(pred):
        def cbody(j, acc):
            blk = key_s[:, pl.ds(pl.multiple_of(j * tk, tk), tk)]
            hit = jnp.where(pred(blk, j), 1.0, 0.0)
            for c in range(tk // LANES):
                acc = acc + hit[:, c * LANES:(c + 1) * LANES]
            return acc
        acc = lax.fori_loop(0, nkb, cbody, jnp.zeros((tq, LANES), F32))
        return jnp.sum(acc, axis=1, keepdims=True)

    n_sel_f = float(n_sel)
    zero_col = jnp.zeros((tq, 1), I32)
    c_zero = count(lambda blk, j: blk >= zero_col)
    thr = jnp.where(c_zero >= n_sel_f, 0, INT_MIN)
    c_ge = jnp.where(c_zero >= n_sel_f, c_zero, (nkb * tk).astype(F32))

    def unsettled(c_ge):
        return jnp.max(jnp.where(c_ge == n_sel_f, 0.0, 1.0)) > 0.0

    def bit_cond(state):
        b, _, _, more = state
        return (b < 31) & more

    def bit_body(state):
        b, thr, c_ge, _ = state
        cand = thr | (1 << (30 - b))
        c = count(lambda blk, j: blk >= cand)
        keep = c >= n_sel_f
        c_ge = jnp.where(keep, c, c_ge)
        return b + 1, jnp.where(keep, cand, thr), c_ge, unsettled(c_ge)

    _, thr, c_ge, _ = lax.while_loop(bit_cond, bit_body, (0, thr, c_ge, unsettled(c_ge)))
    tied = (c_ge > n_sel_f) & (thr > KEY_NEG_INF)
    any_tied = jnp.max(jnp.where(tied, 1.0, 0.0)) > 0.0

    def tie_index_bound():
        need = n_sel_f - count(lambda blk, j: blk > thr)

        def kidx(j):
            return j * tk + lax.broadcasted_iota(I32, (1, tk), 1)

        n_bits = (n_kb_total * tk - 1).bit_length()

        def jbody(b, jb):
            cand = jb | (1 << (n_bits - 1 - b))
            cnt = count(lambda blk, j: (blk == thr) & (kidx(j) < cand))
            return jnp.where(cnt < need, cand, jb)

        jb = lax.fori_loop(0, n_bits, jbody, jnp.zeros((tq, 1), I32))
        return jnp.where(tied, jb, INT_MAX)

    jbound = lax.cond(any_tied, tie_index_bound, lambda: jnp.full((tq, 1), INT_MAX, I32))

    def bias_body(j, _):
        ks = pl.multiple_of(j * tk, tk)
        blk = key_s[:, pl.ds(ks, tk)]
        kpos = ks + lax.broadcasted_iota(I32, (1, tk), 1)
        sel = (blk > thr) | ((blk == thr) & (kpos <= jbound))
        sel = sel & (blk > KEY_NEG_INF)
        bias_s[:, pl.ds(ks, tk)] = jnp.where(sel, 0.0, NEG_BIG)
        return 0

    lax.fori_loop(0, nkb, bias_body, 0)

    n_grp = H_DSA // KV_DSA
    for p in range(n_grp):
        blk = q_ref[0, :, p * LANES:(p + 1) * LANES]
        zero = jnp.zeros_like(blk)
        q_s[0, p * tq:(p + 1) * tq, :] = jnp.where(lo_half, blk, zero)
        q_s[1, p * tq:(p + 1) * tq, :] = jnp.where(lo_half, zero, blk)

    def att_body(j, carry, width):
        ks = pl.multiple_of(j * tk, tk)
        k = k_ref[0, pl.ds(ks, width), :]
        v = v_ref[0, pl.ds(ks, width), :]
        bias = bias_s[:, pl.ds(ks, width)]
        bias = jnp.concatenate([bias] * n_grp, axis=0)
        return tuple(_softmax_step(_dot_nt(q_s[half], k) + bias, v, *carry[half])
                     for half in range(2))

    rows = n_grp * tq
    init = (jnp.full((rows, 1), M_INIT, F32), jnp.zeros((rows, 1), F32),
            jnp.zeros((rows, LANES), F32))
    (_, l0, a0), (_, l1, a1) = wide_then_single(att_body, (init, init))
    o0, o1 = a0 / l0, a1 / l1
    for p in range(n_grp):
        o_ref[0, :, p * LANES:(p + 1) * LANES] = jnp.where(
            lo_half, o0[p * tq:(p + 1) * tq], o1[p * tq:(p + 1) * tq]).astype(BF16)


def _dsa_attention(q, qi, qi_cb, wi, wi_cb, k, k_cb, v, v_cb, ki, ki_cb,
                   q_pos0, l_valid, n_sel, tq, tk):
    b, t = q.shape[:2]
    l = k.shape[1]
    width = H_DSA * HEAD_DIM
    return pl.pallas_call(
        functools.partial(_dsa_body, tq=tq, tk=tk, q_pos0=q_pos0, l_valid=l_valid,
                          n_kb_total=l // tk, n_sel=n_sel),
        grid=(b, t // tq),
        in_specs=[pl.BlockSpec((1, tq, width), lambda bi, i: (bi, i, 0)),
                  pl.BlockSpec((1, tq, width), lambda bi, i: (bi, i, qi_cb)),
                  pl.BlockSpec((1, tq, LANES), lambda bi, i: (bi, i, wi_cb)),
                  pl.BlockSpec((1, l, LANES), lambda bi, i: (bi, 0, k_cb)),
                  pl.BlockSpec((1, l, LANES), lambda bi, i: (bi, 0, v_cb)),
                  pl.BlockSpec((1, l, LANES), lambda bi, i: (bi, 0, ki_cb))],
        out_specs=pl.BlockSpec((1, tq, width), lambda bi, i: (bi, i, 0)),
        out_shape=jax.ShapeDtypeStruct((b, t, width), BF16),
        scratch_shapes=[pltpu.VMEM((tq, l), I32), pltpu.VMEM((tq, l), F32),
                        pltpu.VMEM((2, IDX_HEADS // 2 * tq, LANES), BF16),
                        pltpu.VMEM((2, H_DSA // KV_DSA * tq, LANES), BF16)],
        compiler_params=_params(("parallel", "parallel")),
        name="dsa_attention",
    )(q, qi, wi, k, v, ki)


def _moe_body(h_ref, g_ref, wr_ref, w1_ref, w3_ref, w2_ref, o_ref,
              xn_s, xg_s, ye_s, rank_s, gate_s, rank_t_s, *, tm, ch, n_fc):
    e = pl.program_id(1)
    fc = pl.program_id(2)
    lane = lax.broadcasted_iota(I32, (1, LANES), 1)

    @pl.when((e == 0) & (fc == 0))
    def _route():
        x = h_ref[...]
        xn = _rms(x, g_ref[...])
        xn_s[...] = xn.astype(BF16)
        o_ref[...] = x
        x3 = _split3(xn)
        logits = jnp.zeros((tm, LANES), F32)
        for a, b in ((2, 0), (0, 2), (1, 1), (1, 0), (0, 1), (0, 0)):
            logits = logits + _dot(x3[a], wr_ref[b])
        lane_f = lane.astype(F32)
        logits = jnp.where(lane < N_EXPERTS, logits, -jnp.inf)
        m1 = jnp.max(logits, axis=1, keepdims=True)
        i1 = jnp.min(jnp.where(logits == m1, lane_f, float(LANES)), axis=1, keepdims=True)
        rest = jnp.where(lane_f == i1, -jnp.inf, logits)
        m2 = jnp.max(rest, axis=1, keepdims=True)
        i2 = jnp.min(jnp.where(rest == m2, lane_f, float(LANES)), axis=1, keepdims=True)
        e2 = jnp.exp(m2 - m1)
        g1 = 1.0 / (1.0 + e2)
        g2 = e2 / (1.0 + e2)
        sel1 = lane_f == i1
        sel2 = lane_f == i2
        gate_s[...] = jnp.where(sel1, g1, 0.0) + jnp.where(sel2, g2, 0.0)
        sel = jnp.where(sel1 | sel2, 1.0, 0.0)
        incl = (lax.broadcasted_iota(I32, (tm, tm), 1) <=
                lax.broadcasted_iota(I32, (tm, tm), 0)).astype(BF16)
        rank = _dot(incl, sel.astype(BF16)) * sel
        rank_s[...] = rank
        rank_t_s[...] = rank.T

    rank_row = rank_t_s[pl.ds(e, 1), :]
    cnt = jnp.max(rank_row).astype(I32)
    nch = (cnt + ch - 1) // ch

    @pl.when(fc == 0)
    def _gather():
        def gbody(c, _):
            base = pl.multiple_of(c * ch, ch)
            slot = (base + 1 + lax.broadcasted_iota(I32, (ch, 1), 0)).astype(F32)
            pick = jnp.where(rank_row == slot, 1.0, 0.0).astype(BF16)
            xg_s[pl.ds(base, ch), :] = _dot(pick, xn_s[...]).astype(BF16)
            return 0
        lax.fori_loop(0, nch, gbody, 0)

    def fbody(c, _):
        base = pl.multiple_of(c * ch, ch)
        xg = xg_s[pl.ds(base, ch), :]
        a = _dot(xg, w1_ref[0])
        b = _dot(xg, w3_ref[0])
        part = _dot((a * _sigmoid(a) * b).astype(BF16), w2_ref[0])

        @pl.when(fc == 0)
        def _():
            ye_s[pl.ds(base, ch), :] = part

        @pl.when(fc != 0)
        def _():
            ye_s[pl.ds(base, ch), :] += part
        return 0

    lax.fori_loop(0, nch, fbody, 0)

    @pl.when(fc == n_fc - 1)
    def _scatter():
        here = lane == e
        rank_col = jnp.sum(jnp.where(here, rank_s[...], 0.0), axis=1, keepdims=True)
        gate_col = jnp.sum(jnp.where(here, gate_s[...], 0.0), axis=1, keepdims=True)

        def sbody(c, _):
            base = pl.multiple_of(c * ch, ch)
            slot = (base + 1 + lax.broadcasted_iota(I32, (1, ch), 1)).astype(F32)
            place = jnp.where(rank_col == slot, 1.0, 0.0).astype(BF16)
            ye = ye_s[pl.ds(base, ch), :].astype(BF16)
            o_ref[...] += gate_col * _dot(place, ye)
            return 0
        lax.fori_loop(0, nch, sbody, 0)


def _moe(h, g, wr3, w1, w3, w2, tm, ch, tf):
    n, d = h.shape
    ne, _, f = w1.shape
    n_fc = f // tf
    return pl.pallas_call(
        functools.partial(_moe_body, tm=tm, ch=ch, n_fc=n_fc),
        grid=(n // tm, ne, n_fc),
        in_specs=[pl.BlockSpec((tm, d), lambda i, e, c: (i, 0)),
                  pl.BlockSpec((1, d), lambda i, e, c: (0, 0)),
                  pl.BlockSpec((3, d, LANES), lambda i, e, c: (0, 0, 0)),
                  pl.BlockSpec((1, d, tf), lambda i, e, c: (e, 0, c)),
                  pl.BlockSpec((1, d, tf), lambda i, e, c: (e, 0, c)),
                  pl.BlockSpec((1, tf, d), lambda i, e, c: (e, c, 0))],
        out_specs=pl.BlockSpec((tm, d), lambda i, e, c: (i, 0)),
        out_shape=jax.ShapeDtypeStruct((n, d), F32),
        scratch_shapes=[pltpu.VMEM((tm, d), BF16), pltpu.VMEM((_round_up(tm, ch), d), BF16),
                        pltpu.VMEM((_round_up(tm, ch), d), F32), pltpu.VMEM((tm, LANES), F32),
                        pltpu.VMEM((tm, LANES), F32), pltpu.VMEM((LANES, tm), F32)],
        compiler_params=_params(("parallel", "arbitrary", "arbitrary")),
        name="moe",
    )(h, g.reshape(1, d), wr3, w1, w3, w2)


def _round_up(x, m):
    return (x + m - 1) // m * m


def _pad_rows(a, l_pad):
    return jnp.pad(a, ((0, 0), (0, l_pad - a.shape[1]), (0, 0)))


def _prep_weights(g_q_dsa, g_k_dsa, g_q_fox, g_k_fox, w_in_even, w_out_even, w_in_odd, w_router):
    d = w_in_even.shape[0]
    perm = jnp.asarray(DSA_HEAD_PERM)
    qa, ka, va, qb, kb, vb, qi, ki, wi = jnp.split(
        w_in_even, [512, 1024, 1536, 2048, 2176, 2304, 2816, 2880], axis=1)
    qb = qb.reshape(d, H_DSA, HEAD_DIM)[:, perm].reshape(d, H_DSA * HEAD_DIM)
    wi = jnp.pad(wi, ((0, 0), (0, LANES - IDX_HEADS)))
    w_even = jnp.concatenate([qb, kb, vb, ki, ki, wi, qi, qa, ka, va], axis=1).astype(BF16)
    ones = lambda k: jnp.ones((k,), F32)
    q_scale = lambda k: jnp.full((k,), Q_SCALE, F32)
    scale_even = jnp.concatenate([ones(E_QA), q_scale(E_KA - E_QA), ones(E_END - E_KA)])
    scale_odd = ones(O_END)
    scale_norm_even = jnp.concatenate([q_scale(E_KB - E_QB), ones(E_VB - E_KB)])
    scale_norm_odd = jnp.concatenate([q_scale(O_K - O_Q), ones(O_V - O_K)])
    w_out_sb = w_out_even[:H_SB * HEAD_DIM].astype(BF16)
    w_out_dsa = w_out_even[H_SB * HEAD_DIM:].reshape(H_DSA, HEAD_DIM, d)[perm]
    w_out_dsa = w_out_dsa.reshape(H_DSA * HEAD_DIM, d).astype(BF16)
    w_odd = jnp.pad(w_in_odd, ((0, 0), (0, O_END - w_in_odd.shape[1]))).astype(BF16)
    gains_even = jnp.concatenate([jnp.tile(g_q_dsa, H_DSA), jnp.tile(g_k_dsa, KV_DSA)])
    gains_odd = jnp.concatenate([jnp.tile(g_q_fox, H_FOX), jnp.tile(g_k_fox, H_FOX)])
    wr = jnp.pad(w_router, ((0, 0), (0, LANES - N_EXPERTS)))
    wr_hi = wr.astype(BF16)
    wr_r1 = wr - wr_hi.astype(F32)
    wr_mid = wr_r1.astype(BF16)
    wr_lo = (wr_r1 - wr_mid.astype(F32)).astype(BF16)
    return dict(w_even=w_even, w_out_sb=w_out_sb, w_out_dsa=w_out_dsa, w_odd=w_odd,
                gains_even=gains_even, gains_odd=gains_odd,
                scale_even=scale_even, scale_odd=scale_odd,
                scale_norm_even=scale_norm_even, scale_norm_odd=scale_norm_odd,
                wr3=jnp.stack([wr_hi, wr_mid, wr_lo]))


def _rotary_tables(pos):
    half = ROT_DIM // 2
    inv = ROPE_THETA ** (-jnp.arange(half, dtype=F32) / half)
    ang = pos.astype(F32)[:, None] * inv[None, :]
    cos, sin = jnp.cos(ang), jnp.sin(ang)
    t = pos.shape[0]
    pad = HEAD_DIM - ROT_DIM
    cos_h = jnp.concatenate([cos, cos, jnp.ones((t, pad), F32)], axis=1)
    sin_h = jnp.concatenate([-sin, sin, jnp.zeros((t, pad), F32)], axis=1)
    return jnp.tile(cos_h, (1, 2)), jnp.tile(sin_h, (1, 2))


def _trunk(x, p, pos0, past_even, past_odd, w, wb):
    b, t, d = x.shape
    n = b * t
    tm = min(ROW_TILE, n)
    has_past = past_even is not None
    past_len = past_even[0].shape[1] if has_past else 0
    l_valid = past_len + t
    l_pad = _round_up(l_valid, KEY_PAD)
    n_sel = min(DSA_TOPK, l_valid // 4)
    h = x.reshape(n, d)

    def keys(cache, new):
        if not has_past:
            return new
        full = jnp.concatenate([cache.reshape(b, past_len, -1).astype(BF16), new], axis=1)
        return _pad_rows(full, l_pad)

    pf, pb = _proj(h, w["g_mix"][0], wb["w_even"], wb["scale_even"], tm)
    cos, sin = _rotary_tables(pos0 + jnp.arange(t, dtype=I32))
    if t % tm:
        cos, sin = jnp.tile(cos, (tm // t, 1)), jnp.tile(sin, (tm // t, 1))
    qkf, qkb = _headnorm(pf, E_QB, wb["gains_even"], wb["scale_norm_even"], cos, sin, tm)
    pf3, pb3 = pf.reshape(b, t, E_END), pb.reshape(b, t, E_END)
    qkf3, qkb3 = qkf.reshape(b, t, E_NORM), qkb.reshape(b, t, E_NORM)
    cols = lambda a, c0, width: a[:, :, c0:c0 + width]
    sb_w, kv_w = H_SB * HEAD_DIM, KV_DSA * HEAD_DIM
    if has_past:
        c_sbk, c_sbv, c_dk, c_dv, c_ki = past_even
        ka, ka_cb = keys(c_sbk, cols(pb3, E_KA, sb_w)), 0
        va, va_cb = keys(c_sbv, cols(pb3, E_VA, sb_w)), 0
        kb, kb_cb = keys(c_dk, cols(qkb3, E_KB, kv_w)), 0
        vb, vb_cb = keys(c_dv, cols(pb3, E_VB, kv_w)), 0
        ki2 = jnp.concatenate([c_ki, c_ki], axis=-1)
        ki, ki_cb = keys(ki2, cols(pb3, E_KI, LANES)), 0
    else:
        ka, ka_cb = pb3, E_KA // LANES
        va, va_cb = pb3, E_VA // LANES
        kb, kb_cb = qkb3, E_KB // LANES
        vb, vb_cb = pb3, E_VB // LANES
        ki, ki_cb = pb3, E_KI // LANES
    oa = _sb_attention(pb3, E_QA // LANES, ka, ka_cb, va, va_cb, H_SB // 2, pos0,
                       min(SB_Q_TILE, t), SB_K_TILE)
    ob = _dsa_attention(qkb3, pb3, E_QI // 512, pf3, E_WI // LANES, kb, kb_cb, vb, vb_cb,
                        ki, ki_cb, pos0, l_valid, n_sel, min(DSA_Q_TILE, t), DSA_K_TILE)
    h = _residual_matmul(h, [oa.reshape(n, -1), ob.reshape(n, -1)],
                         [wb["w_out_sb"], wb["w_out_dsa"]], tm)
    h = _ffn(h, w["g_ffn"][0], wb["w_ff1"], wb["w_ff3"], wb["w_ff2"], tm)
    h = _ple(h, w["g_ple"][0], wb["w_ple_gate"][0], p[0].reshape(n, -1), wb["w_ple_in"][0], tm)
    even_state = (cols(pf3, E_KA, sb_w).reshape(b, t, H_SB, HEAD_DIM),
                  cols(pf3, E_VA, sb_w).reshape(b, t, H_SB, HEAD_DIM),
                  cols(qkf3, E_KB, kv_w).reshape(b, t, KV_DSA, HEAD_DIM),
                  cols(pf3, E_VB, kv_w).reshape(b, t, KV_DSA, HEAD_DIM),
                  cols(pf3, E_KI, IDX_DIM))

    fox_w = H_FOX * HEAD_DIM
    pf, pb = _proj(h, w["g_mix"][1], wb["w_odd"], wb["scale_odd"], tm)
    qkf, qkb = _headnorm(pf, O_Q, wb["gains_odd"], wb["scale_norm_odd"], None, None, tm)
    pf3, pb3 = pf.reshape(b, t, O_END), pb.reshape(b, t, O_END)
    qkf3, qkb3 = qkf.reshape(b, t, O_V), qkb.reshape(b, t, O_V)
    gate_pre = cols(pf3, O_F, H_FOX)
    if has_past:
        c_fk, c_fv, c_lf = past_odd
        kf, kf_cb = keys(c_fk, cols(qkb3, O_K, fox_w)), 0
        vf, vf_cb = keys(c_fv, cols(pb3, O_V, fox_w)), 0
        raw = _pad_rows(jnp.concatenate([c_lf, gate_pre], axis=1), l_pad)
    else:
        kf, kf_cb = qkb3, O_K // LANES
        vf, vf_cb = pb3, O_V // LANES
        raw = gate_pre
    logf, cum = _forget_cumsum(raw, w["b_forget"][0], past_len, min(KEY_PAD, l_pad))
    logf, cum = logf[:, :, :H_FOX], cum[:, :, :H_FOX]
    fk = cum.reshape(b, l_pad, H_FOX // 2, 2).transpose(0, 2, 3, 1)

    def max_norm(a, c0):
        x = cols(a, c0, fox_w).astype(F32).reshape(b, a.shape[1], H_FOX, HEAD_DIM)
        return jnp.sqrt(jnp.max(jnp.sum(x * x, axis=-1), axis=1))

    decay_cut = EXP2_ZERO + 2.02 * max_norm(qkb3, O_Q) * max_norm(kf, kf_cb * LANES)
    thr = jnp.broadcast_to(decay_cut.reshape(b, H_FOX // 2, 2, 1), (b, H_FOX // 2, 2, LANES))
    of = _fox_attention(qkb3, O_Q // LANES, kf, kf_cb, vf, vf_cb, fk, thr, pos0,
                        min(FOX_Q_TILE, t), FOX_K_TILE)
    h = _residual_matmul(h, [of.reshape(n, -1)], [wb["w_out_odd"]], tm)
    h = _moe(h, w["g_ffn"][1], wb["wr3"], wb["w_exp1"], wb["w_exp3"], wb["w_exp2"],
             min(MOE_TILE, n), min(MOE_CHUNK, n), MOE_F_TILE)
    h = _ple(h, w["g_ple"][1], wb["w_ple_gate"][1], p[1].reshape(n, -1), wb["w_ple_in"][1], tm)
    odd_state = (cols(qkf3, O_K, fox_w).reshape(b, t, H_FOX, HEAD_DIM),
                 cols(pf3, O_V, fox_w).reshape(b, t, H_FOX, HEAD_DIM),
                 logf[:, past_len:l_valid])
    return h.reshape(b, t, d), even_state, odd_state


def kernel(x_prompt, x_sample, p_prompt, p_sample, cache_sb_k, cache_sb_v, cache_dsa_k, cache_dsa_v, cache_dsa_kidx, cache_fox_k, cache_fox_v, cache_fox_logf, g_mix, g_ffn, g_ple, w_in_even, g_q_dsa, g_k_dsa, w_out_even, w_ff1, w_ff3, w_ff2, w_in_odd, b_forget, g_q_fox, g_k_fox, w_out_odd, w_router, w_exp1, w_exp3, w_exp2, w_ple_in, w_ple_gate):
    assert g_mix.shape[0] == 2, "two layers: one even (stick-breaking + DSA), one odd (FoX + experts)"
    past_len = cache_sb_k.shape[2]
    w = dict(g_mix=g_mix, g_ffn=g_ffn, g_ple=g_ple, b_forget=b_forget)
    wb = _prep_weights(g_q_dsa[0], g_k_dsa[0], g_q_fox[0], g_k_fox[0],
                       w_in_even[0], w_out_even[0], w_in_odd[0], w_router[0])
    wb.update(w_ff1=w_ff1[0].astype(BF16), w_ff3=w_ff3[0].astype(BF16), w_ff2=w_ff2[0].astype(BF16),
              w_out_odd=w_out_odd[0].astype(BF16),
              w_exp1=w_exp1[0].astype(BF16), w_exp3=w_exp3[0].astype(BF16),
              w_exp2=w_exp2[0].astype(BF16),
              w_ple_in=w_ple_in.astype(BF16), w_ple_gate=w_ple_gate.astype(BF16))

    y_p, even_p, odd_p = _trunk(x_prompt, p_prompt, 0, None, None, w, wb)
    y_s, even_s, odd_s = _trunk(
        x_sample, p_sample, past_len,
        (cache_sb_k[0], cache_sb_v[0], cache_dsa_k[0], cache_dsa_v[0], cache_dsa_kidx[0]),
        (cache_fox_k[0], cache_fox_v[0], cache_fox_logf[0]), w, wb)
    tail = lambda a: a[:, -past_len:][None]
    whole = lambda a: a[None]
    return (y_p, y_s,
            *(tail(a) for a in even_p), *(tail(a) for a in odd_p),
            *(whole(a) for a in even_s), *(whole(a) for a in odd_s))
```

```python
import functools

import jax
import jax.numpy as jnp
from jax import lax
from jax.experimental import pallas as pl
from jax.experimental.pallas import tpu as pltpu

F32 = jnp.float32
BF16 = jnp.bfloat16
I32 = jnp.int32

EPS = 1e-6
HEAD_DIM = 64
CHUNK = 64
H_SB = 8
H_DSA = 8
KV_DSA = 2
IDX_HEADS = 8
IDX_DIM = 64
H_FOX = 16
DSA_TOPK = 256
ROT_DIM = HEAD_DIM // 4
ROPE_THETA = 500000.0
N_EXPERTS = 8
ATT_SCALE = HEAD_DIM ** -0.5
LOG2E = 1.4426950408889634
Q_SCALE = ATT_SCALE * LOG2E

LANES = 128
MIB = 1024 * 1024
VMEM_LIMIT = 56 * MIB

ROW_TILE = 512
SB_Q_TILE, SB_K_TILE = 256, 256
SB_BLOCKS_PER_STEP = 1
EXP2_ZERO = 150.0
FOX_Q_TILE, FOX_K_TILE = 512, 512
FOX_PAIRS = 1
DSA_Q_TILE, DSA_K_TILE = 256, 512
COUNT_ROWS = 128
SCORE_SUB = 256
KEY_PAD = 512
MOE_TILE = 1024
MOE_CHUNK = 304
MOE_F_TILE = 1792

NEG_BIG = -1e30
M_INIT = -1e29
INT_MIN = -(2 ** 31)
INT_MAX = 2 ** 31 - 1
KEY_NEG_INF = -2139095041

E_QB, E_KB, E_VB, E_KI, E_WI, E_QI, E_QA, E_KA, E_VA, E_END = (
    0, 512, 640, 768, 896, 1024, 1536, 2048, 2560, 3072)
E_NORM = E_VB - E_QB
O_Q, O_K, O_V, O_F, O_END = 0, 1024, 2048, 3072, 3200
DSA_HEAD_PERM = (0, 4, 1, 5, 2, 6, 3, 7)


def _params(sem, vmem=VMEM_LIMIT):
    return pltpu.CompilerParams(dimension_semantics=sem, vmem_limit_bytes=vmem)


def _dot(a, b):
    return jnp.dot(a, b, preferred_element_type=F32)


def _dot_nt(a, b):
    return lax.dot_general(a, b, (((1,), (1,)), ((), ())), preferred_element_type=F32)


def _rms(x, g):
    return x * lax.rsqrt(jnp.mean(x * x, axis=-1, keepdims=True) + EPS) * g


def _split3(x):
    hi = x.astype(BF16)
    r1 = x - hi.astype(F32)
    mid = r1.astype(BF16)
    lo = (r1 - mid.astype(F32)).astype(BF16)
    return hi, mid, lo


def _sigmoid(x):
    return 1.0 / (1.0 + jnp.exp(-x))


def _log_sigmoid_neg(z):
    return -(jnp.maximum(z, 0.0) + jnp.log(1.0 + jnp.exp(-jnp.abs(z))))


def _proj_body(x_ref, g_ref, w_ref, sc_ref, of_ref, ob_ref):
    y = _rms(x_ref[...], g_ref[...]).astype(BF16)
    r = _dot(y, w_ref[...])
    of_ref[...] = r
    ob_ref[...] = (r * sc_ref[...]).astype(BF16)


def _proj(x, g, w, scales, tm):
    n, d = x.shape
    c = w.shape[1]
    return pl.pallas_call(
        _proj_body,
        grid=(n // tm,),
        in_specs=[pl.BlockSpec((tm, d), lambda i: (i, 0)),
                  pl.BlockSpec((1, d), lambda i: (0, 0)),
                  pl.BlockSpec((d, c), lambda i: (0, 0)),
                  pl.BlockSpec((1, c), lambda i: (0, 0))],
        out_specs=[pl.BlockSpec((tm, c), lambda i: (i, 0)),
                   pl.BlockSpec((tm, c), lambda i: (i, 0))],
        out_shape=[jax.ShapeDtypeStruct((n, c), F32), jax.ShapeDtypeStruct((n, c), BF16)],
        compiler_params=_params(("parallel",)),
        name="proj",
    )(x, g.reshape(1, d), w, scales.reshape(1, c))


def _headnorm_body(x_ref, g_ref, sc_ref, cos_ref, sin_ref, s_ref, of_ref, ob_ref, *, rotary, nblk):
    s = s_ref[...]
    for j in range(nblk):
        cols = slice(j * LANES, (j + 1) * LANES)
        x = x_ref[:, cols]
        hi, mid, lo = _split3(x * x)
        ms = (_dot(hi, s) + _dot(mid, s) + _dot(lo, s)) * (1.0 / HEAD_DIM)
        y = x * lax.rsqrt(ms + EPS) * g_ref[:, cols]
        if rotary:
            lane = lax.broadcasted_iota(I32, (1, LANES), 1) % HEAD_DIM
            partner = jnp.where(lane < ROT_DIM // 2,
                                pltpu.roll(y, LANES - ROT_DIM // 2, 1),
                                pltpu.roll(y, ROT_DIM // 2, 1))
            y = y * cos_ref[...] + partner * sin_ref[...]
        of_ref[:, cols] = y
        ob_ref[:, cols] = (y * sc_ref[:, cols]).astype(BF16)


def _headnorm(x, col0, gains, scales, cos, sin, tm):
    n = x.shape[0]
    width = gains.shape[0]
    nblk = width // LANES
    assert col0 % width == 0
    rotary = cos is not None
    if not rotary:
        cos = jnp.zeros((8, LANES), F32)
        sin = cos
        tab_spec = pl.BlockSpec((8, LANES), lambda i: (0, 0))
    else:
        nt = cos.shape[0] // tm
        tab_spec = pl.BlockSpec((tm, LANES), lambda i: (i % nt, 0))
    r = lax.broadcasted_iota(I32, (LANES, LANES), 0) // HEAD_DIM
    c = lax.broadcasted_iota(I32, (LANES, LANES), 1) // HEAD_DIM
    seg = (r == c).astype(BF16)
    return pl.pallas_call(
        functools.partial(_headnorm_body, rotary=rotary, nblk=nblk),
        grid=(n // tm,),
        in_specs=[pl.BlockSpec((tm, width), lambda i: (i, col0 // width)),
                  pl.BlockSpec((1, width), lambda i: (0, 0)),
                  pl.BlockSpec((1, width), lambda i: (0, 0)),
                  tab_spec, tab_spec,
                  pl.BlockSpec((LANES, LANES), lambda i: (0, 0))],
        out_specs=[pl.BlockSpec((tm, width), lambda i: (i, 0)),
                   pl.BlockSpec((tm, width), lambda i: (i, 0))],
        out_shape=[jax.ShapeDtypeStruct((n, width), F32),
                   jax.ShapeDtypeStruct((n, width), BF16)],
        compiler_params=_params(("parallel",)),
        name="headnorm",
    )(x, gains.reshape(1, width), scales.reshape(1, width), cos, sin, seg)


def _res_body(*refs, n_in):
    h_ref, o_ref = refs[0], refs[-1]
    acc = h_ref[...]
    for t in range(n_in):
        acc = acc + _dot(refs[1 + t][...], refs[1 + n_in + t][...])
    o_ref[...] = acc


def _residual_matmul(h, acts, ws, tm):
    n, d = h.shape
    n_in = len(acts)
    in_specs = [pl.BlockSpec((tm, d), lambda i: (i, 0))]
    in_specs += [pl.BlockSpec((tm, a.shape[1]), lambda i: (i, 0)) for a in acts]
    in_specs += [pl.BlockSpec(w.shape, lambda i: (0, 0)) for w in ws]
    return pl.pallas_call(
        functools.partial(_res_body, n_in=n_in),
        grid=(n // tm,),
        in_specs=in_specs,
        out_specs=pl.BlockSpec((tm, d), lambda i: (i, 0)),
        out_shape=jax.ShapeDtypeStruct((n, d), F32),
        compiler_params=_params(("parallel",)),
        name="residual_matmul",
    )(h, *acts, *ws)


def _ffn_body(h_ref, g_ref, w1_ref, w3_ref, w2_ref, o_ref):
    x = h_ref[...]
    xn = _rms(x, g_ref[...]).astype(BF16)
    a = _dot(xn, w1_ref[...])
    b = _dot(xn, w3_ref[...])
    hm = (a * _sigmoid(a) * b).astype(BF16)
    o_ref[...] = x + _dot(hm, w2_ref[...])


def _ffn(h, g, w1, w3, w2, tm):
    n, d = h.shape
    f = w1.shape[1]
    once = pl.Buffered(1)
    return pl.pallas_call(
        _ffn_body,
        grid=(n // tm,),
        in_specs=[pl.BlockSpec((tm, d), lambda i: (i, 0)),
                  pl.BlockSpec((1, d), lambda i: (0, 0)),
                  pl.BlockSpec((d, f), lambda i: (0, 0), pipeline_mode=once),
                  pl.BlockSpec((d, f), lambda i: (0, 0), pipeline_mode=once),
                  pl.BlockSpec((f, d), lambda i: (0, 0), pipeline_mode=once)],
        out_specs=pl.BlockSpec((tm, d), lambda i: (i, 0)),
        out_shape=jax.ShapeDtypeStruct((n, d), F32),
        compiler_params=_params(("parallel",)),
        name="ffn",
    )(h, g.reshape(1, d), w1, w3, w2)


def _ple_body(h_ref, g_ref, wg_ref, p_ref, wp_ref, o_ref):
    x = h_ref[...]
    xn = _rms(x, g_ref[...]).astype(BF16)
    gate = _sigmoid(_dot(xn, wg_ref[...]))
    o_ref[...] = x + gate * _dot(p_ref[...].astype(BF16), wp_ref[...])


def _ple(h, g, wg, p, wp, tm):
    n, d = h.shape
    e = p.shape[1]
    return pl.pallas_call(
        _ple_body,
        grid=(n // tm,),
        in_specs=[pl.BlockSpec((tm, d), lambda i: (i, 0)),
                  pl.BlockSpec((1, d), lambda i: (0, 0)),
                  pl.BlockSpec((d, d), lambda i: (0, 0)),
                  pl.BlockSpec((tm, e), lambda i: (i, 0)),
                  pl.BlockSpec((e, d), lambda i: (0, 0))],
        out_specs=pl.BlockSpec((tm, d), lambda i: (i, 0)),
        out_shape=jax.ShapeDtypeStruct((n, d), F32),
        compiler_params=_params(("parallel",)),
        name="ple",
    )(h, g.reshape(1, d), wg, p, wp)


def _sb_body(q_ref, k_ref, v_ref, o_ref, *, tq, tk, q_pos0, n_kb_total):
    row0 = q_pos0 + pl.program_id(2) * tq
    q = q_ref[0]
    lane = lax.broadcasted_iota(I32, (1, LANES), 1)
    lo_half = lane < HEAD_DIM
    zero = jnp.zeros_like(q)
    q_halves = (jnp.where(lo_half, q, zero), jnp.where(lo_half, zero, q))
    qpos = row0 + lax.broadcasted_iota(I32, (tq, 1), 0)
    nkb = jnp.minimum((row0 + tq - 1 + tk - 1) // tk, n_kb_total)
    n_full = jnp.minimum(row0 // tk, nkb)
    later = (lax.broadcasted_iota(I32, (tk, tk), 0) >
             lax.broadcasted_iota(I32, (tk, tk), 1)).astype(BF16)

    def block(half, kb, c, masked):
        ks = pl.multiple_of(kb * tk, tk)
        z = _dot_nt(q_halves[half], k_ref[0, pl.ds(ks, tk), :])
        sp = jnp.maximum(z, 0.0) + jnp.log2(1.0 + jnp.exp2(-jnp.abs(z)))
        if masked:
            vis = (ks + lax.broadcasted_iota(I32, (1, tk), 1)) < qpos
            sp = jnp.where(vis, sp, 0.0)
        between = _dot(sp.astype(BF16), later)
        w = jnp.exp2(z - sp - between - c)
        if masked:
            w = jnp.where(vis, w, 0.0)
        row_sum = between[:, 0:1] + sp[:, 0:1]
        return _dot(w.astype(BF16), v_ref[0, pl.ds(ks, tk), :]), row_sum

    def run(kbs, carry, masked):
        accs, cs = carry
        new_accs, new_cs = [], []
        for half in range(2):
            acc, c = accs[half], cs[half]
            for kb in kbs:
                pv, rs = block(half, kb, c, masked)
                acc, c = acc + pv, c + rs
            new_accs.append(acc)
            new_cs.append(c)
        return tuple(new_accs), tuple(new_cs)

    acc0 = jnp.zeros((tq, LANES), F32)
    c0 = jnp.zeros((tq, 1), F32)
    carry = ((acc0, acc0), (c0, c0))
    carry = lax.fori_loop(0, nkb - n_full,
                          lambda j, cr: run([nkb - 1 - j], cr, True), carry)
    def live(cr):
        return jnp.minimum(jnp.min(cr[1][0]), jnp.min(cr[1][1])) < EXP2_ZERO

    def steps(n_steps, kbs_of, carry):
        def cond(state):
            j, _, more = state
            return (j < n_steps) & more

        def body(state):
            j, cr, _ = state
            cr = run(kbs_of(j), cr, False)
            return j + 1, cr, live(cr)

        return lax.while_loop(cond, body, (0, carry, live(carry)))[1]

    grp = SB_BLOCKS_PER_STEP
    carry = steps(n_full // grp, lambda j: [n_full - 1 - grp * j - u for u in range(grp)], carry)
    rem = n_full % grp
    carry = steps(rem, lambda j: [rem - 1 - j], carry)
    accs, _ = carry
    o_ref[0] = jnp.where(lo_half, accs[0], accs[1]).astype(BF16)


def _sb_attention(q, q_cb, k, k_cb, v, v_cb, n_pairs, q_pos0, tq, tk):
    b, t = q.shape[:2]
    l = k.shape[1]
    return pl.pallas_call(
        functools.partial(_sb_body, tq=tq, tk=tk, q_pos0=q_pos0, n_kb_total=l // tk),
        grid=(b, n_pairs, t // tq),
        in_specs=[pl.BlockSpec((1, tq, LANES), lambda bi, p, i: (bi, i, q_cb + p)),
                  pl.BlockSpec((1, l, LANES), lambda bi, p, i: (bi, 0, k_cb + p)),
                  pl.BlockSpec((1, l, LANES), lambda bi, p, i: (bi, 0, v_cb + p))],
        out_specs=pl.BlockSpec((1, tq, LANES), lambda bi, p, i: (bi, i, p)),
        out_shape=jax.ShapeDtypeStruct((b, t, n_pairs * LANES), BF16),
        compiler_params=_params(("parallel", "parallel", "parallel")),
        name="sb_attention",
    )(q, k, v)


def _softmax_step(s, v, m, l, acc):
    m_new = jnp.maximum(m, jnp.max(s, axis=1, keepdims=True))
    alpha = jnp.exp2(m - m_new)
    p = jnp.exp2(s - m_new)
    l_new = alpha * l + jnp.sum(p, axis=1, keepdims=True)
    acc_new = alpha * acc + _dot(p.astype(BF16), v)
    return m_new, l_new, acc_new


def _fox_body(q_ref, k_ref, v_ref, fk_ref, thr_ref, o_ref, q_s,
              *, tq, tk, q_pos0, n_kb_total, npp):
    row0 = q_pos0 + pl.program_id(2) * tq
    lane = lax.broadcasted_iota(I32, (1, LANES), 1)
    lo_half = lane < HEAD_DIM
    for pp in range(npp):
        q = q_ref[0, :, pp * LANES:(pp + 1) * LANES]
        zero = jnp.zeros_like(q)
        q_s[2 * pp] = jnp.where(lo_half, q, zero)
        q_s[2 * pp + 1] = jnp.where(lo_half, zero, q)
    qpos = row0 + lax.broadcasted_iota(I32, (tq, 1), 0)
    nkb = jnp.minimum((row0 + tq + tk - 1) // tk, n_kb_total)
    n_full = jnp.minimum((row0 + 1) // tk, nkb)

    def body(j, carry, masked, width):
        ks = pl.multiple_of(j * tk, tk)
        out = []
        for h in range(2 * npp):
            pp, half = divmod(h, 2)
            k = k_ref[0, pl.ds(ks, width), pp * LANES:(pp + 1) * LANES]
            v = v_ref[0, pl.ds(ks, width), pp * LANES:(pp + 1) * LANES]
            m, l, acc = carry[h]
            fk = fk_ref[0, pp, half:half + 1, pl.ds(ks, width)] * LOG2E
            s = _dot_nt(q_s[h], k) - fk
            if masked:
                vis = (ks + lax.broadcasted_iota(I32, (1, width), 1)) <= qpos
                s = jnp.where(vis, s, NEG_BIG)
            out.append(_softmax_step(s, v, m, l, acc))
        return tuple(out)

    win = max(tq, LANES)
    l_keys = n_kb_total * tk
    pos = lax.broadcasted_iota(I32, (1, l_keys), 1).astype(F32)
    first_needed = []
    for h in range(2 * npp):
        pp, half = divmod(h, 2)
        f_tile = fk_ref[0, pp, half:half + 1, pl.ds(pl.multiple_of(row0, LANES), win)] * LOG2E
        f_all = fk_ref[0, pp, half:half + 1, :] * LOG2E
        gone = (f_all - jnp.max(f_tile, axis=1, keepdims=True)) > thr_ref[0, pp, half:half + 1, 0:1]
        first_needed.append(jnp.min(jnp.where(gone, float(l_keys), pos)))
    j0 = jnp.minimum(functools.reduce(jnp.minimum, first_needed).astype(I32) // tk, n_full)

    init = (jnp.full((tq, 1), M_INIT, F32), jnp.zeros((tq, 1), F32), jnp.zeros((tq, LANES), F32))
    n_wide = (n_full - j0) // 2
    carry = lax.fori_loop(0, n_wide,
                          lambda j, c: body(j0 + 2 * j, c, masked=False, width=2 * tk),
                          (init,) * (2 * npp))
    carry = lax.fori_loop(j0 + 2 * n_wide, n_full,
                          functools.partial(body, masked=False, width=tk), carry)
    carry = lax.fori_loop(n_full, nkb, functools.partial(body, masked=True, width=tk), carry)
    for pp in range(npp):
        (_, l0, a0), (_, l1, a1) = carry[2 * pp], carry[2 * pp + 1]
        o_ref[0, :, pp * LANES:(pp + 1) * LANES] = jnp.where(lo_half, a0 / l0, a1 / l1).astype(BF16)


def _fox_attention(q, q_cb, k, k_cb, v, v_cb, fk, thr, q_pos0, tq, tk):
    b, t = q.shape[:2]
    l = k.shape[1]
    n_pairs = H_FOX // 2
    npp = FOX_PAIRS
    w = npp * LANES
    assert q_cb % npp == 0 and k_cb % npp == 0 and v_cb % npp == 0
    assert q_pos0 % LANES == 0 and (tq % LANES == 0 or t == tq)
    return pl.pallas_call(
        functools.partial(_fox_body, tq=tq, tk=tk, q_pos0=q_pos0, n_kb_total=l // tk, npp=npp),
        grid=(b, n_pairs // npp, t // tq),
        in_specs=[pl.BlockSpec((1, tq, w), lambda bi, p, i: (bi, i, q_cb // npp + p)),
                  pl.BlockSpec((1, l, w), lambda bi, p, i: (bi, 0, k_cb // npp + p)),
                  pl.BlockSpec((1, l, w), lambda bi, p, i: (bi, 0, v_cb // npp + p)),
                  pl.BlockSpec((1, npp, 2, l), lambda bi, p, i: (bi, p, 0, 0)),
                  pl.BlockSpec((1, npp, 2, LANES), lambda bi, p, i: (bi, p, 0, 0))],
        out_specs=pl.BlockSpec((1, tq, w), lambda bi, p, i: (bi, i, p)),
        out_shape=jax.ShapeDtypeStruct((b, t, n_pairs * LANES), BF16),
        scratch_shapes=[pltpu.VMEM((2 * npp, tq, LANES), BF16)],
        compiler_params=_params(("parallel", "parallel", "parallel")),
        name="fox_attention",
    )(q, k, v, fk, thr)


def _forget_body(raw_ref, b_ref, logf_ref, cum_ref, carry_ref, *, tb, n_given):
    j = pl.program_id(1)

    @pl.when(j == 0)
    def _():
        carry_ref[...] = jnp.zeros_like(carry_ref)

    raw = raw_ref[0]
    z = raw + b_ref[...]
    computed = _log_sigmoid_neg(-z)
    row = j * tb + lax.broadcasted_iota(I32, (tb, 1), 0)
    logf = jnp.where(row < n_given, raw, computed)
    incl = (lax.broadcasted_iota(I32, (tb, tb), 1) <=
            lax.broadcasted_iota(I32, (tb, tb), 0)).astype(BF16)
    hi, mid, lo = _split3(logf)
    cum = _dot(incl, hi) + _dot(incl, mid) + _dot(incl, lo) + carry_ref[0:1, :]
    logf_ref[0] = logf
    cum_ref[0] = cum
    carry_ref[0:1, :] = cum[tb - 1:tb, :]


def _forget_cumsum(raw, bias, n_given, tb):
    raw = jnp.pad(raw, ((0, 0), (0, 0), (0, LANES - raw.shape[2])))
    bias = jnp.pad(bias, (0, LANES - bias.shape[0]))
    b, l, h = raw.shape
    return pl.pallas_call(
        functools.partial(_forget_body, tb=tb, n_given=n_given),
        grid=(b, l // tb),
        in_specs=[pl.BlockSpec((1, tb, h), lambda bi, j: (bi, j, 0)),
                  pl.BlockSpec((1, h), lambda bi, j: (0, 0))],
        out_specs=[pl.BlockSpec((1, tb, h), lambda bi, j: (bi, j, 0)),
                   pl.BlockSpec((1, tb, h), lambda bi, j: (bi, j, 0))],
        out_shape=[jax.ShapeDtypeStruct((b, l, h), F32), jax.ShapeDtypeStruct((b, l, h), F32)],
        scratch_shapes=[pltpu.VMEM((8, h), F32)],
        compiler_params=_params(("parallel", "arbitrary")),
        name="forget_cumsum",
    )(raw, bias.reshape(1, h))


def _dsa_body(q_ref, qi_ref, wi_ref, k_ref, v_ref, ki_ref, o_ref,
              key_s, bias_s, qi_s, q_s, *, tq, tk, q_pos0, l_valid, n_kb_total, n_sel):
    row0 = q_pos0 + pl.program_id(1) * tq
    lane = lax.broadcasted_iota(I32, (1, LANES), 1)
    lo_half = lane < HEAD_DIM
    qpos = row0 + lax.broadcasted_iota(I32, (tq, 1), 0)
    qchunk = qpos // CHUNK
    kend = jnp.minimum(((row0 + tq - 1) // CHUNK + 1) * CHUNK, l_valid)
    nkb = jnp.minimum(jnp.maximum((kend + tk - 1) // tk, (n_sel + tk - 1) // tk), n_kb_total)

    n_stack = IDX_HEADS // 2
    for p in range(n_stack):
        blk = qi_ref[0, :, p * LANES:(p + 1) * LANES]
        zero = jnp.zeros_like(blk)
        qi_s[0, p * tq:(p + 1) * tq, :] = jnp.where(lo_half, blk, zero)
        qi_s[1, p * tq:(p + 1) * tq, :] = jnp.where(lo_half, zero, blk)
    wsc = wi_ref[0] * (IDX_DIM ** -0.5 * IDX_HEADS ** -0.5)

    def wide_then_single(step, init):
        carry = lax.fori_loop(0, nkb // 2, lambda j, c: step(2 * j, c, 2 * tk), init)
        return lax.fori_loop(nkb // 2 * 2, nkb, lambda j, c: step(j, c, tk), carry)

    def score_body(j, _, width):
        for u in range(width // SCORE_SUB):
            ks = pl.multiple_of(j * tk + u * SCORE_SUB, SCORE_SUB)
            ki = ki_ref[0, pl.ds(ks, SCORE_SUB), :]
            score = jnp.zeros((tq, SCORE_SUB), F32)
            for half in range(2):
                rel = jnp.maximum(_dot_nt(qi_s[half], ki), 0.0)
                for p in range(n_stack):
                    h = 2 * p + half
                    score = score + wsc[:, h:h + 1] * rel[p * tq:(p + 1) * tq]
            kpos = ks + lax.broadcasted_iota(I32, (1, SCORE_SUB), 1)
            vis = ((kpos // CHUNK) <= qchunk) & (kpos < l_valid)
            score = jnp.where(vis, score, -jnp.inf)
            bits = lax.bitcast_convert_type(score, I32)
            key = bits ^ ((bits >> 31) & INT_MAX)
            key_s[:, pl.ds(ks, SCORE_SUB)] = jnp.where(bits == INT_MIN, 0, key)
        return 0

    wide_then_single(score_body, 0)

    n_sel_f = float(n_sel)
    cr = min(COUNT_ROWS, tq)

    def select_rows(rs):
        def count(pred):
            def cbody(j, acc):
                blk = key_s[rs, pl.ds(pl.multiple_of(j * tk, tk), tk)]
                hit = jnp.where(pred(blk, j), 1.0, 0.0)
                for c in range(tk // LANES):
                    acc = acc + hit[:, c * LANES:(c + 1) * LANES]
                return acc
            acc = lax.fori_loop(0, nkb, cbody, jnp.zeros((cr, LANES), F32))
            return jnp.sum(acc, axis=1, keepdims=True)

        c_zero = count(lambda blk, j: blk >= 0)
        thr = jnp.where(c_zero >= n_sel_f, 0, INT_MIN)
        c_ge = jnp.where(c_zero >= n_sel_f, c_zero, (nkb * tk).astype(F32))

        def unsettled(c_ge):
            return jnp.max(jnp.where(c_ge == n_sel_f, 0.0, 1.0)) > 0.0

        def bit_cond(state):
            b, _, _, more = state
            return (b < 31) & more

        def bit_body(state):
            b, thr, c_ge, _ = state
            cand = thr | (1 << (30 - b))
            c = count(lambda blk, j: blk >= cand)
            keep = c >= n_sel_f
            c_ge = jnp.where(keep, c, c_ge)
            return b + 1, jnp.where(keep, cand, thr), c_ge, unsettled(c_ge)

        _, thr, c_ge, _ = lax.while_loop(bit_cond, bit_body, (0, thr, c_ge, unsettled(c_ge)))
        tied = (c_ge > n_sel_f) & (thr > KEY_NEG_INF)
        any_tied = jnp.max(jnp.where(tied, 1.0, 0.0)) > 0.0

        def tie_index_bound():
            need = n_sel_f - count(lambda blk, j: blk > thr)

            def kidx(j):
                return j * tk + lax.broadcasted_iota(I32, (1, tk), 1)

            n_bits = (n_kb_total * tk - 1).bit_length()

            def jbody(b, jb):
                cand = jb | (1 << (n_bits - 1 - b))
                cnt = count(lambda blk, j: (blk == thr) & (kidx(j) < cand))
                return jnp.where(cnt < need, cand, jb)

            jb = lax.fori_loop(0, n_bits, jbody, jnp.zeros((cr, 1), I32))
            return jnp.where(tied, jb, INT_MAX)

        jbound = lax.cond(any_tied, tie_index_bound, lambda: jnp.full((cr, 1), INT_MAX, I32))

        def bias_body(j, _):
            ks = pl.multiple_of(j * tk, tk)
            blk = key_s[rs, pl.ds(ks, tk)]
            kpos = ks + lax.broadcasted_iota(I32, (1, tk), 1)
            sel = (blk > thr) | ((blk == thr) & (kpos <= jbound))
            sel = sel & (blk > KEY_NEG_INF)
            bias_s[rs, pl.ds(ks, tk)] = jnp.where(sel, 0.0, NEG_BIG)
            return 0

        lax.fori_loop(0, nkb, bias_body, 0)

    for r in range(tq // cr):
        select_rows(slice(r * cr, (r + 1) * cr))

    n_grp = H_DSA // KV_DSA
    for p in range(n_grp):
        blk = q_ref[0, :, p * LANES:(p + 1) * LANES]
        zero = jnp.zeros_like(blk)
        q_s[0, p * tq:(p + 1) * tq, :] = jnp.where(lo_half, blk, zero)
        q_s[1, p * tq:(p + 1) * tq, :] = jnp.where(lo_half, zero, blk)

    def att_body(j, carry, width):
        ks = pl.multiple_of(j * tk, tk)
        k = k_ref[0, pl.ds(ks, width), :]
        v = v_ref[0, pl.ds(ks, width), :]
        bias = bias_s[:, pl.ds(ks, width)]
        bias = jnp.concatenate([bias] * n_grp, axis=0)
        return tuple(_softmax_step(_dot_nt(q_s[half], k) + bias, v, *carry[half])
                     for half in range(2))

    rows = n_grp * tq
    init = (jnp.full((rows, 1), M_INIT, F32), jnp.zeros((rows, 1), F32),
            jnp.zeros((rows, LANES), F32))
    (_, l0, a0), (_, l1, a1) = wide_then_single(att_body, (init, init))
    o0, o1 = a0 / l0, a1 / l1
    for p in range(n_grp):
        o_ref[0, :, p * LANES:(p + 1) * LANES] = jnp.where(
            lo_half, o0[p * tq:(p + 1) * tq], o1[p * tq:(p + 1) * tq]).astype(BF16)


def _dsa_attention(q, qi, qi_cb, wi, wi_cb, k, k_cb, v, v_cb, ki, ki_cb,
                   q_pos0, l_valid, n_sel, tq, tk):
    b, t = q.shape[:2]
    l = k.shape[1]
    width = H_DSA * HEAD_DIM
    return pl.pallas_call(
        functools.partial(_dsa_body, tq=tq, tk=tk, q_pos0=q_pos0, l_valid=l_valid,
                          n_kb_total=l // tk, n_sel=n_sel),
        grid=(b, t // tq),
        in_specs=[pl.BlockSpec((1, tq, width), lambda bi, i: (bi, i, 0)),
                  pl.BlockSpec((1, tq, width), lambda bi, i: (bi, i, qi_cb)),
                  pl.BlockSpec((1, tq, LANES), lambda bi, i: (bi, i, wi_cb)),
                  pl.BlockSpec((1, l, LANES), lambda bi, i: (bi, 0, k_cb)),
                  pl.BlockSpec((1, l, LANES), lambda bi, i: (bi, 0, v_cb)),
                  pl.BlockSpec((1, l, LANES), lambda bi, i: (bi, 0, ki_cb))],
        out_specs=pl.BlockSpec((1, tq, width), lambda bi, i: (bi, i, 0)),
        out_shape=jax.ShapeDtypeStruct((b, t, width), BF16),
        scratch_shapes=[pltpu.VMEM((tq, l), I32), pltpu.VMEM((tq, l), F32),
                        pltpu.VMEM((2, IDX_HEADS // 2 * tq, LANES), BF16),
                        pltpu.VMEM((2, H_DSA // KV_DSA * tq, LANES), BF16)],
        compiler_params=_params(("parallel", "parallel")),
        name="dsa_attention",
    )(q, qi, wi, k, v, ki)


def _moe_body(h_ref, g_ref, wr_ref, w1_ref, w3_ref, w2_ref, o_ref,
              xn_s, xg_s, ye_s, rank_s, gate_s, rank_t_s, *, tm, ch, n_fc):
    e = pl.program_id(1)
    fc = pl.program_id(2)
    lane = lax.broadcasted_iota(I32, (1, LANES), 1)

    @pl.when((e == 0) & (fc == 0))
    def _route():
        x = h_ref[...]
        xn = _rms(x, g_ref[...])
        xn_s[...] = xn.astype(BF16)
        o_ref[...] = x
        x3 = _split3(xn)
        logits = jnp.zeros((tm, LANES), F32)
        for a, b in ((2, 0), (0, 2), (1, 1), (1, 0), (0, 1), (0, 0)):
            logits = logits + _dot(x3[a], wr_ref[b])
        lane_f = lane.astype(F32)
        logits = jnp.where(lane < N_EXPERTS, logits, -jnp.inf)
        m1 = jnp.max(logits, axis=1, keepdims=True)
        i1 = jnp.min(jnp.where(logits == m1, lane_f, float(LANES)), axis=1, keepdims=True)
        rest = jnp.where(lane_f == i1, -jnp.inf, logits)
        m2 = jnp.max(rest, axis=1, keepdims=True)
        i2 = jnp.min(jnp.where(rest == m2, lane_f, float(LANES)), axis=1, keepdims=True)
        e2 = jnp.exp(m2 - m1)
        g1 = 1.0 / (1.0 + e2)
        g2 = e2 / (1.0 + e2)
        sel1 = lane_f == i1
        sel2 = lane_f == i2
        gate_s[...] = jnp.where(sel1, g1, 0.0) + jnp.where(sel2, g2, 0.0)
        sel = jnp.where(sel1 | sel2, 1.0, 0.0)
        incl = (lax.broadcasted_iota(I32, (tm, tm), 1) <=
                lax.broadcasted_iota(I32, (tm, tm), 0)).astype(BF16)
        rank = _dot(incl, sel.astype(BF16)) * sel
        rank_s[...] = rank
        rank_t_s[...] = rank.T

    rank_row = rank_t_s[pl.ds(e, 1), :]
    cnt = jnp.max(rank_row).astype(I32)
    nch = (cnt + ch - 1) // ch

    @pl.when(fc == 0)
    def _gather():
        def gbody(c, _):
            base = pl.multiple_of(c * ch, ch)
            slot = (base + 1 + lax.broadcasted_iota(I32, (ch, 1), 0)).astype(F32)
            pick = jnp.where(rank_row == slot, 1.0, 0.0).astype(BF16)
            xg_s[pl.ds(base, ch), :] = _dot(pick, xn_s[...]).astype(BF16)
            return 0
        lax.fori_loop(0, nch, gbody, 0)

    def fbody(c, _):
        base = pl.multiple_of(c * ch, ch)
        xg = xg_s[pl.ds(base, ch), :]
        a = _dot(xg, w1_ref[0])
        b = _dot(xg, w3_ref[0])
        part = _dot((a * _sigmoid(a) * b).astype(BF16), w2_ref[0])

        @pl.when(fc == 0)
        def _():
            ye_s[pl.ds(base, ch), :] = part

        @pl.when(fc != 0)
        def _():
            ye_s[pl.ds(base, ch), :] += part
        return 0

    lax.fori_loop(0, nch, fbody, 0)

    @pl.when(fc == n_fc - 1)
    def _scatter():
        here = lane == e
        rank_col = jnp.sum(jnp.where(here, rank_s[...], 0.0), axis=1, keepdims=True)
        gate_col = jnp.sum(jnp.where(here, gate_s[...], 0.0), axis=1, keepdims=True)

        def sbody(c, _):
            base = pl.multiple_of(c * ch, ch)
            slot = (base + 1 + lax.broadcasted_iota(I32, (1, ch), 1)).astype(F32)
            place = jnp.where(rank_col == slot, 1.0, 0.0).astype(BF16)
            ye = ye_s[pl.ds(base, ch), :].astype(BF16)
            o_ref[...] += gate_col * _dot(place, ye)
            return 0
        lax.fori_loop(0, nch, sbody, 0)


def _moe(h, g, wr3, w1, w3, w2, tm, ch, tf):
    n, d = h.shape
    ne, _, f = w1.shape
    n_fc = f // tf
    return pl.pallas_call(
        functools.partial(_moe_body, tm=tm, ch=ch, n_fc=n_fc),
        grid=(n // tm, ne, n_fc),
        in_specs=[pl.BlockSpec((tm, d), lambda i, e, c: (i, 0), pipeline_mode=pl.Buffered(1)),
                  pl.BlockSpec((1, d), lambda i, e, c: (0, 0)),
                  pl.BlockSpec((3, d, LANES), lambda i, e, c: (0, 0, 0)),
                  pl.BlockSpec((1, d, tf), lambda i, e, c: (e, 0, c)),
                  pl.BlockSpec((1, d, tf), lambda i, e, c: (e, 0, c)),
                  pl.BlockSpec((1, tf, d), lambda i, e, c: (e, c, 0))],
        out_specs=pl.BlockSpec((tm, d), lambda i, e, c: (i, 0)),
        out_shape=jax.ShapeDtypeStruct((n, d), F32),
        scratch_shapes=[pltpu.VMEM((tm, d), BF16), pltpu.VMEM((_round_up(tm, ch), d), BF16),
                        pltpu.VMEM((_round_up(tm, ch), d), F32), pltpu.VMEM((tm, LANES), F32),
                        pltpu.VMEM((tm, LANES), F32), pltpu.VMEM((LANES, tm), F32)],
        compiler_params=_params(("parallel", "arbitrary", "arbitrary")),
        name="moe",
    )(h, g.reshape(1, d), wr3, w1, w3, w2)


def _round_up(x, m):
    return (x + m - 1) // m * m


def _pad_rows(a, l_pad):
    return jnp.pad(a, ((0, 0), (0, l_pad - a.shape[1]), (0, 0)))


def _prep_weights(g_q_dsa, g_k_dsa, g_q_fox, g_k_fox, w_in_even, w_out_even, w_in_odd, w_router):
    d = w_in_even.shape[0]
    perm = jnp.asarray(DSA_HEAD_PERM)
    qa, ka, va, qb, kb, vb, qi, ki, wi = jnp.split(
        w_in_even, [512, 1024, 1536, 2048, 2176, 2304, 2816, 2880], axis=1)
    qb = qb.reshape(d, H_DSA, HEAD_DIM)[:, perm].reshape(d, H_DSA * HEAD_DIM)
    wi = jnp.pad(wi, ((0, 0), (0, LANES - IDX_HEADS)))
    w_even = jnp.concatenate([qb, kb, vb, ki, ki, wi, qi, qa, ka, va], axis=1).astype(BF16)
    ones = lambda k: jnp.ones((k,), F32)
    q_scale = lambda k: jnp.full((k,), Q_SCALE, F32)
    scale_even = jnp.concatenate([ones(E_QA), q_scale(E_KA - E_QA), ones(E_END - E_KA)])
    scale_odd = ones(O_END)
    scale_norm_even = jnp.concatenate([q_scale(E_KB - E_QB), ones(E_VB - E_KB)])
    scale_norm_odd = jnp.concatenate([q_scale(O_K - O_Q), ones(O_V - O_K)])
    w_out_sb = w_out_even[:H_SB * HEAD_DIM].astype(BF16)
    w_out_dsa = w_out_even[H_SB * HEAD_DIM:].reshape(H_DSA, HEAD_DIM, d)[perm]
    w_out_dsa = w_out_dsa.reshape(H_DSA * HEAD_DIM, d).astype(BF16)
    w_odd = jnp.pad(w_in_odd, ((0, 0), (0, O_END - w_in_odd.shape[1]))).astype(BF16)
    gains_even = jnp.concatenate([jnp.tile(g_q_dsa, H_DSA), jnp.tile(g_k_dsa, KV_DSA)])
    gains_odd = jnp.concatenate([jnp.tile(g_q_fox, H_FOX), jnp.tile(g_k_fox, H_FOX)])
    wr = jnp.pad(w_router, ((0, 0), (0, LANES - N_EXPERTS)))
    wr_hi = wr.astype(BF16)
    wr_r1 = wr - wr_hi.astype(F32)
    wr_mid = wr_r1.astype(BF16)
    wr_lo = (wr_r1 - wr_mid.astype(F32)).astype(BF16)
    return dict(w_even=w_even, w_out_sb=w_out_sb, w_out_dsa=w_out_dsa, w_odd=w_odd,
                gains_even=gains_even, gains_odd=gains_odd,
                scale_even=scale_even, scale_odd=scale_odd,
                scale_norm_even=scale_norm_even, scale_norm_odd=scale_norm_odd,
                wr3=jnp.stack([wr_hi, wr_mid, wr_lo]))


def _rotary_tables(pos):
    half = ROT_DIM // 2
    inv = ROPE_THETA ** (-jnp.arange(half, dtype=F32) / half)
    ang = pos.astype(F32)[:, None] * inv[None, :]
    cos, sin = jnp.cos(ang), jnp.sin(ang)
    t = pos.shape[0]
    pad = HEAD_DIM - ROT_DIM
    cos_h = jnp.concatenate([cos, cos, jnp.ones((t, pad), F32)], axis=1)
    sin_h = jnp.concatenate([-sin, sin, jnp.zeros((t, pad), F32)], axis=1)
    return jnp.tile(cos_h, (1, 2)), jnp.tile(sin_h, (1, 2))


def _trunk(x, p, pos0, past_even, past_odd, w, wb):
    b, t, d = x.shape
    n = b * t
    tm = min(ROW_TILE, n)
    has_past = past_even is not None
    past_len = past_even[0].shape[1] if has_past else 0
    l_valid = past_len + t
    l_pad = _round_up(l_valid, KEY_PAD)
    n_sel = min(DSA_TOPK, l_valid // 4)
    h = x.reshape(n, d)

    def keys(cache, new):
        if not has_past:
            return new
        full = jnp.concatenate([cache.reshape(b, past_len, -1).astype(BF16), new], axis=1)
        return _pad_rows(full, l_pad)

    pf, pb = _proj(h, w["g_mix"][0], wb["w_even"], wb["scale_even"], tm)
    cos, sin = _rotary_tables(pos0 + jnp.arange(t, dtype=I32))
    if t % tm:
        cos, sin = jnp.tile(cos, (tm // t, 1)), jnp.tile(sin, (tm // t, 1))
    qkf, qkb = _headnorm(pf, E_QB, wb["gains_even"], wb["scale_norm_even"], cos, sin, tm)
    pf3, pb3 = pf.reshape(b, t, E_END), pb.reshape(b, t, E_END)
    qkf3, qkb3 = qkf.reshape(b, t, E_NORM), qkb.reshape(b, t, E_NORM)
    cols = lambda a, c0, width: a[:, :, c0:c0 + width]
    sb_w, kv_w = H_SB * HEAD_DIM, KV_DSA * HEAD_DIM
    if has_past:
        c_sbk, c_sbv, c_dk, c_dv, c_ki = past_even
        ka, ka_cb = keys(c_sbk, cols(pb3, E_KA, sb_w)), 0
        va, va_cb = keys(c_sbv, cols(pb3, E_VA, sb_w)), 0
        kb, kb_cb = keys(c_dk, cols(qkb3, E_KB, kv_w)), 0
        vb, vb_cb = keys(c_dv, cols(pb3, E_VB, kv_w)), 0
        ki2 = jnp.concatenate([c_ki, c_ki], axis=-1)
        ki, ki_cb = keys(ki2, cols(pb3, E_KI, LANES)), 0
    else:
        ka, ka_cb = pb3, E_KA // LANES
        va, va_cb = pb3, E_VA // LANES
        kb, kb_cb = qkb3, E_KB // LANES
        vb, vb_cb = pb3, E_VB // LANES
        ki, ki_cb = pb3, E_KI // LANES
    oa = _sb_attention(pb3, E_QA // LANES, ka, ka_cb, va, va_cb, H_SB // 2, pos0,
                       min(SB_Q_TILE, t), SB_K_TILE)
    ob = _dsa_attention(qkb3, pb3, E_QI // 512, pf3, E_WI // LANES, kb, kb_cb, vb, vb_cb,
                        ki, ki_cb, pos0, l_valid, n_sel, min(DSA_Q_TILE, t), DSA_K_TILE)
    h = _residual_matmul(h, [oa.reshape(n, -1), ob.reshape(n, -1)],
                         [wb["w_out_sb"], wb["w_out_dsa"]], tm)
    h = _ffn(h, w["g_ffn"][0], wb["w_ff1"], wb["w_ff3"], wb["w_ff2"], tm)
    h = _ple(h, w["g_ple"][0], wb["w_ple_gate"][0], p[0].reshape(n, -1), wb["w_ple_in"][0], tm)
    even_state = (cols(pf3, E_KA, sb_w).reshape(b, t, H_SB, HEAD_DIM),
                  cols(pf3, E_VA, sb_w).reshape(b, t, H_SB, HEAD_DIM),
                  cols(qkf3, E_KB, kv_w).reshape(b, t, KV_DSA, HEAD_DIM),
                  cols(pf3, E_VB, kv_w).reshape(b, t, KV_DSA, HEAD_DIM),
                  cols(pf3, E_KI, IDX_DIM))

    fox_w = H_FOX * HEAD_DIM
    pf, pb = _proj(h, w["g_mix"][1], wb["w_odd"], wb["scale_odd"], tm)
    qkf, qkb = _headnorm(pf, O_Q, wb["gains_odd"], wb["scale_norm_odd"], None, None, tm)
    pf3, pb3 = pf.reshape(b, t, O_END), pb.reshape(b, t, O_END)
    qkf3, qkb3 = qkf.reshape(b, t, O_V), qkb.reshape(b, t, O_V)
    gate_pre = cols(pf3, O_F, H_FOX)
    if has_past:
        c_fk, c_fv, c_lf = past_odd
        kf, kf_cb = keys(c_fk, cols(qkb3, O_K, fox_w)), 0
        vf, vf_cb = keys(c_fv, cols(pb3, O_V, fox_w)), 0
        raw = _pad_rows(jnp.concatenate([c_lf, gate_pre], axis=1), l_pad)
    else:
        kf, kf_cb = qkb3, O_K // LANES
        vf, vf_cb = pb3, O_V // LANES
        raw = gate_pre
    logf, cum = _forget_cumsum(raw, w["b_forget"][0], past_len, min(KEY_PAD, l_pad))
    logf, cum = logf[:, :, :H_FOX], cum[:, :, :H_FOX]
    fk = cum.reshape(b, l_pad, H_FOX // 2, 2).transpose(0, 2, 3, 1)

    normed_max = lambda g: HEAD_DIM ** 0.5 * jnp.max(jnp.abs(g))
    q_max = jnp.full((b, H_FOX), Q_SCALE * normed_max(w["g_q_fox"]), F32)
    if has_past:
        kx = kf.astype(F32).reshape(b, l_pad, H_FOX, HEAD_DIM)
        k_max = jnp.sqrt(jnp.max(jnp.sum(kx * kx, axis=-1), axis=1))
    else:
        k_max = jnp.full((b, H_FOX), normed_max(w["g_k_fox"]), F32)
    decay_cut = EXP2_ZERO + 2.0 * 1.02 * q_max * k_max
    thr = jnp.broadcast_to(decay_cut.reshape(b, H_FOX // 2, 2, 1), (b, H_FOX // 2, 2, LANES))
    of = _fox_attention(qkb3, O_Q // LANES, kf, kf_cb, vf, vf_cb, fk, thr, pos0,
                        min(FOX_Q_TILE, t), FOX_K_TILE)
    h = _residual_matmul(h, [of.reshape(n, -1)], [wb["w_out_odd"]], tm)
    h = _moe(h, w["g_ffn"][1], wb["wr3"], wb["w_exp1"], wb["w_exp3"], wb["w_exp2"],
             min(MOE_TILE, n), min(MOE_CHUNK, n), MOE_F_TILE)
    h = _ple(h, w["g_ple"][1], wb["w_ple_gate"][1], p[1].reshape(n, -1), wb["w_ple_in"][1], tm)
    odd_state = (cols(qkf3, O_K, fox_w).reshape(b, t, H_FOX, HEAD_DIM),
                 cols(pf3, O_V, fox_w).reshape(b, t, H_FOX, HEAD_DIM),
                 logf[:, past_len:l_valid])
    return h.reshape(b, t, d), even_state, odd_state


def kernel(x_prompt, x_sample, p_prompt, p_sample, cache_sb_k, cache_sb_v, cache_dsa_k, cache_dsa_v, cache_dsa_kidx, cache_fox_k, cache_fox_v, cache_fox_logf, g_mix, g_ffn, g_ple, w_in_even, g_q_dsa, g_k_dsa, w_out_even, w_ff1, w_ff3, w_ff2, w_in_odd, b_forget, g_q_fox, g_k_fox, w_out_odd, w_router, w_exp1, w_exp3, w_exp2, w_ple_in, w_ple_gate):
    assert g_mix.shape[0] == 2, "two layers: one even (stick-breaking + DSA), one odd (FoX + experts)"
    past_len = cache_sb_k.shape[2]
    w = dict(g_mix=g_mix, g_ffn=g_ffn, g_ple=g_ple, b_forget=b_forget,
             g_q_fox=g_q_fox[0], g_k_fox=g_k_fox[0])
    wb = _prep_weights(g_q_dsa[0], g_k_dsa[0], g_q_fox[0], g_k_fox[0],
                       w_in_even[0], w_out_even[0], w_in_odd[0], w_router[0])
    wb.update(w_ff1=w_ff1[0].astype(BF16), w_ff3=w_ff3[0].astype(BF16), w_ff2=w_ff2[0].astype(BF16),
              w_out_odd=w_out_odd[0].astype(BF16),
              w_exp1=w_exp1[0].astype(BF16), w_exp3=w_exp3[0].astype(BF16),
              w_exp2=w_exp2[0].astype(BF16),
              w_ple_in=w_ple_in.astype(BF16), w_ple_gate=w_ple_gate.astype(BF16))

    y_p, even_p, odd_p = _trunk(x_prompt, p_prompt, 0, None, None, w, wb)
    y_s, even_s, odd_s = _trunk(
        x_sample, p_sample, past_len,
        (cache_sb_k[0], cache_sb_v[0], cache_dsa_k[0], cache_dsa_v[0], cache_dsa_kidx[0]),
        (cache_fox_k[0], cache_fox_v[0], cache_fox_logf[0]), w, wb)
    tail = lambda a: a[:, -past_len:][None]
    whole = lambda a: a[None]
    return (y_p, y_s,
            *(tail(a) for a in even_p), *(tail(a) for a in odd_p),
            *(whole(a) for a in even_s), *(whole(a) for a in odd_s))
```

```python
import functools

import jax
import jax.numpy as jnp
from jax import lax
from jax.experimental import pallas as pl
from jax.experimental.pallas import tpu as pltpu

F32 = jnp.float32
BF16 = jnp.bfloat16
I32 = jnp.int32

EPS = 1e-6
HEAD_DIM = 64
CHUNK = 64
H_SB = 8
H_DSA = 8
KV_DSA = 2
IDX_HEADS = 8
IDX_DIM = 64
H_FOX = 16
DSA_TOPK = 256
ROT_DIM = HEAD_DIM // 4
ROPE_THETA = 500000.0
N_EXPERTS = 8
ATT_SCALE = HEAD_DIM ** -0.5
LOG2E = 1.4426950408889634
Q_SCALE = ATT_SCALE * LOG2E

LANES = 128
MIB = 1024 * 1024
VMEM_LIMIT = 56 * MIB

ROW_TILE = 512
SB_Q_TILE, SB_K_TILE = 256, 256
SB_BLOCKS_PER_STEP = 1
EXP2_ZERO = 150.0
FOX_Q_TILE, FOX_K_TILE = 256, 512
FOX_PAIRS = 1
DSA_Q_TILE, DSA_K_TILE = 256, 512
COUNT_ROWS = 128
SCORE_SUB = 256
KEY_PAD = 512
MOE_TILE = 1024
MOE_CHUNK = 304
MOE_F_TILE = 1792

NEG_BIG = -1e30
M_INIT = -1e29
INT_MIN = -(2 ** 31)
INT_MAX = 2 ** 31 - 1
KEY_NEG_INF = -2139095041

E_QB, E_KB, E_VB, E_KI, E_WI, E_QI, E_QA, E_KA, E_VA, E_END = (
    0, 512, 640, 768, 896, 1024, 1536, 2048, 2560, 3072)
E_NORM = E_VB - E_QB
O_Q, O_K, O_V, O_F, O_END = 0, 1024, 2048, 3072, 3200
DSA_HEAD_PERM = (0, 4, 1, 5, 2, 6, 3, 7)


def _params(sem, vmem=VMEM_LIMIT):
    return pltpu.CompilerParams(dimension_semantics=sem, vmem_limit_bytes=vmem)


def _dot(a, b):
    return jnp.dot(a, b, preferred_element_type=F32)


def _dot_nt(a, b):
    return lax.dot_general(a, b, (((1,), (1,)), ((), ())), preferred_element_type=F32)


def _rms(x, g):
    return x * lax.rsqrt(jnp.mean(x * x, axis=-1, keepdims=True) + EPS) * g


def _split3(x):
    hi = x.astype(BF16)
    r1 = x - hi.astype(F32)
    mid = r1.astype(BF16)
    lo = (r1 - mid.astype(F32)).astype(BF16)
    return hi, mid, lo


def _sigmoid(x):
    return 1.0 / (1.0 + jnp.exp(-x))


def _log_sigmoid_neg(z):
    return -(jnp.maximum(z, 0.0) + jnp.log(1.0 + jnp.exp(-jnp.abs(z))))


def _proj_body(x_ref, g_ref, w_ref, sc_ref, of_ref, ob_ref):
    y = _rms(x_ref[...], g_ref[...]).astype(BF16)
    r = _dot(y, w_ref[...])
    of_ref[...] = r
    ob_ref[...] = (r * sc_ref[...]).astype(BF16)


def _proj(x, g, w, scales, tm):
    n, d = x.shape
    c = w.shape[1]
    return pl.pallas_call(
        _proj_body,
        grid=(n // tm,),
        in_specs=[pl.BlockSpec((tm, d), lambda i: (i, 0)),
                  pl.BlockSpec((1, d), lambda i: (0, 0)),
                  pl.BlockSpec((d, c), lambda i: (0, 0)),
                  pl.BlockSpec((1, c), lambda i: (0, 0))],
        out_specs=[pl.BlockSpec((tm, c), lambda i: (i, 0)),
                   pl.BlockSpec((tm, c), lambda i: (i, 0))],
        out_shape=[jax.ShapeDtypeStruct((n, c), F32), jax.ShapeDtypeStruct((n, c), BF16)],
        compiler_params=_params(("parallel",)),
        name="proj",
    )(x, g.reshape(1, d), w, scales.reshape(1, c))


def _headnorm_body(x_ref, g_ref, sc_ref, cos_ref, sin_ref, s_ref, of_ref, ob_ref, *, rotary, nblk):
    s = s_ref[...]
    for j in range(nblk):
        cols = slice(j * LANES, (j + 1) * LANES)
        x = x_ref[:, cols]
        hi, mid, lo = _split3(x * x)
        ms = (_dot(hi, s) + _dot(mid, s) + _dot(lo, s)) * (1.0 / HEAD_DIM)
        y = x * lax.rsqrt(ms + EPS) * g_ref[:, cols]
        if rotary:
            lane = lax.broadcasted_iota(I32, (1, LANES), 1) % HEAD_DIM
            partner = jnp.where(lane < ROT_DIM // 2,
                                pltpu.roll(y, LANES - ROT_DIM // 2, 1),
                                pltpu.roll(y, ROT_DIM // 2, 1))
            y = y * cos_ref[...] + partner * sin_ref[...]
        of_ref[:, cols] = y
        ob_ref[:, cols] = (y * sc_ref[:, cols]).astype(BF16)


def _headnorm(x, col0, gains, scales, cos, sin, tm):
    n = x.shape[0]
    width = gains.shape[0]
    nblk = width // LANES
    assert col0 % width == 0
    rotary = cos is not None
    if not rotary:
        cos = jnp.zeros((8, LANES), F32)
        sin = cos
        tab_spec = pl.BlockSpec((8, LANES), lambda i: (0, 0))
    else:
        nt = cos.shape[0] // tm
        tab_spec = pl.BlockSpec((tm, LANES), lambda i: (i % nt, 0))
    r = lax.broadcasted_iota(I32, (LANES, LANES), 0) // HEAD_DIM
    c = lax.broadcasted_iota(I32, (LANES, LANES), 1) // HEAD_DIM
    seg = (r == c).astype(BF16)
    return pl.pallas_call(
        functools.partial(_headnorm_body, rotary=rotary, nblk=nblk),
        grid=(n // tm,),
        in_specs=[pl.BlockSpec((tm, width), lambda i: (i, col0 // width)),
                  pl.BlockSpec((1, width), lambda i: (0, 0)),
                  pl.BlockSpec((1, width), lambda i: (0, 0)),
                  tab_spec, tab_spec,
                  pl.BlockSpec((LANES, LANES), lambda i: (0, 0))],
        out_specs=[pl.BlockSpec((tm, width), lambda i: (i, 0)),
                   pl.BlockSpec((tm, width), lambda i: (i, 0))],
        out_shape=[jax.ShapeDtypeStruct((n, width), F32),
                   jax.ShapeDtypeStruct((n, width), BF16)],
        compiler_params=_params(("parallel",)),
        name="headnorm",
    )(x, gains.reshape(1, width), scales.reshape(1, width), cos, sin, seg)


def _res_body(*refs, n_in):
    h_ref, o_ref = refs[0], refs[-1]
    acc = h_ref[...]
    for t in range(n_in):
        acc = acc + _dot(refs[1 + t][...], refs[1 + n_in + t][...])
    o_ref[...] = acc


def _residual_matmul(h, acts, ws, tm):
    n, d = h.shape
    n_in = len(acts)
    in_specs = [pl.BlockSpec((tm, d), lambda i: (i, 0))]
    in_specs += [pl.BlockSpec((tm, a.shape[1]), lambda i: (i, 0)) for a in acts]
    in_specs += [pl.BlockSpec(w.shape, lambda i: (0, 0)) for w in ws]
    return pl.pallas_call(
        functools.partial(_res_body, n_in=n_in),
        grid=(n // tm,),
        in_specs=in_specs,
        out_specs=pl.BlockSpec((tm, d), lambda i: (i, 0)),
        out_shape=jax.ShapeDtypeStruct((n, d), F32),
        compiler_params=_params(("parallel",)),
        name="residual_matmul",
    )(h, *acts, *ws)


def _ple_body(h_ref, g_ref, wg_ref, p_ref, wp_ref, o_ref):
    x = h_ref[...]
    xn = _rms(x, g_ref[...]).astype(BF16)
    gate = _sigmoid(_dot(xn, wg_ref[...]))
    o_ref[...] = x + gate * _dot(p_ref[...].astype(BF16), wp_ref[...])


def _ple(h, g, wg, p, wp, tm):
    n, d = h.shape
    e = p.shape[1]
    return pl.pallas_call(
        _ple_body,
        grid=(n // tm,),
        in_specs=[pl.BlockSpec((tm, d), lambda i: (i, 0)),
                  pl.BlockSpec((1, d), lambda i: (0, 0)),
                  pl.BlockSpec((d, d), lambda i: (0, 0)),
                  pl.BlockSpec((tm, e), lambda i: (i, 0)),
                  pl.BlockSpec((e, d), lambda i: (0, 0))],
        out_specs=pl.BlockSpec((tm, d), lambda i: (i, 0)),
        out_shape=jax.ShapeDtypeStruct((n, d), F32),
        compiler_params=_params(("parallel",)),
        name="ple",
    )(h, g.reshape(1, d), wg, p, wp)


def _even_channel_body(h_ref, oa_ref, ob_ref, wa_ref, wb_ref, gf_ref, w1_ref, w3_ref, w2_ref,
                       gp_ref, wg_ref, p_ref, wp_ref, o_ref):
    x = h_ref[...] + _dot(oa_ref[...], wa_ref[...]) + _dot(ob_ref[...], wb_ref[...])
    xn = _rms(x, gf_ref[...]).astype(BF16)
    a = _dot(xn, w1_ref[...])
    b = _dot(xn, w3_ref[...])
    x = x + _dot((a * _sigmoid(a) * b).astype(BF16), w2_ref[...])
    xn = _rms(x, gp_ref[...]).astype(BF16)
    gate = _sigmoid(_dot(xn, wg_ref[...]))
    o_ref[...] = x + gate * _dot(p_ref[...].astype(BF16), wp_ref[...])


def _even_channel(h, oa, ob, wa, wb, gf, w1, w3, w2, gp, wg, p, wp, tm):
    n, d = h.shape
    once = pl.Buffered(1)
    row = lambda a: pl.BlockSpec((tm, a.shape[1]), lambda i: (i, 0))
    vec = pl.BlockSpec((1, d), lambda i: (0, 0))
    full = lambda a: pl.BlockSpec(a.shape, lambda i: (0, 0), pipeline_mode=once)
    return pl.pallas_call(
        _even_channel_body,
        grid=(n // tm,),
        in_specs=[row(h), row(oa), row(ob), full(wa), full(wb), vec, full(w1), full(w3), full(w2),
                  vec, full(wg), row(p), full(wp)],
        out_specs=row(h),
        out_shape=jax.ShapeDtypeStruct((n, d), F32),
        compiler_params=_params(("parallel",)),
        name="even_channel",
    )(h, oa, ob, wa, wb, gf.reshape(1, d), w1, w3, w2, gp.reshape(1, d), wg, p, wp)


def _sb_body(q_ref, k_ref, v_ref, o_ref, *, tq, tk, q_pos0, n_kb_total):
    row0 = q_pos0 + pl.program_id(2) * tq
    q = q_ref[0]
    lane = lax.broadcasted_iota(I32, (1, LANES), 1)
    lo_half = lane < HEAD_DIM
    zero = jnp.zeros_like(q)
    q_halves = (jnp.where(lo_half, q, zero), jnp.where(lo_half, zero, q))
    qpos = row0 + lax.broadcasted_iota(I32, (tq, 1), 0)
    nkb = jnp.minimum((row0 + tq - 1 + tk - 1) // tk, n_kb_total)
    n_full = jnp.minimum(row0 // tk, nkb)
    later = (lax.broadcasted_iota(I32, (tk, tk), 0) >
             lax.broadcasted_iota(I32, (tk, tk), 1)).astype(BF16)

    def block(half, kb, c, masked):
        ks = pl.multiple_of(kb * tk, tk)
        z = _dot_nt(q_halves[half], k_ref[0, pl.ds(ks, tk), :])
        sp = jnp.maximum(z, 0.0) + jnp.log2(1.0 + jnp.exp2(-jnp.abs(z)))
        if masked:
            vis = (ks + lax.broadcasted_iota(I32, (1, tk), 1)) < qpos
            sp = jnp.where(vis, sp, 0.0)
        between = _dot(sp.astype(BF16), later)
        w = jnp.exp2(z - sp - between - c)
        if masked:
            w = jnp.where(vis, w, 0.0)
        row_sum = between[:, 0:1] + sp[:, 0:1]
        return _dot(w.astype(BF16), v_ref[0, pl.ds(ks, tk), :]), row_sum

    def run(kbs, carry, masked):
        accs, cs = carry
        new_accs, new_cs = [], []
        for half in range(2):
            acc, c = accs[half], cs[half]
            for kb in kbs:
                pv, rs = block(half, kb, c, masked)
                acc, c = acc + pv, c + rs
            new_accs.append(acc)
            new_cs.append(c)
        return tuple(new_accs), tuple(new_cs)

    acc0 = jnp.zeros((tq, LANES), F32)
    c0 = jnp.zeros((tq, 1), F32)
    carry = ((acc0, acc0), (c0, c0))
    carry = lax.fori_loop(0, nkb - n_full,
                          lambda j, cr: run([nkb - 1 - j], cr, True), carry)
    def live(cr):
        return jnp.minimum(jnp.min(cr[1][0]), jnp.min(cr[1][1])) < EXP2_ZERO

    def steps(n_steps, kbs_of, carry):
        def cond(state):
            j, _, more = state
            return (j < n_steps) & more

        def body(state):
            j, cr, _ = state
            cr = run(kbs_of(j), cr, False)
            return j + 1, cr, live(cr)

        return lax.while_loop(cond, body, (0, carry, live(carry)))[1]

    grp = SB_BLOCKS_PER_STEP
    carry = steps(n_full // grp, lambda j: [n_full - 1 - grp * j - u for u in range(grp)], carry)
    rem = n_full % grp
    carry = steps(rem, lambda j: [rem - 1 - j], carry)
    accs, _ = carry
    o_ref[0] = jnp.where(lo_half, accs[0], accs[1]).astype(BF16)


def _sb_attention(q, q_cb, k, k_cb, v, v_cb, n_pairs, q_pos0, tq, tk):
    b, t = q.shape[:2]
    l = k.shape[1]
    return pl.pallas_call(
        functools.partial(_sb_body, tq=tq, tk=tk, q_pos0=q_pos0, n_kb_total=l // tk),
        grid=(b, n_pairs, t // tq),
        in_specs=[pl.BlockSpec((1, tq, LANES), lambda bi, p, i: (bi, i, q_cb + p)),
                  pl.BlockSpec((1, l, LANES), lambda bi, p, i: (bi, 0, k_cb + p)),
                  pl.BlockSpec((1, l, LANES), lambda bi, p, i: (bi, 0, v_cb + p))],
        out_specs=pl.BlockSpec((1, tq, LANES), lambda bi, p, i: (bi, i, p)),
        out_shape=jax.ShapeDtypeStruct((b, t, n_pairs * LANES), BF16),
        compiler_params=_params(("parallel", "parallel", "parallel")),
        name="sb_attention",
    )(q, k, v)


def _softmax_step(s, v, m, l, acc):
    m_new = jnp.maximum(m, jnp.max(s, axis=1, keepdims=True))
    alpha = jnp.exp2(m - m_new)
    p = jnp.exp2(s - m_new)
    l_new = alpha * l + jnp.sum(p, axis=1, keepdims=True)
    acc_new = alpha * acc + _dot(p.astype(BF16), v)
    return m_new, l_new, acc_new


def _fox_body(q_ref, k_ref, v_ref, fk_ref, thr_ref, o_ref, q_s,
              *, tq, tk, q_pos0, n_kb_total, npp):
    row0 = q_pos0 + pl.program_id(2) * tq
    lane = lax.broadcasted_iota(I32, (1, LANES), 1)
    lo_half = lane < HEAD_DIM
    for pp in range(npp):
        q = q_ref[0, :, pp * LANES:(pp + 1) * LANES]
        zero = jnp.zeros_like(q)
        q_s[2 * pp] = jnp.where(lo_half, q, zero)
        q_s[2 * pp + 1] = jnp.where(lo_half, zero, q)
    qpos = row0 + lax.broadcasted_iota(I32, (tq, 1), 0)
    nkb = jnp.minimum((row0 + tq + tk - 1) // tk, n_kb_total)
    n_full = jnp.minimum((row0 + 1) // tk, nkb)

    def body(j, carry, masked, width):
        ks = pl.multiple_of(j * tk, tk)
        out = []
        for h in range(2 * npp):
            pp, half = divmod(h, 2)
            k = k_ref[0, pl.ds(ks, width), pp * LANES:(pp + 1) * LANES]
            v = v_ref[0, pl.ds(ks, width), pp * LANES:(pp + 1) * LANES]
            m, l, acc = carry[h]
            fk = fk_ref[0, pp, half:half + 1, pl.ds(ks, width)] * LOG2E
            s = _dot_nt(q_s[h], k) - fk
            if masked:
                vis = (ks + lax.broadcasted_iota(I32, (1, width), 1)) <= qpos
                s = jnp.where(vis, s, NEG_BIG)
            out.append(_softmax_step(s, v, m, l, acc))
        return tuple(out)

    win = max(tq, LANES)
    l_keys = n_kb_total * tk
    pos = lax.broadcasted_iota(I32, (1, l_keys), 1).astype(F32)
    first_needed = []
    for h in range(2 * npp):
        pp, half = divmod(h, 2)
        f_tile = fk_ref[0, pp, half:half + 1, pl.ds(pl.multiple_of(row0, LANES), win)] * LOG2E
        f_all = fk_ref[0, pp, half:half + 1, :] * LOG2E
        gone = (f_all - jnp.max(f_tile, axis=1, keepdims=True)) > thr_ref[0, pp, half:half + 1, 0:1]
        first_needed.append(jnp.min(jnp.where(gone, float(l_keys), pos)))
    j0 = jnp.minimum(functools.reduce(jnp.minimum, first_needed).astype(I32) // tk, n_full)

    init = (jnp.full((tq, 1), M_INIT, F32), jnp.zeros((tq, 1), F32), jnp.zeros((tq, LANES), F32))
    n_wide = (n_full - j0) // 2
    carry = lax.fori_loop(0, n_wide,
                          lambda j, c: body(j0 + 2 * j, c, masked=False, width=2 * tk),
                          (init,) * (2 * npp))
    carry = lax.fori_loop(j0 + 2 * n_wide, n_full,
                          functools.partial(body, masked=False, width=tk), carry)
    carry = lax.fori_loop(n_full, nkb, functools.partial(body, masked=True, width=tk), carry)
    for pp in range(npp):
        (_, l0, a0), (_, l1, a1) = carry[2 * pp], carry[2 * pp + 1]
        o_ref[0, :, pp * LANES:(pp + 1) * LANES] = jnp.where(lo_half, a0 / l0, a1 / l1).astype(BF16)


def _fox_attention(q, q_cb, k, k_cb, v, v_cb, fk, thr, q_pos0, tq, tk):
    b, t = q.shape[:2]
    l = k.shape[1]
    n_pairs = H_FOX // 2
    npp = FOX_PAIRS
    w = npp * LANES
    assert q_cb % npp == 0 and k_cb % npp == 0 and v_cb % npp == 0
    assert q_pos0 % LANES == 0 and (tq % LANES == 0 or t == tq)
    return pl.pallas_call(
        functools.partial(_fox_body, tq=tq, tk=tk, q_pos0=q_pos0, n_kb_total=l // tk, npp=npp),
        grid=(b, n_pairs // npp, t // tq),
        in_specs=[pl.BlockSpec((1, tq, w), lambda bi, p, i: (bi, i, q_cb // npp + p)),
                  pl.BlockSpec((1, l, w), lambda bi, p, i: (bi, 0, k_cb // npp + p)),
                  pl.BlockSpec((1, l, w), lambda bi, p, i: (bi, 0, v_cb // npp + p)),
                  pl.BlockSpec((1, npp, 2, l), lambda bi, p, i: (bi, p, 0, 0)),
                  pl.BlockSpec((1, npp, 2, LANES), lambda bi, p, i: (bi, p, 0, 0))],
        out_specs=pl.BlockSpec((1, tq, w), lambda bi, p, i: (bi, i, p)),
        out_shape=jax.ShapeDtypeStruct((b, t, n_pairs * LANES), BF16),
        scratch_shapes=[pltpu.VMEM((2 * npp, tq, LANES), BF16)],
        compiler_params=_params(("parallel", "parallel", "parallel")),
        name="fox_attention",
    )(q, k, v, fk, thr)


def _forget_body(raw_ref, b_ref, logf_ref, cum_ref, carry_ref, *, tb, n_given):
    j = pl.program_id(1)

    @pl.when(j == 0)
    def _():
        carry_ref[...] = jnp.zeros_like(carry_ref)

    raw = raw_ref[0]
    z = raw + b_ref[...]
    computed = _log_sigmoid_neg(-z)
    row = j * tb + lax.broadcasted_iota(I32, (tb, 1), 0)
    logf = jnp.where(row < n_given, raw, computed)
    incl = (lax.broadcasted_iota(I32, (tb, tb), 1) <=
            lax.broadcasted_iota(I32, (tb, tb), 0)).astype(BF16)
    hi, mid, lo = _split3(logf)
    cum = _dot(incl, hi) + _dot(incl, mid) + _dot(incl, lo) + carry_ref[0:1, :]
    logf_ref[0] = logf
    cum_ref[0] = cum
    carry_ref[0:1, :] = cum[tb - 1:tb, :]


def _forget_cumsum(raw, bias, n_given, tb):
    raw = jnp.pad(raw, ((0, 0), (0, 0), (0, LANES - raw.shape[2])))
    bias = jnp.pad(bias, (0, LANES - bias.shape[0]))
    b, l, h = raw.shape
    return pl.pallas_call(
        functools.partial(_forget_body, tb=tb, n_given=n_given),
        grid=(b, l // tb),
        in_specs=[pl.BlockSpec((1, tb, h), lambda bi, j: (bi, j, 0)),
                  pl.BlockSpec((1, h), lambda bi, j: (0, 0))],
        out_specs=[pl.BlockSpec((1, tb, h), lambda bi, j: (bi, j, 0)),
                   pl.BlockSpec((1, tb, h), lambda bi, j: (bi, j, 0))],
        out_shape=[jax.ShapeDtypeStruct((b, l, h), F32), jax.ShapeDtypeStruct((b, l, h), F32)],
        scratch_shapes=[pltpu.VMEM((8, h), F32)],
        compiler_params=_params(("parallel", "arbitrary")),
        name="forget_cumsum",
    )(raw, bias.reshape(1, h))


def _dsa_body(q_ref, qi_ref, wi_ref, k_ref, v_ref, ki_ref, o_ref,
              key_s, bias_s, qi_s, q_s, *, tq, tk, q_pos0, l_valid, n_kb_total, n_sel):
    row0 = q_pos0 + pl.program_id(1) * tq
    lane = lax.broadcasted_iota(I32, (1, LANES), 1)
    lo_half = lane < HEAD_DIM
    qpos = row0 + lax.broadcasted_iota(I32, (tq, 1), 0)
    qchunk = qpos // CHUNK
    kend = jnp.minimum(((row0 + tq - 1) // CHUNK + 1) * CHUNK, l_valid)
    nkb = jnp.minimum(jnp.maximum((kend + tk - 1) // tk, (n_sel + tk - 1) // tk), n_kb_total)

    n_stack = IDX_HEADS // 2
    for p in range(n_stack):
        blk = qi_ref[0, :, p * LANES:(p + 1) * LANES]
        zero = jnp.zeros_like(blk)
        qi_s[0, p * tq:(p + 1) * tq, :] = jnp.where(lo_half, blk, zero)
        qi_s[1, p * tq:(p + 1) * tq, :] = jnp.where(lo_half, zero, blk)
    wsc = wi_ref[0] * (IDX_DIM ** -0.5 * IDX_HEADS ** -0.5)

    def wide_then_single(step, init):
        carry = lax.fori_loop(0, nkb // 2, lambda j, c: step(2 * j, c, 2 * tk), init)
        return lax.fori_loop(nkb // 2 * 2, nkb, lambda j, c: step(j, c, tk), carry)

    def score_body(j, _, width):
        for u in range(width // SCORE_SUB):
            ks = pl.multiple_of(j * tk + u * SCORE_SUB, SCORE_SUB)
            ki = ki_ref[0, pl.ds(ks, SCORE_SUB), :]
            score = jnp.zeros((tq, SCORE_SUB), F32)
            for half in range(2):
                rel = jnp.maximum(_dot_nt(qi_s[half], ki), 0.0)
                for p in range(n_stack):
                    h = 2 * p + half
                    score = score + wsc[:, h:h + 1] * rel[p * tq:(p + 1) * tq]
            kpos = ks + lax.broadcasted_iota(I32, (1, SCORE_SUB), 1)
            vis = ((kpos // CHUNK) <= qchunk) & (kpos < l_valid)
            score = jnp.where(vis, score, -jnp.inf)
            bits = lax.bitcast_convert_type(score, I32)
            key = bits ^ ((bits >> 31) & INT_MAX)
            key_s[:, pl.ds(ks, SCORE_SUB)] = jnp.where(bits == INT_MIN, 0, key)
        return 0

    wide_then_single(score_body, 0)

    n_sel_f = float(n_sel)
    cr = min(COUNT_ROWS, tq)

    slabs = [slice(r * cr, (r + 1) * cr) for r in range(tq // cr)]

    def count(rs, pred):
        def cbody(j, acc):
            blk = key_s[rs, pl.ds(pl.multiple_of(j * tk, tk), tk)]
            hit = jnp.where(pred(blk, j), 1.0, 0.0)
            for c in range(tk // LANES):
                acc = acc + hit[:, c * LANES:(c + 1) * LANES]
            return acc
        acc = lax.fori_loop(0, nkb, cbody, jnp.zeros((cr, LANES), F32))
        return jnp.sum(acc, axis=1, keepdims=True)

    c_zero = [count(rs, lambda blk, j: blk >= 0) for rs in slabs]
    thrs = tuple(jnp.where(c >= n_sel_f, 0, INT_MIN) for c in c_zero)
    c_ges = tuple(jnp.where(c >= n_sel_f, c, (nkb * tk).astype(F32)) for c in c_zero)

    def unsettled(c_ges):
        open_rows = functools.reduce(jnp.maximum, [jnp.where(c == n_sel_f, 0.0, 1.0) for c in c_ges])
        return jnp.max(open_rows) > 0.0

    def bit_cond(state):
        b, _, _, more = state
        return (b < 31) & more

    def bit_body(state):
        b, thrs, c_ges, _ = state
        new_thrs, new_c_ges = [], []
        for rs, thr, c_ge in zip(slabs, thrs, c_ges):
            cand = thr | (1 << (30 - b))
            c = count(rs, lambda blk, j, cand=cand: blk >= cand)
            keep = c >= n_sel_f
            new_thrs.append(jnp.where(keep, cand, thr))
            new_c_ges.append(jnp.where(keep, c, c_ge))
        return b + 1, tuple(new_thrs), tuple(new_c_ges), unsettled(new_c_ges)

    _, thrs, c_ges, _ = lax.while_loop(bit_cond, bit_body, (0, thrs, c_ges, unsettled(c_ges)))

    for rs, thr, c_ge in zip(slabs, thrs, c_ges):
        tied = (c_ge > n_sel_f) & (thr > KEY_NEG_INF)
        any_tied = jnp.max(jnp.where(tied, 1.0, 0.0)) > 0.0

        def tie_index_bound(rs=rs, thr=thr, tied=tied):
            need = n_sel_f - count(rs, lambda blk, j: blk > thr)

            def kidx(j):
                return j * tk + lax.broadcasted_iota(I32, (1, tk), 1)

            n_bits = (n_kb_total * tk - 1).bit_length()

            def jbody(b, jb):
                cand = jb | (1 << (n_bits - 1 - b))
                cnt = count(rs, lambda blk, j: (blk == thr) & (kidx(j) < cand))
                return jnp.where(cnt < need, cand, jb)

            jb = lax.fori_loop(0, n_bits, jbody, jnp.zeros((cr, 1), I32))
            return jnp.where(tied, jb, INT_MAX)

        jbound = lax.cond(any_tied, tie_index_bound, lambda: jnp.full((cr, 1), INT_MAX, I32))

        def bias_body(j, _, rs=rs, thr=thr, jbound=jbound):
            ks = pl.multiple_of(j * tk, tk)
            blk = key_s[rs, pl.ds(ks, tk)]
            kpos = ks + lax.broadcasted_iota(I32, (1, tk), 1)
            sel = (blk > thr) | ((blk == thr) & (kpos <= jbound))
            sel = sel & (blk > KEY_NEG_INF)
            bias_s[rs, pl.ds(ks, tk)] = jnp.where(sel, 0.0, NEG_BIG)
            return 0

        lax.fori_loop(0, nkb, bias_body, 0)

    n_grp = H_DSA // KV_DSA
    for p in range(n_grp):
        blk = q_ref[0, :, p * LANES:(p + 1) * LANES]
        zero = jnp.zeros_like(blk)
        q_s[0, p * tq:(p + 1) * tq, :] = jnp.where(lo_half, blk, zero)
        q_s[1, p * tq:(p + 1) * tq, :] = jnp.where(lo_half, zero, blk)

    def att_body(j, carry, width):
        ks = pl.multiple_of(j * tk, tk)
        k = k_ref[0, pl.ds(ks, width), :]
        v = v_ref[0, pl.ds(ks, width), :]
        bias = bias_s[:, pl.ds(ks, width)]
        bias = jnp.concatenate([bias] * n_grp, axis=0)
        return tuple(_softmax_step(_dot_nt(q_s[half], k) + bias, v, *carry[half])
                     for half in range(2))

    rows = n_grp * tq
    init = (jnp.full((rows, 1), M_INIT, F32), jnp.zeros((rows, 1), F32),
            jnp.zeros((rows, LANES), F32))
    (_, l0, a0), (_, l1, a1) = wide_then_single(att_body, (init, init))
    o0, o1 = a0 / l0, a1 / l1
    for p in range(n_grp):
        o_ref[0, :, p * LANES:(p + 1) * LANES] = jnp.where(
            lo_half, o0[p * tq:(p + 1) * tq], o1[p * tq:(p + 1) * tq]).astype(BF16)


def _dsa_attention(q, qi, qi_cb, wi, wi_cb, k, k_cb, v, v_cb, ki, ki_cb,
                   q_pos0, l_valid, n_sel, tq, tk):
    b, t = q.shape[:2]
    l = k.shape[1]
    width = H_DSA * HEAD_DIM
    return pl.pallas_call(
        functools.partial(_dsa_body, tq=tq, tk=tk, q_pos0=q_pos0, l_valid=l_valid,
                          n_kb_total=l // tk, n_sel=n_sel),
        grid=(b, t // tq),
        in_specs=[pl.BlockSpec((1, tq, width), lambda bi, i: (bi, i, 0)),
                  pl.BlockSpec((1, tq, width), lambda bi, i: (bi, i, qi_cb)),
                  pl.BlockSpec((1, tq, LANES), lambda bi, i: (bi, i, wi_cb)),
                  pl.BlockSpec((1, l, LANES), lambda bi, i: (bi, 0, k_cb)),
                  pl.BlockSpec((1, l, LANES), lambda bi, i: (bi, 0, v_cb)),
                  pl.BlockSpec((1, l, LANES), lambda bi, i: (bi, 0, ki_cb))],
        out_specs=pl.BlockSpec((1, tq, width), lambda bi, i: (bi, i, 0)),
        out_shape=jax.ShapeDtypeStruct((b, t, width), BF16),
        scratch_shapes=[pltpu.VMEM((tq, l), I32), pltpu.VMEM((tq, l), F32),
                        pltpu.VMEM((2, IDX_HEADS // 2 * tq, LANES), BF16),
                        pltpu.VMEM((2, H_DSA // KV_DSA * tq, LANES), BF16)],
        compiler_params=_params(("parallel", "parallel")),
        name="dsa_attention",
    )(q, qi, wi, k, v, ki)


def _moe_body(h_ref, g_ref, wr_ref, w1_ref, w3_ref, w2_ref, o_ref,
              xn_s, xg_s, ye_s, rank_s, gate_s, rank_t_s, *, tm, ch, n_fc):
    e = pl.program_id(1)
    fc = pl.program_id(2)
    lane = lax.broadcasted_iota(I32, (1, LANES), 1)

    @pl.when((e == 0) & (fc == 0))
    def _route():
        x = h_ref[...]
        xn = _rms(x, g_ref[...])
        xn_s[...] = xn.astype(BF16)
        o_ref[...] = x
        x3 = _split3(xn)
        logits = jnp.zeros((tm, LANES), F32)
        for a, b in ((2, 0), (0, 2), (1, 1), (1, 0), (0, 1), (0, 0)):
            logits = logits + _dot(x3[a], wr_ref[b])
        lane_f = lane.astype(F32)
        logits = jnp.where(lane < N_EXPERTS, logits, -jnp.inf)
        m1 = jnp.max(logits, axis=1, keepdims=True)
        i1 = jnp.min(jnp.where(logits == m1, lane_f, float(LANES)), axis=1, keepdims=True)
        rest = jnp.where(lane_f == i1, -jnp.inf, logits)
        m2 = jnp.max(rest, axis=1, keepdims=True)
        i2 = jnp.min(jnp.where(rest == m2, lane_f, float(LANES)), axis=1, keepdims=True)
        e2 = jnp.exp(m2 - m1)
        g1 = 1.0 / (1.0 + e2)
        g2 = e2 / (1.0 + e2)
        sel1 = lane_f == i1
        sel2 = lane_f == i2
        gate_s[...] = jnp.where(sel1, g1, 0.0) + jnp.where(sel2, g2, 0.0)
        sel = jnp.where(sel1 | sel2, 1.0, 0.0)
        incl = (lax.broadcasted_iota(I32, (tm, tm), 1) <=
                lax.broadcasted_iota(I32, (tm, tm), 0)).astype(BF16)
        rank = _dot(incl, sel.astype(BF16)) * sel
        rank_s[...] = rank
        rank_t_s[...] = rank.T

    rank_row = rank_t_s[pl.ds(e, 1), :]
    cnt = jnp.max(rank_row).astype(I32)
    nch = (cnt + ch - 1) // ch

    @pl.when(fc == 0)
    def _gather():
        def gbody(c, _):
            base = pl.multiple_of(c * ch, ch)
            slot = (base + 1 + lax.broadcasted_iota(I32, (ch, 1), 0)).astype(F32)
            pick = jnp.where(rank_row == slot, 1.0, 0.0).astype(BF16)
            xg_s[pl.ds(base, ch), :] = _dot(pick, xn_s[...]).astype(BF16)
            return 0
        lax.fori_loop(0, nch, gbody, 0)

    def fbody(c, _):
        base = pl.multiple_of(c * ch, ch)
        xg = xg_s[pl.ds(base, ch), :]
        a = _dot(xg, w1_ref[0])
        b = _dot(xg, w3_ref[0])
        part = _dot((a * _sigmoid(a) * b).astype(BF16), w2_ref[0])

        @pl.when(fc == 0)
        def _():
            ye_s[pl.ds(base, ch), :] = part

        @pl.when(fc != 0)
        def _():
            ye_s[pl.ds(base, ch), :] += part
        return 0

    lax.fori_loop(0, nch, fbody, 0)

    @pl.when(fc == n_fc - 1)
    def _scatter():
        here = lane == e
        rank_col = jnp.sum(jnp.where(here, rank_s[...], 0.0), axis=1, keepdims=True)
        gate_col = jnp.sum(jnp.where(here, gate_s[...], 0.0), axis=1, keepdims=True)

        def sbody(c, _):
            base = pl.multiple_of(c * ch, ch)
            slot = (base + 1 + lax.broadcasted_iota(I32, (1, ch), 1)).astype(F32)
            place = jnp.where(rank_col == slot, 1.0, 0.0).astype(BF16)
            ye = ye_s[pl.ds(base, ch), :].astype(BF16)
            o_ref[...] += gate_col * _dot(place, ye)
            return 0
        lax.fori_loop(0, nch, sbody, 0)


def _moe(h, g, wr3, w1, w3, w2, tm, ch, tf):
    n, d = h.shape
    ne, _, f = w1.shape
    n_fc = f // tf
    return pl.pallas_call(
        functools.partial(_moe_body, tm=tm, ch=ch, n_fc=n_fc),
        grid=(n // tm, ne, n_fc),
        in_specs=[pl.BlockSpec((tm, d), lambda i, e, c: (i, 0), pipeline_mode=pl.Buffered(1)),
                  pl.BlockSpec((1, d), lambda i, e, c: (0, 0)),
                  pl.BlockSpec((3, d, LANES), lambda i, e, c: (0, 0, 0)),
                  pl.BlockSpec((1, d, tf), lambda i, e, c: (e, 0, c)),
                  pl.BlockSpec((1, d, tf), lambda i, e, c: (e, 0, c)),
                  pl.BlockSpec((1, tf, d), lambda i, e, c: (e, c, 0))],
        out_specs=pl.BlockSpec((tm, d), lambda i, e, c: (i, 0)),
        out_shape=jax.ShapeDtypeStruct((n, d), F32),
        scratch_shapes=[pltpu.VMEM((tm, d), BF16), pltpu.VMEM((_round_up(tm, ch), d), BF16),
                        pltpu.VMEM((_round_up(tm, ch), d), F32), pltpu.VMEM((tm, LANES), F32),
                        pltpu.VMEM((tm, LANES), F32), pltpu.VMEM((LANES, tm), F32)],
        compiler_params=_params(("parallel", "arbitrary", "arbitrary")),
        name="moe",
    )(h, g.reshape(1, d), wr3, w1, w3, w2)


def _round_up(x, m):
    return (x + m - 1) // m * m


def _pad_rows(a, l_pad):
    return jnp.pad(a, ((0, 0), (0, l_pad - a.shape[1]), (0, 0)))


def _prep_weights(g_q_dsa, g_k_dsa, g_q_fox, g_k_fox, w_in_even, w_out_even, w_in_odd, w_router):
    d = w_in_even.shape[0]
    perm = jnp.asarray(DSA_HEAD_PERM)
    qa, ka, va, qb, kb, vb, qi, ki, wi = jnp.split(
        w_in_even, [512, 1024, 1536, 2048, 2176, 2304, 2816, 2880], axis=1)
    qb = qb.reshape(d, H_DSA, HEAD_DIM)[:, perm].reshape(d, H_DSA * HEAD_DIM)
    wi = jnp.pad(wi, ((0, 0), (0, LANES - IDX_HEADS)))
    w_even = jnp.concatenate([qb, kb, vb, ki, ki, wi, qi, qa, ka, va], axis=1).astype(BF16)
    ones = lambda k: jnp.ones((k,), F32)
    q_scale = lambda k: jnp.full((k,), Q_SCALE, F32)
    scale_even = jnp.concatenate([ones(E_QA), q_scale(E_KA - E_QA), ones(E_END - E_KA)])
    scale_odd = ones(O_END)
    scale_norm_even = jnp.concatenate([q_scale(E_KB - E_QB), ones(E_VB - E_KB)])
    scale_norm_odd = jnp.concatenate([q_scale(O_K - O_Q), ones(O_V - O_K)])
    w_out_sb = w_out_even[:H_SB * HEAD_DIM].astype(BF16)
    w_out_dsa = w_out_even[H_SB * HEAD_DIM:].reshape(H_DSA, HEAD_DIM, d)[perm]
    w_out_dsa = w_out_dsa.reshape(H_DSA * HEAD_DIM, d).astype(BF16)
    w_odd = jnp.pad(w_in_odd, ((0, 0), (0, O_END - w_in_odd.shape[1]))).astype(BF16)
    gains_even = jnp.concatenate([jnp.tile(g_q_dsa, H_DSA), jnp.tile(g_k_dsa, KV_DSA)])
    gains_odd = jnp.concatenate([jnp.tile(g_q_fox, H_FOX), jnp.tile(g_k_fox, H_FOX)])
    wr = jnp.pad(w_router, ((0, 0), (0, LANES - N_EXPERTS)))
    wr_hi = wr.astype(BF16)
    wr_r1 = wr - wr_hi.astype(F32)
    wr_mid = wr_r1.astype(BF16)
    wr_lo = (wr_r1 - wr_mid.astype(F32)).astype(BF16)
    return dict(w_even=w_even, w_out_sb=w_out_sb, w_out_dsa=w_out_dsa, w_odd=w_odd,
                gains_even=gains_even, gains_odd=gains_odd,
                scale_even=scale_even, scale_odd=scale_odd,
                scale_norm_even=scale_norm_even, scale_norm_odd=scale_norm_odd,
                wr3=jnp.stack([wr_hi, wr_mid, wr_lo]))


def _rotary_tables(pos):
    half = ROT_DIM // 2
    inv = ROPE_THETA ** (-jnp.arange(half, dtype=F32) / half)
    ang = pos.astype(F32)[:, None] * inv[None, :]
    cos, sin = jnp.cos(ang), jnp.sin(ang)
    t = pos.shape[0]
    pad = HEAD_DIM - ROT_DIM
    cos_h = jnp.concatenate([cos, cos, jnp.ones((t, pad), F32)], axis=1)
    sin_h = jnp.concatenate([-sin, sin, jnp.zeros((t, pad), F32)], axis=1)
    return jnp.tile(cos_h, (1, 2)), jnp.tile(sin_h, (1, 2))


def _trunk(x, p, pos0, past_even, past_odd, w, wb):
    b, t, d = x.shape
    n = b * t
    tm = min(ROW_TILE, n)
    has_past = past_even is not None
    past_len = past_even[0].shape[1] if has_past else 0
    l_valid = past_len + t
    l_pad = _round_up(l_valid, KEY_PAD)
    n_sel = min(DSA_TOPK, l_valid // 4)
    h = x.reshape(n, d)

    def keys(cache, new):
        if not has_past:
            return new
        full = jnp.concatenate([cache.reshape(b, past_len, -1).astype(BF16), new], axis=1)
        return _pad_rows(full, l_pad)

    pf, pb = _proj(h, w["g_mix"][0], wb["w_even"], wb["scale_even"], tm)
    cos, sin = _rotary_tables(pos0 + jnp.arange(t, dtype=I32))
    if t % tm:
        cos, sin = jnp.tile(cos, (tm // t, 1)), jnp.tile(sin, (tm // t, 1))
    qkf, qkb = _headnorm(pf, E_QB, wb["gains_even"], wb["scale_norm_even"], cos, sin, tm)
    pf3, pb3 = pf.reshape(b, t, E_END), pb.reshape(b, t, E_END)
    qkf3, qkb3 = qkf.reshape(b, t, E_NORM), qkb.reshape(b, t, E_NORM)
    cols = lambda a, c0, width: a[:, :, c0:c0 + width]
    sb_w, kv_w = H_SB * HEAD_DIM, KV_DSA * HEAD_DIM
    if has_past:
        c_sbk, c_sbv, c_dk, c_dv, c_ki = past_even
        ka, ka_cb = keys(c_sbk, cols(pb3, E_KA, sb_w)), 0
        va, va_cb = keys(c_sbv, cols(pb3, E_VA, sb_w)), 0
        kb, kb_cb = keys(c_dk, cols(qkb3, E_KB, kv_w)), 0
        vb, vb_cb = keys(c_dv, cols(pb3, E_VB, kv_w)), 0
        ki2 = jnp.concatenate([c_ki, c_ki], axis=-1)
        ki, ki_cb = keys(ki2, cols(pb3, E_KI, LANES)), 0
    else:
        ka, ka_cb = pb3, E_KA // LANES
        va, va_cb = pb3, E_VA // LANES
        kb, kb_cb = qkb3, E_KB // LANES
        vb, vb_cb = pb3, E_VB // LANES
        ki, ki_cb = pb3, E_KI // LANES
    oa = _sb_attention(pb3, E_QA // LANES, ka, ka_cb, va, va_cb, H_SB // 2, pos0,
                       min(SB_Q_TILE, t), SB_K_TILE)
    ob = _dsa_attention(qkb3, pb3, E_QI // 512, pf3, E_WI // LANES, kb, kb_cb, vb, vb_cb,
                        ki, ki_cb, pos0, l_valid, n_sel, min(DSA_Q_TILE, t), DSA_K_TILE)
    h = _even_channel(h, oa.reshape(n, -1), ob.reshape(n, -1), wb["w_out_sb"], wb["w_out_dsa"],
                      w["g_ffn"][0], wb["w_ff1"], wb["w_ff3"], wb["w_ff2"],
                      w["g_ple"][0], wb["w_ple_gate"][0], p[0].reshape(n, -1), wb["w_ple_in"][0], tm)
    even_state = (cols(pf3, E_KA, sb_w).reshape(b, t, H_SB, HEAD_DIM),
                  cols(pf3, E_VA, sb_w).reshape(b, t, H_SB, HEAD_DIM),
                  cols(qkf3, E_KB, kv_w).reshape(b, t, KV_DSA, HEAD_DIM),
                  cols(pf3, E_VB, kv_w).reshape(b, t, KV_DSA, HEAD_DIM),
                  cols(pf3, E_KI, IDX_DIM))

    fox_w = H_FOX * HEAD_DIM
    pf, pb = _proj(h, w["g_mix"][1], wb["w_odd"], wb["scale_odd"], tm)
    qkf, qkb = _headnorm(pf, O_Q, wb["gains_odd"], wb["scale_norm_odd"], None, None, tm)
    pf3, pb3 = pf.reshape(b, t, O_END), pb.reshape(b, t, O_END)
    qkf3, qkb3 = qkf.reshape(b, t, O_V), qkb.reshape(b, t, O_V)
    gate_pre = cols(pf3, O_F, H_FOX)
    if has_past:
        c_fk, c_fv, c_lf = past_odd
        kf, kf_cb = keys(c_fk, cols(qkb3, O_K, fox_w)), 0
        vf, vf_cb = keys(c_fv, cols(pb3, O_V, fox_w)), 0
        raw = _pad_rows(jnp.concatenate([c_lf, gate_pre], axis=1), l_pad)
    else:
        kf, kf_cb = qkb3, O_K // LANES
        vf, vf_cb = pb3, O_V // LANES
        raw = gate_pre
    logf, cum = _forget_cumsum(raw, w["b_forget"][0], past_len, min(KEY_PAD, l_pad))
    logf, cum = logf[:, :, :H_FOX], cum[:, :, :H_FOX]
    fk = cum.reshape(b, l_pad, H_FOX // 2, 2).transpose(0, 2, 3, 1)

    normed_max = lambda g: HEAD_DIM ** 0.5 * jnp.max(jnp.abs(g))
    q_max = jnp.full((b, H_FOX), Q_SCALE * normed_max(w["g_q_fox"]), F32)
    if has_past:
        kx = kf.astype(F32).reshape(b, l_pad, H_FOX, HEAD_DIM)
        k_max = jnp.sqrt(jnp.max(jnp.sum(kx * kx, axis=-1), axis=1))
    else:
        k_max = jnp.full((b, H_FOX), normed_max(w["g_k_fox"]), F32)
    decay_cut = EXP2_ZERO + 2.0 * 1.02 * q_max * k_max
    thr = jnp.broadcast_to(decay_cut.reshape(b, H_FOX // 2, 2, 1), (b, H_FOX // 2, 2, LANES))
    of = _fox_attention(qkb3, O_Q // LANES, kf, kf_cb, vf, vf_cb, fk, thr, pos0,
                        min(FOX_Q_TILE, t), FOX_K_TILE)
    h = _residual_matmul(h, [of.reshape(n, -1)], [wb["w_out_odd"]], tm)
    h = _moe(h, w["g_ffn"][1], wb["wr3"], wb["w_exp1"], wb["w_exp3"], wb["w_exp2"],
             min(MOE_TILE, n), min(MOE_CHUNK, n), MOE_F_TILE)
    h = _ple(h, w["g_ple"][1], wb["w_ple_gate"][1], p[1].reshape(n, -1), wb["w_ple_in"][1], tm)
    odd_state = (cols(qkf3, O_K, fox_w).reshape(b, t, H_FOX, HEAD_DIM),
                 cols(pf3, O_V, fox_w).reshape(b, t, H_FOX, HEAD_DIM),
                 logf[:, past_len:l_valid])
    return h.reshape(b, t, d), even_state, odd_state


def kernel(x_prompt, x_sample, p_prompt, p_sample, cache_sb_k, cache_sb_v, cache_dsa_k, cache_dsa_v, cache_dsa_kidx, cache_fox_k, cache_fox_v, cache_fox_logf, g_mix, g_ffn, g_ple, w_in_even, g_q_dsa, g_k_dsa, w_out_even, w_ff1, w_ff3, w_ff2, w_in_odd, b_forget, g_q_fox, g_k_fox, w_out_odd, w_router, w_exp1, w_exp3, w_exp2, w_ple_in, w_ple_gate):
    assert g_mix.shape[0] == 2, "two layers: one even (stick-breaking + DSA), one odd (FoX + experts)"
    past_len = cache_sb_k.shape[2]
    w = dict(g_mix=g_mix, g_ffn=g_ffn, g_ple=g_ple, b_forget=b_forget,
             g_q_fox=g_q_fox[0], g_k_fox=g_k_fox[0])
    wb = _prep_weights(g_q_dsa[0], g_k_dsa[0], g_q_fox[0], g_k_fox[0],
                       w_in_even[0], w_out_even[0], w_in_odd[0], w_router[0])
    wb.update(w_ff1=w_ff1[0].astype(BF16), w_ff3=w_ff3[0].astype(BF16), w_ff2=w_ff2[0].astype(BF16),
              w_out_odd=w_out_odd[0].astype(BF16),
              w_exp1=w_exp1[0].astype(BF16), w_exp3=w_exp3[0].astype(BF16),
              w_exp2=w_exp2[0].astype(BF16),
              w_ple_in=w_ple_in.astype(BF16), w_ple_gate=w_ple_gate.astype(BF16))

    y_p, even_p, odd_p = _trunk(x_prompt, p_prompt, 0, None, None, w, wb)
    y_s, even_s, odd_s = _trunk(
        x_sample, p_sample, past_len,
        (cache_sb_k[0], cache_sb_v[0], cache_dsa_k[0], cache_dsa_v[0], cache_dsa_kidx[0]),
        (cache_fox_k[0], cache_fox_v[0], cache_fox_logf[0]), w, wb)
    tail = lambda a: a[:, -past_len:][None]
    whole = lambda a: a[None]
    return (y_p, y_s,
            *(tail(a) for a in even_p), *(tail(a) for a in odd_p),
            *(whole(a) for a in even_s), *(whole(a) for a in odd_s))
```

```python
import functools

import jax
import jax.numpy as jnp
from jax import lax
from jax.experimental import pallas as pl
from jax.experimental.pallas import tpu as pltpu

F32 = jnp.float32
BF16 = jnp.bfloat16
I32 = jnp.int32

EPS = 1e-6
HEAD_DIM = 64
CHUNK = 64
H_SB = 8
H_DSA = 8
KV_DSA = 2
IDX_HEADS = 8
IDX_DIM = 64
H_FOX = 16
DSA_TOPK = 256
ROT_DIM = HEAD_DIM // 4
ROPE_THETA = 500000.0
N_EXPERTS = 8
ATT_SCALE = HEAD_DIM ** -0.5
LOG2E = 1.4426950408889634
Q_SCALE = ATT_SCALE * LOG2E

LANES = 128
MIB = 1024 * 1024
VMEM_LIMIT = 56 * MIB

ROW_TILE = 512
SB_Q_TILE, SB_K_TILE = 256, 256
SB_BLOCKS_PER_STEP = 1
EXP2_ZERO = 150.0
FOX_Q_TILE, FOX_K_TILE = 512, 512
FOX_PAIRS = 1
DSA_Q_TILE, DSA_K_TILE = 256, 512
COUNT_ROWS = 128
SCORE_SUB = 256
KEY_PAD = 512
MOE_TILE = 1024
MOE_CHUNK = 304
MOE_F_TILE = 1792

NEG_BIG = -1e30
M_INIT = -1e29
INT_MIN = -(2 ** 31)
INT_MAX = 2 ** 31 - 1
KEY_NEG_INF = -2139095041

E_QB, E_KB, E_VB, E_KI, E_WI, E_QI, E_QA, E_KA, E_VA, E_END = (
    0, 512, 640, 768, 896, 1024, 1536, 2048, 2560, 3072)
E_NORM = E_VB - E_QB
O_Q, O_K, O_V, O_F, O_END = 0, 1024, 2048, 3072, 3200
DSA_HEAD_PERM = (0, 4, 1, 5, 2, 6, 3, 7)


def _params(sem, vmem=VMEM_LIMIT):
    return pltpu.CompilerParams(dimension_semantics=sem, vmem_limit_bytes=vmem)


def _dot(a, b):
    return jnp.dot(a, b, preferred_element_type=F32)


def _dot_nt(a, b):
    return lax.dot_general(a, b, (((1,), (1,)), ((), ())), preferred_element_type=F32)


def _rms(x, g):
    return x * lax.rsqrt(jnp.mean(x * x, axis=-1, keepdims=True) + EPS) * g


def _split3(x):
    hi = x.astype(BF16)
    r1 = x - hi.astype(F32)
    mid = r1.astype(BF16)
    lo = (r1 - mid.astype(F32)).astype(BF16)
    return hi, mid, lo


def _sigmoid(x):
    return 1.0 / (1.0 + jnp.exp(-x))


def _log_sigmoid_neg(z):
    return -(jnp.maximum(z, 0.0) + jnp.log(1.0 + jnp.exp(-jnp.abs(z))))


def _proj_body(x_ref, g_ref, w_ref, sc_ref, of_ref, ob_ref):
    y = _rms(x_ref[...], g_ref[...]).astype(BF16)
    r = _dot(y, w_ref[...])
    of_ref[...] = r
    ob_ref[...] = (r * sc_ref[...]).astype(BF16)


def _proj(x, g, w, scales, tm):
    n, d = x.shape
    c = w.shape[1]
    return pl.pallas_call(
        _proj_body,
        grid=(n // tm,),
        in_specs=[pl.BlockSpec((tm, d), lambda i: (i, 0)),
                  pl.BlockSpec((1, d), lambda i: (0, 0)),
                  pl.BlockSpec((d, c), lambda i: (0, 0)),
                  pl.BlockSpec((1, c), lambda i: (0, 0))],
        out_specs=[pl.BlockSpec((tm, c), lambda i: (i, 0)),
                   pl.BlockSpec((tm, c), lambda i: (i, 0))],
        out_shape=[jax.ShapeDtypeStruct((n, c), F32), jax.ShapeDtypeStruct((n, c), BF16)],
        compiler_params=_params(("parallel",)),
        name="proj",
    )(x, g.reshape(1, d), w, scales.reshape(1, c))


def _headnorm_body(x_ref, g_ref, sc_ref, cos_ref, sin_ref, s_ref, of_ref, ob_ref, *, rotary, nblk):
    s = s_ref[...]
    for j in range(nblk):
        cols = slice(j * LANES, (j + 1) * LANES)
        x = x_ref[:, cols]
        hi, mid, lo = _split3(x * x)
        ms = (_dot(hi, s) + _dot(mid, s) + _dot(lo, s)) * (1.0 / HEAD_DIM)
        y = x * lax.rsqrt(ms + EPS) * g_ref[:, cols]
        if rotary:
            lane = lax.broadcasted_iota(I32, (1, LANES), 1) % HEAD_DIM
            partner = jnp.where(lane < ROT_DIM // 2,
                                pltpu.roll(y, LANES - ROT_DIM // 2, 1),
                                pltpu.roll(y, ROT_DIM // 2, 1))
            y = y * cos_ref[...] + partner * sin_ref[...]
        of_ref[:, cols] = y
        ob_ref[:, cols] = (y * sc_ref[:, cols]).astype(BF16)


def _headnorm(x, col0, gains, scales, cos, sin, tm):
    n = x.shape[0]
    width = gains.shape[0]
    nblk = width // LANES
    assert col0 % width == 0
    rotary = cos is not None
    if not rotary:
        cos = jnp.zeros((8, LANES), F32)
        sin = cos
        tab_spec = pl.BlockSpec((8, LANES), lambda i: (0, 0))
    else:
        nt = cos.shape[0] // tm
        tab_spec = pl.BlockSpec((tm, LANES), lambda i: (i % nt, 0))
    r = lax.broadcasted_iota(I32, (LANES, LANES), 0) // HEAD_DIM
    c = lax.broadcasted_iota(I32, (LANES, LANES), 1) // HEAD_DIM
    seg = (r == c).astype(BF16)
    return pl.pallas_call(
        functools.partial(_headnorm_body, rotary=rotary, nblk=nblk),
        grid=(n // tm,),
        in_specs=[pl.BlockSpec((tm, width), lambda i: (i, col0 // width)),
                  pl.BlockSpec((1, width), lambda i: (0, 0)),
                  pl.BlockSpec((1, width), lambda i: (0, 0)),
                  tab_spec, tab_spec,
                  pl.BlockSpec((LANES, LANES), lambda i: (0, 0))],
        out_specs=[pl.BlockSpec((tm, width), lambda i: (i, 0)),
                   pl.BlockSpec((tm, width), lambda i: (i, 0))],
        out_shape=[jax.ShapeDtypeStruct((n, width), F32),
                   jax.ShapeDtypeStruct((n, width), BF16)],
        compiler_params=_params(("parallel",)),
        name="headnorm",
    )(x, gains.reshape(1, width), scales.reshape(1, width), cos, sin, seg)


def _res_body(*refs, n_in):
    h_ref, o_ref = refs[0], refs[-1]
    acc = h_ref[...]
    for t in range(n_in):
        acc = acc + _dot(refs[1 + t][...], refs[1 + n_in + t][...])
    o_ref[...] = acc


def _residual_matmul(h, acts, ws, tm):
    n, d = h.shape
    n_in = len(acts)
    in_specs = [pl.BlockSpec((tm, d), lambda i: (i, 0))]
    in_specs += [pl.BlockSpec((tm, a.shape[1]), lambda i: (i, 0)) for a in acts]
    in_specs += [pl.BlockSpec(w.shape, lambda i: (0, 0)) for w in ws]
    return pl.pallas_call(
        functools.partial(_res_body, n_in=n_in),
        grid=(n // tm,),
        in_specs=in_specs,
        out_specs=pl.BlockSpec((tm, d), lambda i: (i, 0)),
        out_shape=jax.ShapeDtypeStruct((n, d), F32),
        compiler_params=_params(("parallel",)),
        name="residual_matmul",
    )(h, *acts, *ws)


def _ple_body(h_ref, g_ref, wg_ref, p_ref, wp_ref, o_ref):
    x = h_ref[...]
    xn = _rms(x, g_ref[...]).astype(BF16)
    gate = _sigmoid(_dot(xn, wg_ref[...]))
    o_ref[...] = x + gate * _dot(p_ref[...].astype(BF16), wp_ref[...])


def _ple(h, g, wg, p, wp, tm):
    n, d = h.shape
    e = p.shape[1]
    return pl.pallas_call(
        _ple_body,
        grid=(n // tm,),
        in_specs=[pl.BlockSpec((tm, d), lambda i: (i, 0)),
                  pl.BlockSpec((1, d), lambda i: (0, 0)),
                  pl.BlockSpec((d, d), lambda i: (0, 0)),
                  pl.BlockSpec((tm, e), lambda i: (i, 0)),
                  pl.BlockSpec((e, d), lambda i: (0, 0))],
        out_specs=pl.BlockSpec((tm, d), lambda i: (i, 0)),
        out_shape=jax.ShapeDtypeStruct((n, d), F32),
        compiler_params=_params(("parallel",)),
        name="ple",
    )(h, g.reshape(1, d), wg, p, wp)


def _even_channel_body(h_ref, oa_ref, ob_ref, wa_ref, wb_ref, gf_ref, w1_ref, w3_ref, w2_ref,
                       gp_ref, wg_ref, p_ref, wp_ref, o_ref):
    x = h_ref[...] + _dot(oa_ref[...], wa_ref[...]) + _dot(ob_ref[...], wb_ref[...])
    xn = _rms(x, gf_ref[...]).astype(BF16)
    a = _dot(xn, w1_ref[...])
    b = _dot(xn, w3_ref[...])
    x = x + _dot((a * _sigmoid(a) * b).astype(BF16), w2_ref[...])
    xn = _rms(x, gp_ref[...]).astype(BF16)
    gate = _sigmoid(_dot(xn, wg_ref[...]))
    o_ref[...] = x + gate * _dot(p_ref[...].astype(BF16), wp_ref[...])


def _even_channel(h, oa, ob, wa, wb, gf, w1, w3, w2, gp, wg, p, wp, tm):
    n, d = h.shape
    once = pl.Buffered(1)
    row = lambda a: pl.BlockSpec((tm, a.shape[1]), lambda i: (i, 0))
    vec = pl.BlockSpec((1, d), lambda i: (0, 0))
    full = lambda a: pl.BlockSpec(a.shape, lambda i: (0, 0), pipeline_mode=once)
    return pl.pallas_call(
        _even_channel_body,
        grid=(n // tm,),
        in_specs=[row(h), row(oa), row(ob), full(wa), full(wb), vec, full(w1), full(w3), full(w2),
                  vec, full(wg), row(p), full(wp)],
        out_specs=row(h),
        out_shape=jax.ShapeDtypeStruct((n, d), F32),
        compiler_params=_params(("parallel",)),
        name="even_channel",
    )(h, oa, ob, wa, wb, gf.reshape(1, d), w1, w3, w2, gp.reshape(1, d), wg, p, wp)


def _sb_body(q_ref, k_ref, v_ref, later_ref, o_ref, *, tq, tk, q_pos0, n_kb_total):
    row0 = q_pos0 + pl.program_id(2) * tq
    q = q_ref[0]
    lane = lax.broadcasted_iota(I32, (1, LANES), 1)
    lo_half = lane < HEAD_DIM
    zero = jnp.zeros_like(q)
    q_halves = (jnp.where(lo_half, q, zero), jnp.where(lo_half, zero, q))
    qpos = row0 + lax.broadcasted_iota(I32, (tq, 1), 0)
    nkb = jnp.minimum((row0 + tq - 1 + tk - 1) // tk, n_kb_total)
    n_full = jnp.minimum(row0 // tk, nkb)
    later = later_ref[...]

    def block(half, kb, c, masked):
        ks = pl.multiple_of(kb * tk, tk)
        z = _dot_nt(q_halves[half], k_ref[0, pl.ds(ks, tk), :])
        sp = jnp.maximum(z, 0.0) + jnp.log2(1.0 + jnp.exp2(-jnp.abs(z)))
        if masked:
            vis = (ks + lax.broadcasted_iota(I32, (1, tk), 1)) < qpos
            sp = jnp.where(vis, sp, 0.0)
        between = _dot(sp.astype(BF16), later)
        w = jnp.exp2(z - sp - between - c)
        if masked:
            w = jnp.where(vis, w, 0.0)
        row_sum = between[:, 0:1] + sp[:, 0:1]
        return _dot(w.astype(BF16), v_ref[0, pl.ds(ks, tk), :]), row_sum

    def run(kbs, carry, masked):
        accs, cs = carry
        new_accs, new_cs = [], []
        for half in range(2):
            acc, c = accs[half], cs[half]
            for kb in kbs:
                pv, rs = block(half, kb, c, masked)
                acc, c = acc + pv, c + rs
            new_accs.append(acc)
            new_cs.append(c)
        return tuple(new_accs), tuple(new_cs)

    acc0 = jnp.zeros((tq, LANES), F32)
    c0 = jnp.zeros((tq, 1), F32)
    carry = ((acc0, acc0), (c0, c0))
    carry = lax.fori_loop(0, nkb - n_full,
                          lambda j, cr: run([nkb - 1 - j], cr, True), carry)
    def live(cr):
        return jnp.minimum(jnp.min(cr[1][0]), jnp.min(cr[1][1])) < EXP2_ZERO

    def steps(n_steps, kbs_of, carry):
        def cond(state):
            j, _, more = state
            return (j < n_steps) & more

        def body(state):
            j, cr, _ = state
            cr = run(kbs_of(j), cr, False)
            return j + 1, cr, live(cr)

        return lax.while_loop(cond, body, (0, carry, live(carry)))[1]

    grp = SB_BLOCKS_PER_STEP
    carry = steps(n_full // grp, lambda j: [n_full - 1 - grp * j - u for u in range(grp)], carry)
    rem = n_full % grp
    carry = steps(rem, lambda j: [rem - 1 - j], carry)
    accs, _ = carry
    o_ref[0] = jnp.where(lo_half, accs[0], accs[1]).astype(BF16)


def _sb_attention(q, q_cb, k, k_cb, v, v_cb, n_pairs, q_pos0, tq, tk):
    b, t = q.shape[:2]
    l = k.shape[1]
    later = (lax.broadcasted_iota(I32, (tk, tk), 0) >
             lax.broadcasted_iota(I32, (tk, tk), 1)).astype(BF16)
    return pl.pallas_call(
        functools.partial(_sb_body, tq=tq, tk=tk, q_pos0=q_pos0, n_kb_total=l // tk),
        grid=(b, n_pairs, t // tq),
        in_specs=[pl.BlockSpec((1, tq, LANES), lambda bi, p, i: (bi, i, q_cb + p)),
                  pl.BlockSpec((1, l, LANES), lambda bi, p, i: (bi, 0, k_cb + p)),
                  pl.BlockSpec((1, l, LANES), lambda bi, p, i: (bi, 0, v_cb + p)),
                  pl.BlockSpec((tk, tk), lambda bi, p, i: (0, 0))],
        out_specs=pl.BlockSpec((1, tq, LANES), lambda bi, p, i: (bi, i, p)),
        out_shape=jax.ShapeDtypeStruct((b, t, n_pairs * LANES), BF16),
        compiler_params=_params(("parallel", "parallel", "parallel")),
        name="sb_attention",
    )(q, k, v, later)


def _softmax_step(s, v, m, l, acc):
    m_new = jnp.maximum(m, jnp.max(s, axis=1, keepdims=True))
    alpha = jnp.exp2(m - m_new)
    p = jnp.exp2(s - m_new)
    l_new = alpha * l + jnp.sum(p, axis=1, keepdims=True)
    acc_new = alpha * acc + _dot(p.astype(BF16), v)
    return m_new, l_new, acc_new


def _fox_body(q_ref, k_ref, v_ref, fk_ref, j0_ref, o_ref, q_s,
              *, tq, tk, q_pos0, n_kb_total, npp):
    row0 = q_pos0 + pl.program_id(2) * tq
    lane = lax.broadcasted_iota(I32, (1, LANES), 1)
    lo_half = lane < HEAD_DIM
    for pp in range(npp):
        q = q_ref[0, :, pp * LANES:(pp + 1) * LANES]
        zero = jnp.zeros_like(q)
        q_s[2 * pp] = jnp.where(lo_half, q, zero)
        q_s[2 * pp + 1] = jnp.where(lo_half, zero, q)
    qpos = row0 + lax.broadcasted_iota(I32, (tq, 1), 0)
    nkb = jnp.minimum((row0 + tq + tk - 1) // tk, n_kb_total)
    n_full = jnp.minimum((row0 + 1) // tk, nkb)

    def body(j, carry, masked, width):
        ks = pl.multiple_of(j * tk, tk)
        out = []
        for h in range(2 * npp):
            pp, half = divmod(h, 2)
            k = k_ref[0, pl.ds(ks, width), pp * LANES:(pp + 1) * LANES]
            v = v_ref[0, pl.ds(ks, width), pp * LANES:(pp + 1) * LANES]
            m, l, acc = carry[h]
            fk = fk_ref[0, pp, half:half + 1, pl.ds(ks, width)] * LOG2E
            s = _dot_nt(q_s[h], k) - fk
            if masked:
                vis = (ks + lax.broadcasted_iota(I32, (1, width), 1)) <= qpos
                s = jnp.where(vis, s, NEG_BIG)
            out.append(_softmax_step(s, v, m, l, acc))
        return tuple(out)

    j0 = jnp.minimum(j0_ref[pl.program_id(0), pl.program_id(1), pl.program_id(2)], n_full)

    init = (jnp.full((tq, 1), M_INIT, F32), jnp.zeros((tq, 1), F32), jnp.zeros((tq, LANES), F32))
    n_wide = (n_full - j0) // 2
    carry = lax.fori_loop(0, n_wide,
                          lambda j, c: body(j0 + 2 * j, c, masked=False, width=2 * tk),
                          (init,) * (2 * npp))
    carry = lax.fori_loop(j0 + 2 * n_wide, n_full,
                          functools.partial(body, masked=False, width=tk), carry)
    carry = lax.fori_loop(n_full, nkb, functools.partial(body, masked=True, width=tk), carry)
    for pp in range(npp):
        (_, l0, a0), (_, l1, a1) = carry[2 * pp], carry[2 * pp + 1]
        o_ref[0, :, pp * LANES:(pp + 1) * LANES] = jnp.where(lo_half, a0 / l0, a1 / l1).astype(BF16)


def _fox_first_block(cum, decay_cut, q_pos0, t, tq, tk, npp):
    b, l, h = cum.shape
    f2 = cum * LOG2E
    f_tile = jnp.max(f2[:, q_pos0:q_pos0 + t].reshape(b, t // tq, tq, h), axis=2)
    gone = (f2[:, None] - f_tile[:, :, None]) > decay_cut[:, None, None, :]
    first_needed = jnp.min(jnp.where(gone, l, jnp.arange(l, dtype=I32)[None, None, :, None]), axis=2)
    j0 = jnp.min((first_needed // tk).reshape(b, t // tq, h // (2 * npp), 2 * npp), axis=-1)
    return j0.transpose(0, 2, 1).astype(I32)


def _fox_attention(q, q_cb, k, k_cb, v, v_cb, fk, j0, q_pos0, tq, tk):
    b, t = q.shape[:2]
    l = k.shape[1]
    n_pairs = H_FOX // 2
    npp = FOX_PAIRS
    w = npp * LANES
    assert q_cb % npp == 0 and k_cb % npp == 0 and v_cb % npp == 0
    return pl.pallas_call(
        functools.partial(_fox_body, tq=tq, tk=tk, q_pos0=q_pos0, n_kb_total=l // tk, npp=npp),
        grid=(b, n_pairs // npp, t // tq),
        in_specs=[pl.BlockSpec((1, tq, w), lambda bi, p, i: (bi, i, q_cb // npp + p)),
                  pl.BlockSpec((1, l, w), lambda bi, p, i: (bi, 0, k_cb // npp + p)),
                  pl.BlockSpec((1, l, w), lambda bi, p, i: (bi, 0, v_cb // npp + p)),
                  pl.BlockSpec((1, npp, 2, l), lambda bi, p, i: (bi, p, 0, 0)),
                  pl.BlockSpec(memory_space=pltpu.SMEM)],
        out_specs=pl.BlockSpec((1, tq, w), lambda bi, p, i: (bi, i, p)),
        out_shape=jax.ShapeDtypeStruct((b, t, n_pairs * LANES), BF16),
        scratch_shapes=[pltpu.VMEM((2 * npp, tq, LANES), BF16)],
        compiler_params=_params(("parallel", "parallel", "parallel")),
        name="fox_attention",
    )(q, k, v, fk, j0)


def _forget_body(raw_ref, b_ref, logf_ref, cum_ref, carry_ref, *, tb, n_given):
    j = pl.program_id(1)

    @pl.when(j == 0)
    def _():
        carry_ref[...] = jnp.zeros_like(carry_ref)

    raw = raw_ref[0]
    z = raw + b_ref[...]
    computed = _log_sigmoid_neg(-z)
    row = j * tb + lax.broadcasted_iota(I32, (tb, 1), 0)
    logf = jnp.where(row < n_given, raw, computed)
    incl = (lax.broadcasted_iota(I32, (tb, tb), 1) <=
            lax.broadcasted_iota(I32, (tb, tb), 0)).astype(BF16)
    hi, mid, lo = _split3(logf)
    cum = _dot(incl, hi) + _dot(incl, mid) + _dot(incl, lo) + carry_ref[0:1, :]
    logf_ref[0] = logf
    cum_ref[0] = cum
    carry_ref[0:1, :] = cum[tb - 1:tb, :]


def _forget_cumsum(raw, bias, n_given, tb):
    raw = jnp.pad(raw, ((0, 0), (0, 0), (0, LANES - raw.shape[2])))
    bias = jnp.pad(bias, (0, LANES - bias.shape[0]))
    b, l, h = raw.shape
    return pl.pallas_call(
        functools.partial(_forget_body, tb=tb, n_given=n_given),
        grid=(b, l // tb),
        in_specs=[pl.BlockSpec((1, tb, h), lambda bi, j: (bi, j, 0)),
                  pl.BlockSpec((1, h), lambda bi, j: (0, 0))],
        out_specs=[pl.BlockSpec((1, tb, h), lambda bi, j: (bi, j, 0)),
                   pl.BlockSpec((1, tb, h), lambda bi, j: (bi, j, 0))],
        out_shape=[jax.ShapeDtypeStruct((b, l, h), F32), jax.ShapeDtypeStruct((b, l, h), F32)],
        scratch_shapes=[pltpu.VMEM((8, h), F32)],
        compiler_params=_params(("parallel", "arbitrary")),
        name="forget_cumsum",
    )(raw, bias.reshape(1, h))


def _dsa_body(q_ref, qi_ref, wi_ref, k_ref, v_ref, ki_ref, o_ref,
              key_s, bias_s, qi_s, q_s, *, tq, tk, q_pos0, l_valid, n_kb_total, n_sel):
    row0 = q_pos0 + pl.program_id(1) * tq
    lane = lax.broadcasted_iota(I32, (1, LANES), 1)
    lo_half = lane < HEAD_DIM
    qpos = row0 + lax.broadcasted_iota(I32, (tq, 1), 0)
    qchunk = qpos // CHUNK
    kend = jnp.minimum(((row0 + tq - 1) // CHUNK + 1) * CHUNK, l_valid)
    nkb = jnp.minimum(jnp.maximum((kend + tk - 1) // tk, (n_sel + tk - 1) // tk), n_kb_total)

    n_stack = IDX_HEADS // 2
    for p in range(n_stack):
        blk = qi_ref[0, :, p * LANES:(p + 1) * LANES]
        zero = jnp.zeros_like(blk)
        qi_s[0, p * tq:(p + 1) * tq, :] = jnp.where(lo_half, blk, zero)
        qi_s[1, p * tq:(p + 1) * tq, :] = jnp.where(lo_half, zero, blk)
    wsc = wi_ref[0] * (IDX_DIM ** -0.5 * IDX_HEADS ** -0.5)

    def wide_then_single(step, init):
        carry = lax.fori_loop(0, nkb // 2, lambda j, c: step(2 * j, c, 2 * tk), init)
        return lax.fori_loop(nkb // 2 * 2, nkb, lambda j, c: step(j, c, tk), carry)

    def score_body(j, _, width):
        for u in range(width // SCORE_SUB):
            ks = pl.multiple_of(j * tk + u * SCORE_SUB, SCORE_SUB)
            ki = ki_ref[0, pl.ds(ks, SCORE_SUB), :]
            score = jnp.zeros((tq, SCORE_SUB), F32)
            for half in range(2):
                rel = jnp.maximum(_dot_nt(qi_s[half], ki), 0.0)
                for p in range(n_stack):
                    h = 2 * p + half
                    score = score + wsc[:, h:h + 1] * rel[p * tq:(p + 1) * tq]
            kpos = ks + lax.broadcasted_iota(I32, (1, SCORE_SUB), 1)
            vis = ((kpos // CHUNK) <= qchunk) & (kpos < l_valid)
            score = jnp.where(vis, score, -jnp.inf)
            bits = lax.bitcast_convert_type(score, I32)
            key = bits ^ ((bits >> 31) & INT_MAX)
            key_s[:, pl.ds(ks, SCORE_SUB)] = jnp.where(bits == INT_MIN, 0, key)
        return 0

    wide_then_single(score_body, 0)

    n_sel_f = float(n_sel)
    cr = min(COUNT_ROWS, tq)

    slabs = [slice(r * cr, (r + 1) * cr) for r in range(tq // cr)]

    def count(rs, pred):
        def cbody(j, acc):
            blk = key_s[rs, pl.ds(pl.multiple_of(j * tk, tk), tk)]
            hit = jnp.where(pred(blk, j), 1.0, 0.0)
            for c in range(tk // LANES):
                acc = acc + hit[:, c * LANES:(c + 1) * LANES]
            return acc
        acc = lax.fori_loop(0, nkb, cbody, jnp.zeros((cr, LANES), F32))
        return jnp.sum(acc, axis=1, keepdims=True)

    c_zero = [count(rs, lambda blk, j: blk >= 0) for rs in slabs]
    thrs = tuple(jnp.where(c >= n_sel_f, 0, INT_MIN) for c in c_zero)
    c_ges = tuple(jnp.where(c >= n_sel_f, c, (nkb * tk).astype(F32)) for c in c_zero)

    def unsettled(c_ges):
        open_rows = functools.reduce(jnp.maximum, [jnp.where(c == n_sel_f, 0.0, 1.0) for c in c_ges])
        return jnp.max(open_rows) > 0.0

    def try_bit(bit, thrs, c_ges):
        new_thrs, new_c_ges = [], []
        for rs, thr, c_ge in zip(slabs, thrs, c_ges):
            cand = thr | (1 << bit)
            c = count(rs, lambda blk, j, cand=cand: blk >= cand)
            keep = c >= n_sel_f
            new_thrs.append(jnp.where(keep, cand, thr))
            new_c_ges.append(jnp.where(keep, c, c_ge))
        return tuple(new_thrs), tuple(new_c_ges)

    def bit_cond(state):
        bit, _, _, more = state
        return (bit >= 0) & more

    def bit_body(state):
        bit, thrs, c_ges, _ = state
        thrs, c_ges = try_bit(bit, thrs, c_ges)
        thrs, c_ges = try_bit(bit - 1, thrs, c_ges)
        return bit - 2, thrs, c_ges, unsettled(c_ges)

    thrs, c_ges = try_bit(30, thrs, c_ges)
    _, thrs, c_ges, _ = lax.while_loop(bit_cond, bit_body, (29, thrs, c_ges, unsettled(c_ges)))

    for rs, thr, c_ge in zip(slabs, thrs, c_ges):
        tied = (c_ge > n_sel_f) & (thr > KEY_NEG_INF)
        any_tied = jnp.max(jnp.where(tied, 1.0, 0.0)) > 0.0

        def tie_index_bound(rs=rs, thr=thr, tied=tied):
            need = n_sel_f - count(rs, lambda blk, j: blk > thr)

            def kidx(j):
                return j * tk + lax.broadcasted_iota(I32, (1, tk), 1)

            n_bits = (n_kb_total * tk - 1).bit_length()

            def jbody(b, jb):
                cand = jb | (1 << (n_bits - 1 - b))
                cnt = count(rs, lambda blk, j: (blk == thr) & (kidx(j) < cand))
                return jnp.where(cnt < need, cand, jb)

            jb = lax.fori_loop(0, n_bits, jbody, jnp.zeros((cr, 1), I32))
            return jnp.where(tied, jb, INT_MAX)

        jbound = lax.cond(any_tied, tie_index_bound, lambda: jnp.full((cr, 1), INT_MAX, I32))

        def bias_body(j, _, rs=rs, thr=thr, jbound=jbound):
            ks = pl.multiple_of(j * tk, tk)
            blk = key_s[rs, pl.ds(ks, tk)]
            kpos = ks + lax.broadcasted_iota(I32, (1, tk), 1)
            sel = (blk > thr) | ((blk == thr) & (kpos <= jbound))
            sel = sel & (blk > KEY_NEG_INF)
            bias_s[rs, pl.ds(ks, tk)] = jnp.where(sel, 0.0, NEG_BIG)
            return 0

        lax.fori_loop(0, nkb, bias_body, 0)

    n_grp = H_DSA // KV_DSA
    for p in range(n_grp):
        blk = q_ref[0, :, p * LANES:(p + 1) * LANES]
        zero = jnp.zeros_like(blk)
        q_s[0, p * tq:(p + 1) * tq, :] = jnp.where(lo_half, blk, zero)
        q_s[1, p * tq:(p + 1) * tq, :] = jnp.where(lo_half, zero, blk)

    def att_body(j, carry, width):
        ks = pl.multiple_of(j * tk, tk)
        k = k_ref[0, pl.ds(ks, width), :]
        v = v_ref[0, pl.ds(ks, width), :]
        bias = bias_s[:, pl.ds(ks, width)]
        bias = jnp.concatenate([bias] * n_grp, axis=0)
        return tuple(_softmax_step(_dot_nt(q_s[half], k) + bias, v, *carry[half])
                     for half in range(2))

    rows = n_grp * tq
    init = (jnp.full((rows, 1), M_INIT, F32), jnp.zeros((rows, 1), F32),
            jnp.zeros((rows, LANES), F32))
    (_, l0, a0), (_, l1, a1) = wide_then_single(att_body, (init, init))
    o0, o1 = a0 / l0, a1 / l1
    for p in range(n_grp):
        o_ref[0, :, p * LANES:(p + 1) * LANES] = jnp.where(
            lo_half, o0[p * tq:(p + 1) * tq], o1[p * tq:(p + 1) * tq]).astype(BF16)


def _dsa_attention(q, qi, qi_cb, wi, wi_cb, k, k_cb, v, v_cb, ki, ki_cb,
                   q_pos0, l_valid, n_sel, tq, tk):
    b, t = q.shape[:2]
    l = k.shape[1]
    width = H_DSA * HEAD_DIM
    return pl.pallas_call(
        functools.partial(_dsa_body, tq=tq, tk=tk, q_pos0=q_pos0, l_valid=l_valid,
                          n_kb_total=l // tk, n_sel=n_sel),
        grid=(b, t // tq),
        in_specs=[pl.BlockSpec((1, tq, width), lambda bi, i: (bi, i, 0)),
                  pl.BlockSpec((1, tq, width), lambda bi, i: (bi, i, qi_cb)),
                  pl.BlockSpec((1, tq, LANES), lambda bi, i: (bi, i, wi_cb)),
                  pl.BlockSpec((1, l, LANES), lambda bi, i: (bi, 0, k_cb)),
                  pl.BlockSpec((1, l, LANES), lambda bi, i: (bi, 0, v_cb)),
                  pl.BlockSpec((1, l, LANES), lambda bi, i: (bi, 0, ki_cb))],
        out_specs=pl.BlockSpec((1, tq, width), lambda bi, i: (bi, i, 0)),
        out_shape=jax.ShapeDtypeStruct((b, t, width), BF16),
        scratch_shapes=[pltpu.VMEM((tq, l), I32), pltpu.VMEM((tq, l), F32),
                        pltpu.VMEM((2, IDX_HEADS // 2 * tq, LANES), BF16),
                        pltpu.VMEM((2, H_DSA // KV_DSA * tq, LANES), BF16)],
        compiler_params=_params(("parallel", "parallel")),
        name="dsa_attention",
    )(q, qi, wi, k, v, ki)


def _moe_body(h_ref, g_ref, wr_ref, w1_ref, w3_ref, w2_ref, o_ref,
              xn_s, xg_s, ye_s, rank_s, gate_s, rank_t_s, *, tm, ch, n_fc):
    e = pl.program_id(1)
    fc = pl.program_id(2)
    lane = lax.broadcasted_iota(I32, (1, LANES), 1)

    @pl.when((e == 0) & (fc == 0))
    def _route():
        x = h_ref[...]
        xn = _rms(x, g_ref[...])
        xn_s[...] = xn.astype(BF16)
        o_ref[...] = x
        x3 = _split3(xn)
        logits = jnp.zeros((tm, LANES), F32)
        for a, b in ((2, 0), (0, 2), (1, 1), (1, 0), (0, 1), (0, 0)):
            logits = logits + _dot(x3[a], wr_ref[b])
        lane_f = lane.astype(F32)
        logits = jnp.where(lane < N_EXPERTS, logits, -jnp.inf)
        m1 = jnp.max(logits, axis=1, keepdims=True)
        i1 = jnp.min(jnp.where(logits == m1, lane_f, float(LANES)), axis=1, keepdims=True)
        rest = jnp.where(lane_f == i1, -jnp.inf, logits)
        m2 = jnp.max(rest, axis=1, keepdims=True)
        i2 = jnp.min(jnp.where(rest == m2, lane_f, float(LANES)), axis=1, keepdims=True)
        e2 = jnp.exp(m2 - m1)
        g1 = 1.0 / (1.0 + e2)
        g2 = e2 / (1.0 + e2)
        sel1 = lane_f == i1
        sel2 = lane_f == i2
        gate_s[...] = jnp.where(sel1, g1, 0.0) + jnp.where(sel2, g2, 0.0)
        sel = jnp.where(sel1 | sel2, 1.0, 0.0)
        incl = (lax.broadcasted_iota(I32, (tm, tm), 1) <=
                lax.broadcasted_iota(I32, (tm, tm), 0)).astype(BF16)
        rank = _dot(incl, sel.astype(BF16)) * sel
        rank_s[...] = rank
        rank_t_s[...] = rank.T

    rank_row = rank_t_s[pl.ds(e, 1), :]
    cnt = jnp.max(rank_row).astype(I32)
    nch = (cnt + ch - 1) // ch

    @pl.when(fc == 0)
    def _gather():
        def gbody(c, _):
            base = pl.multiple_of(c * ch, ch)
            slot = (base + 1 + lax.broadcasted_iota(I32, (ch, 1), 0)).astype(F32)
            pick = jnp.where(rank_row == slot, 1.0, 0.0).astype(BF16)
            xg_s[pl.ds(base, ch), :] = _dot(pick, xn_s[...]).astype(BF16)
            return 0
        lax.fori_loop(0, nch, gbody, 0)

    def fbody(c, _):
        base = pl.multiple_of(c * ch, ch)
        xg = xg_s[pl.ds(base, ch), :]
        a = _dot(xg, w1_ref[0])
        b = _dot(xg, w3_ref[0])
        part = _dot((a * _sigmoid(a) * b).astype(BF16), w2_ref[0])

        @pl.when(fc == 0)
        def _():
            ye_s[pl.ds(base, ch), :] = part

        @pl.when(fc != 0)
        def _():
            ye_s[pl.ds(base, ch), :] += part
        return 0

    lax.fori_loop(0, nch, fbody, 0)

    @pl.when(fc == n_fc - 1)
    def _scatter():
        here = lane == e
        rank_col = jnp.sum(jnp.where(here, rank_s[...], 0.0), axis=1, keepdims=True)
        gate_col = jnp.sum(jnp.where(here, gate_s[...], 0.0), axis=1, keepdims=True)

        def sbody(c, _):
            base = pl.multiple_of(c * ch, ch)
            slot = (base + 1 + lax.broadcasted_iota(I32, (1, ch), 1)).astype(F32)
            place = jnp.where(rank_col == slot, 1.0, 0.0).astype(BF16)
            ye = ye_s[pl.ds(base, ch), :].astype(BF16)
            o_ref[...] += gate_col * _dot(place, ye)
            return 0
        lax.fori_loop(0, nch, sbody, 0)


def _moe(h, g, wr3, w1, w3, w2, tm, ch, tf):
    n, d = h.shape
    ne, _, f = w1.shape
    n_fc = f // tf
    return pl.pallas_call(
        functools.partial(_moe_body, tm=tm, ch=ch, n_fc=n_fc),
        grid=(n // tm, ne, n_fc),
        in_specs=[pl.BlockSpec((tm, d), lambda i, e, c: (i, 0), pipeline_mode=pl.Buffered(1)),
                  pl.BlockSpec((1, d), lambda i, e, c: (0, 0)),
                  pl.BlockSpec((3, d, LANES), lambda i, e, c: (0, 0, 0)),
                  pl.BlockSpec((1, d, tf), lambda i, e, c: (e, 0, c)),
                  pl.BlockSpec((1, d, tf), lambda i, e, c: (e, 0, c)),
                  pl.BlockSpec((1, tf, d), lambda i, e, c: (e, c, 0))],
        out_specs=pl.BlockSpec((tm, d), lambda i, e, c: (i, 0)),
        out_shape=jax.ShapeDtypeStruct((n, d), F32),
        scratch_shapes=[pltpu.VMEM((tm, d), BF16), pltpu.VMEM((_round_up(tm, ch), d), BF16),
                        pltpu.VMEM((_round_up(tm, ch), d), F32), pltpu.VMEM((tm, LANES), F32),
                        pltpu.VMEM((tm, LANES), F32), pltpu.VMEM((LANES, tm), F32)],
        compiler_params=_params(("parallel", "arbitrary", "arbitrary")),
        name="moe",
    )(h, g.reshape(1, d), wr3, w1, w3, w2)


def _round_up(x, m):
    return (x + m - 1) // m * m


def _pad_rows(a, l_pad):
    return jnp.pad(a, ((0, 0), (0, l_pad - a.shape[1]), (0, 0)))


def _prep_weights(g_q_dsa, g_k_dsa, g_q_fox, g_k_fox, w_in_even, w_out_even, w_in_odd, w_router):
    d = w_in_even.shape[0]
    perm = jnp.asarray(DSA_HEAD_PERM)
    qa, ka, va, qb, kb, vb, qi, ki, wi = jnp.split(
        w_in_even, [512, 1024, 1536, 2048, 2176, 2304, 2816, 2880], axis=1)
    qb = qb.reshape(d, H_DSA, HEAD_DIM)[:, perm].reshape(d, H_DSA * HEAD_DIM)
    wi = jnp.pad(wi, ((0, 0), (0, LANES - IDX_HEADS)))
    w_even = jnp.concatenate([qb, kb, vb, ki, ki, wi, qi, qa, ka, va], axis=1).astype(BF16)
    ones = lambda k: jnp.ones((k,), F32)
    q_scale = lambda k: jnp.full((k,), Q_SCALE, F32)
    scale_even = jnp.concatenate([ones(E_QA), q_scale(E_KA - E_QA), ones(E_END - E_KA)])
    scale_odd = ones(O_END)
    scale_norm_even = jnp.concatenate([q_scale(E_KB - E_QB), ones(E_VB - E_KB)])
    scale_norm_odd = jnp.concatenate([q_scale(O_K - O_Q), ones(O_V - O_K)])
    w_out_sb = w_out_even[:H_SB * HEAD_DIM].astype(BF16)
    w_out_dsa = w_out_even[H_SB * HEAD_DIM:].reshape(H_DSA, HEAD_DIM, d)[perm]
    w_out_dsa = w_out_dsa.reshape(H_DSA * HEAD_DIM, d).astype(BF16)
    w_odd = jnp.pad(w_in_odd, ((0, 0), (0, O_END - w_in_odd.shape[1]))).astype(BF16)
    gains_even = jnp.concatenate([jnp.tile(g_q_dsa, H_DSA), jnp.tile(g_k_dsa, KV_DSA)])
    gains_odd = jnp.concatenate([jnp.tile(g_q_fox, H_FOX), jnp.tile(g_k_fox, H_FOX)])
    wr = jnp.pad(w_router, ((0, 0), (0, LANES - N_EXPERTS)))
    wr_hi = wr.astype(BF16)
    wr_r1 = wr - wr_hi.astype(F32)
    wr_mid = wr_r1.astype(BF16)
    wr_lo = (wr_r1 - wr_mid.astype(F32)).astype(BF16)
    return dict(w_even=w_even, w_out_sb=w_out_sb, w_out_dsa=w_out_dsa, w_odd=w_odd,
                gains_even=gains_even, gains_odd=gains_odd,
                scale_even=scale_even, scale_odd=scale_odd,
                scale_norm_even=scale_norm_even, scale_norm_odd=scale_norm_odd,
                wr3=jnp.stack([wr_hi, wr_mid, wr_lo]))


def _rotary_tables(pos):
    half = ROT_DIM // 2
    inv = ROPE_THETA ** (-jnp.arange(half, dtype=F32) / half)
    ang = pos.astype(F32)[:, None] * inv[None, :]
    cos, sin = jnp.cos(ang), jnp.sin(ang)
    t = pos.shape[0]
    pad = HEAD_DIM - ROT_DIM
    cos_h = jnp.concatenate([cos, cos, jnp.ones((t, pad), F32)], axis=1)
    sin_h = jnp.concatenate([-sin, sin, jnp.zeros((t, pad), F32)], axis=1)
    return jnp.tile(cos_h, (1, 2)), jnp.tile(sin_h, (1, 2))


def _trunk(x, p, pos0, past_even, past_odd, w, wb):
    b, t, d = x.shape
    n = b * t
    tm = min(ROW_TILE, n)
    has_past = past_even is not None
    past_len = past_even[0].shape[1] if has_past else 0
    l_valid = past_len + t
    l_pad = _round_up(l_valid, KEY_PAD)
    n_sel = min(DSA_TOPK, l_valid // 4)
    h = x.reshape(n, d)

    def keys(cache, new):
        if not has_past:
            return new
        full = jnp.concatenate([cache.reshape(b, past_len, -1).astype(BF16), new], axis=1)
        return _pad_rows(full, l_pad)

    pf, pb = _proj(h, w["g_mix"][0], wb["w_even"], wb["scale_even"], tm)
    cos, sin = _rotary_tables(pos0 + jnp.arange(t, dtype=I32))
    if t % tm:
        cos, sin = jnp.tile(cos, (tm // t, 1)), jnp.tile(sin, (tm // t, 1))
    qkf, qkb = _headnorm(pf, E_QB, wb["gains_even"], wb["scale_norm_even"], cos, sin, tm)
    pf3, pb3 = pf.reshape(b, t, E_END), pb.reshape(b, t, E_END)
    qkf3, qkb3 = qkf.reshape(b, t, E_NORM), qkb.reshape(b, t, E_NORM)
    cols = lambda a, c0, width: a[:, :, c0:c0 + width]
    sb_w, kv_w = H_SB * HEAD_DIM, KV_DSA * HEAD_DIM
    if has_past:
        c_sbk, c_sbv, c_dk, c_dv, c_ki = past_even
        ka, ka_cb = keys(c_sbk, cols(pb3, E_KA, sb_w)), 0
        va, va_cb = keys(c_sbv, cols(pb3, E_VA, sb_w)), 0
        kb, kb_cb = keys(c_dk, cols(qkb3, E_KB, kv_w)), 0
        vb, vb_cb = keys(c_dv, cols(pb3, E_VB, kv_w)), 0
        ki2 = jnp.concatenate([c_ki, c_ki], axis=-1)
        ki, ki_cb = keys(ki2, cols(pb3, E_KI, LANES)), 0
    else:
        ka, ka_cb = pb3, E_KA // LANES
        va, va_cb = pb3, E_VA // LANES
        kb, kb_cb = qkb3, E_KB // LANES
        vb, vb_cb = pb3, E_VB // LANES
        ki, ki_cb = pb3, E_KI // LANES
    oa = _sb_attention(pb3, E_QA // LANES, ka, ka_cb, va, va_cb, H_SB // 2, pos0,
                       min(SB_Q_TILE, t), SB_K_TILE)
    ob = _dsa_attention(qkb3, pb3, E_QI // 512, pf3, E_WI // LANES, kb, kb_cb, vb, vb_cb,
                        ki, ki_cb, pos0, l_valid, n_sel, min(DSA_Q_TILE, t), DSA_K_TILE)
    h = _even_channel(h, oa.reshape(n, -1), ob.reshape(n, -1), wb["w_out_sb"], wb["w_out_dsa"],
                      w["g_ffn"][0], wb["w_ff1"], wb["w_ff3"], wb["w_ff2"],
                      w["g_ple"][0], wb["w_ple_gate"][0], p[0].reshape(n, -1), wb["w_ple_in"][0], tm)
    even_state = (cols(pf3, E_KA, sb_w).reshape(b, t, H_SB, HEAD_DIM),
                  cols(pf3, E_VA, sb_w).reshape(b, t, H_SB, HEAD_DIM),
                  cols(qkf3, E_KB, kv_w).reshape(b, t, KV_DSA, HEAD_DIM),
                  cols(pf3, E_VB, kv_w).reshape(b, t, KV_DSA, HEAD_DIM),
                  cols(pf3, E_KI, IDX_DIM))

    fox_w = H_FOX * HEAD_DIM
    pf, pb = _proj(h, w["g_mix"][1], wb["w_odd"], wb["scale_odd"], tm)
    qkf, qkb = _headnorm(pf, O_Q, wb["gains_odd"], wb["scale_norm_odd"], None, None, tm)
    pf3, pb3 = pf.reshape(b, t, O_END), pb.reshape(b, t, O_END)
    qkf3, qkb3 = qkf.reshape(b, t, O_V), qkb.reshape(b, t, O_V)
    gate_pre = cols(pf3, O_F, H_FOX)
    if has_past:
        c_fk, c_fv, c_lf = past_odd
        kf, kf_cb = keys(c_fk, cols(qkb3, O_K, fox_w)), 0
        vf, vf_cb = keys(c_fv, cols(pb3, O_V, fox_w)), 0
        raw = _pad_rows(jnp.concatenate([c_lf, gate_pre], axis=1), l_pad)
    else:
        kf, kf_cb = qkb3, O_K // LANES
        vf, vf_cb = pb3, O_V // LANES
        raw = gate_pre
    logf, cum = _forget_cumsum(raw, w["b_forget"][0], past_len, min(KEY_PAD, l_pad))
    logf, cum = logf[:, :, :H_FOX], cum[:, :, :H_FOX]
    fk = cum.reshape(b, l_pad, H_FOX // 2, 2).transpose(0, 2, 3, 1)

    normed_max = lambda g: HEAD_DIM ** 0.5 * jnp.max(jnp.abs(g))
    q_max = jnp.full((b, H_FOX), Q_SCALE * normed_max(w["g_q_fox"]), F32)
    if has_past:
        kx = kf.astype(F32).reshape(b, l_pad, H_FOX, HEAD_DIM)
        k_max = jnp.sqrt(jnp.max(jnp.sum(kx * kx, axis=-1), axis=1))
    else:
        k_max = jnp.full((b, H_FOX), normed_max(w["g_k_fox"]), F32)
    decay_cut = EXP2_ZERO + 2.0 * 1.02 * q_max * k_max
    fox_tq = min(FOX_Q_TILE, t)
    j0 = _fox_first_block(cum, decay_cut, pos0, t, fox_tq, FOX_K_TILE, FOX_PAIRS)
    of = _fox_attention(qkb3, O_Q // LANES, kf, kf_cb, vf, vf_cb, fk, j0, pos0, fox_tq, FOX_K_TILE)
    h = _residual_matmul(h, [of.reshape(n, -1)], [wb["w_out_odd"]], tm)
    h = _moe(h, w["g_ffn"][1], wb["wr3"], wb["w_exp1"], wb["w_exp3"], wb["w_exp2"],
             min(MOE_TILE, n), min(MOE_CHUNK, n), MOE_F_TILE)
    h = _ple(h, w["g_ple"][1], wb["w_ple_gate"][1], p[1].reshape(n, -1), wb["w_ple_in"][1], tm)
    odd_state = (cols(qkf3, O_K, fox_w).reshape(b, t, H_FOX, HEAD_DIM),
                 cols(pf3, O_V, fox_w).reshape(b, t, H_FOX, HEAD_DIM),
                 logf[:, past_len:l_valid])
    return h.reshape(b, t, d), even_state, odd_state


def kernel(x_prompt, x_sample, p_prompt, p_sample, cache_sb_k, cache_sb_v, cache_dsa_k, cache_dsa_v, cache_dsa_kidx, cache_fox_k, cache_fox_v, cache_fox_logf, g_mix, g_ffn, g_ple, w_in_even, g_q_dsa, g_k_dsa, w_out_even, w_ff1, w_ff3, w_ff2, w_in_odd, b_forget, g_q_fox, g_k_fox, w_out_odd, w_router, w_exp1, w_exp3, w_exp2, w_ple_in, w_ple_gate):
    assert g_mix.shape[0] == 2, "two layers: one even (stick-breaking + DSA), one odd (FoX + experts)"
    past_len = cache_sb_k.shape[2]
    w = dict(g_mix=g_mix, g_ffn=g_ffn, g_ple=g_ple, b_forget=b_forget,
             g_q_fox=g_q_fox[0], g_k_fox=g_k_fox[0])
    wb = _prep_weights(g_q_dsa[0], g_k_dsa[0], g_q_fox[0], g_k_fox[0],
                       w_in_even[0], w_out_even[0], w_in_odd[0], w_router[0])
    wb.update(w_ff1=w_ff1[0].astype(BF16), w_ff3=w_ff3[0].astype(BF16), w_ff2=w_ff2[0].astype(BF16),
              w_out_odd=w_out_odd[0].astype(BF16),
              w_exp1=w_exp1[0].astype(BF16), w_exp3=w_exp3[0].astype(BF16),
              w_exp2=w_exp2[0].astype(BF16),
              w_ple_in=w_ple_in.astype(BF16), w_ple_gate=w_ple_gate.astype(BF16))

    y_p, even_p, odd_p = _trunk(x_prompt, p_prompt, 0, None, None, w, wb)
    y_s, even_s, odd_s = _trunk(
        x_sample, p_sample, past_len,
        (cache_sb_k[0], cache_sb_v[0], cache_dsa_k[0], cache_dsa_v[0], cache_dsa_kidx[0]),
        (cache_fox_k[0], cache_fox_v[0], cache_fox_logf[0]), w, wb)
    tail = lambda a: a[:, -past_len:][None]
    whole = lambda a: a[None]
    return (y_p, y_s,
            *(tail(a) for a in even_p), *(tail(a) for a in odd_p),
            *(whole(a) for a in even_s), *(whole(a) for a in odd_s))
```

```python
import functools

import jax
import jax.numpy as jnp
from jax import lax
from jax.experimental import pallas as pl
from jax.experimental.pallas import tpu as pltpu

F32 = jnp.float32
BF16 = jnp.bfloat16
I32 = jnp.int32

EPS = 1e-6
HEAD_DIM = 64
CHUNK = 64
H_SB = 8
H_DSA = 8
KV_DSA = 2
IDX_HEADS = 8
IDX_DIM = 64
H_FOX = 16
DSA_TOPK = 256
ROT_DIM = HEAD_DIM // 4
ROPE_THETA = 500000.0
N_EXPERTS = 8
ATT_SCALE = HEAD_DIM ** -0.5
LOG2E = 1.4426950408889634
Q_SCALE = ATT_SCALE * LOG2E

LANES = 128
MIB = 1024 * 1024
VMEM_LIMIT = 56 * MIB

ROW_TILE = 512
SB_Q_TILE, SB_K_TILE = 256, 256
SB_BLOCKS_PER_STEP = 1
EXP2_ZERO = 150.0
FOX_Q_TILE, FOX_K_TILE = 512, 512
FOX_PAIRS = 1
DSA_Q_TILE, DSA_K_TILE = 256, 512
COUNT_ROWS = 128
SCORE_SUB = 256
KEY_PAD = 512
MOE_TILE = 1024
MOE_CHUNK = 304
MOE_F_TILE = 1792

NEG_BIG = -1e30
M_INIT = -1e29
INT_MIN = -(2 ** 31)
INT_MAX = 2 ** 31 - 1
KEY_NEG_INF = -2139095041

E_QB, E_KB, E_VB, E_KI, E_WI, E_QI, E_QA, E_KA, E_VA, E_END = (
    0, 512, 640, 768, 896, 1024, 1536, 2048, 2560, 3072)
E_NORM = E_VB - E_QB
O_Q, O_K, O_V, O_F, O_END = 0, 1024, 2048, 3072, 3200
DSA_HEAD_PERM = (0, 4, 1, 5, 2, 6, 3, 7)


def _params(sem, vmem=VMEM_LIMIT):
    return pltpu.CompilerParams(dimension_semantics=sem, vmem_limit_bytes=vmem)


def _dot(a, b):
    return jnp.dot(a, b, preferred_element_type=F32)


def _dot_nt(a, b):
    return lax.dot_general(a, b, (((1,), (1,)), ((), ())), preferred_element_type=F32)


def _rms(x, g):
    return x * lax.rsqrt(jnp.mean(x * x, axis=-1, keepdims=True) + EPS) * g


def _split3(x):
    hi = x.astype(BF16)
    r1 = x - hi.astype(F32)
    mid = r1.astype(BF16)
    lo = (r1 - mid.astype(F32)).astype(BF16)
    return hi, mid, lo


def _sigmoid(x):
    return 1.0 / (1.0 + jnp.exp(-x))


def _log_sigmoid_neg(z):
    return -(jnp.maximum(z, 0.0) + jnp.log(1.0 + jnp.exp(-jnp.abs(z))))


def _proj_body(x_ref, g_ref, w_ref, sc_ref, gn_ref, scn_ref, cos_ref, sin_ref, s_ref,
               of_ref, ob_ref, nf_ref, nb_ref, *, rotary, width, keep_from):
    y = _rms(x_ref[...], g_ref[...]).astype(BF16)
    r = _dot(y, w_ref[...])
    of_ref[...] = r[:, keep_from:]
    ob_ref[...] = (r[:, keep_from:] * sc_ref[:, keep_from:]).astype(BF16)
    s = s_ref[...]
    for j in range(width // LANES):
        cols = slice(j * LANES, (j + 1) * LANES)
        x = r[:, cols]
        hi, mid, lo = _split3(x * x)
        ms = (_dot(hi, s) + _dot(mid, s) + _dot(lo, s)) * (1.0 / HEAD_DIM)
        y = x * lax.rsqrt(ms + EPS) * gn_ref[:, cols]
        if rotary:
            lane = lax.broadcasted_iota(I32, (1, LANES), 1) % HEAD_DIM
            partner = jnp.where(lane < ROT_DIM // 2,
                                pltpu.roll(y, LANES - ROT_DIM // 2, 1),
                                pltpu.roll(y, ROT_DIM // 2, 1))
            y = y * cos_ref[...] + partner * sin_ref[...]
        nf_ref[:, cols] = y
        nb_ref[:, cols] = (y * scn_ref[:, cols]).astype(BF16)


def _proj(x, g, w, scales, gains_n, scales_n, cos, sin, keep_from, tm):
    n, d = x.shape
    c = w.shape[1]
    width = gains_n.shape[0]
    kept = c - keep_from
    rotary = cos is not None
    if not rotary:
        cos = jnp.zeros((8, LANES), F32)
        sin = cos
        tab_spec = pl.BlockSpec((8, LANES), lambda i: (0, 0))
    else:
        nt = cos.shape[0] // tm
        tab_spec = pl.BlockSpec((tm, LANES), lambda i: (i % nt, 0))
    rr = lax.broadcasted_iota(I32, (LANES, LANES), 0) // HEAD_DIM
    cc = lax.broadcasted_iota(I32, (LANES, LANES), 1) // HEAD_DIM
    seg = (rr == cc).astype(BF16)
    row = lambda width_: pl.BlockSpec((tm, width_), lambda i: (i, 0))
    vec = lambda width_: pl.BlockSpec((1, width_), lambda i: (0, 0))
    return pl.pallas_call(
        functools.partial(_proj_body, rotary=rotary, width=width, keep_from=keep_from),
        grid=(n // tm,),
        in_specs=[row(d), vec(d),
                  pl.BlockSpec((d, c), lambda i: (0, 0), pipeline_mode=pl.Buffered(1)),
                  vec(c), vec(width), vec(width), tab_spec, tab_spec,
                  pl.BlockSpec((LANES, LANES), lambda i: (0, 0))],
        out_specs=[row(kept), row(kept), row(width), row(width)],
        out_shape=[jax.ShapeDtypeStruct((n, kept), F32), jax.ShapeDtypeStruct((n, kept), BF16),
                   jax.ShapeDtypeStruct((n, width), F32), jax.ShapeDtypeStruct((n, width), BF16)],
        compiler_params=_params(("parallel",)),
        name="proj",
    )(x, g.reshape(1, d), w, scales.reshape(1, c), gains_n.reshape(1, width),
      scales_n.reshape(1, width), cos, sin, seg)


def _res_body(*refs, n_in):
    h_ref, o_ref = refs[0], refs[-1]
    acc = h_ref[...]
    for t in range(n_in):
        acc = acc + _dot(refs[1 + t][...], refs[1 + n_in + t][...])
    o_ref[...] = acc


def _residual_matmul(h, acts, ws, tm):
    n, d = h.shape
    n_in = len(acts)
    in_specs = [pl.BlockSpec((tm, d), lambda i: (i, 0))]
    in_specs += [pl.BlockSpec((tm, a.shape[1]), lambda i: (i, 0)) for a in acts]
    in_specs += [pl.BlockSpec(w.shape, lambda i: (0, 0)) for w in ws]
    return pl.pallas_call(
        functools.partial(_res_body, n_in=n_in),
        grid=(n // tm,),
        in_specs=in_specs,
        out_specs=pl.BlockSpec((tm, d), lambda i: (i, 0)),
        out_shape=jax.ShapeDtypeStruct((n, d), F32),
        compiler_params=_params(("parallel",)),
        name="residual_matmul",
    )(h, *acts, *ws)


def _ple_body(h_ref, g_ref, wg_ref, p_ref, wp_ref, o_ref):
    x = h_ref[...]
    xn = _rms(x, g_ref[...]).astype(BF16)
    gate = _sigmoid(_dot(xn, wg_ref[...]))
    o_ref[...] = x + gate * _dot(p_ref[...].astype(BF16), wp_ref[...])


def _ple(h, g, wg, p, wp, tm):
    n, d = h.shape
    e = p.shape[1]
    return pl.pallas_call(
        _ple_body,
        grid=(n // tm,),
        in_specs=[pl.BlockSpec((tm, d), lambda i: (i, 0)),
                  pl.BlockSpec((1, d), lambda i: (0, 0)),
                  pl.BlockSpec((d, d), lambda i: (0, 0)),
                  pl.BlockSpec((tm, e), lambda i: (i, 0)),
                  pl.BlockSpec((e, d), lambda i: (0, 0))],
        out_specs=pl.BlockSpec((tm, d), lambda i: (i, 0)),
        out_shape=jax.ShapeDtypeStruct((n, d), F32),
        compiler_params=_params(("parallel",)),
        name="ple",
    )(h, g.reshape(1, d), wg, p, wp)


def _even_channel_body(h_ref, oa_ref, ob_ref, wa_ref, wb_ref, gf_ref, w1_ref, w3_ref, w2_ref,
                       gp_ref, wg_ref, p_ref, wp_ref, o_ref):
    x = h_ref[...] + _dot(oa_ref[...], wa_ref[...]) + _dot(ob_ref[...], wb_ref[...])
    xn = _rms(x, gf_ref[...]).astype(BF16)
    a = _dot(xn, w1_ref[...])
    b = _dot(xn, w3_ref[...])
    x = x + _dot((a * _sigmoid(a) * b).astype(BF16), w2_ref[...])
    xn = _rms(x, gp_ref[...]).astype(BF16)
    gate = _sigmoid(_dot(xn, wg_ref[...]))
    o_ref[...] = x + gate * _dot(p_ref[...].astype(BF16), wp_ref[...])


def _even_channel(h, oa, ob, wa, wb, gf, w1, w3, w2, gp, wg, p, wp, tm):
    n, d = h.shape
    once = pl.Buffered(1)
    row = lambda a: pl.BlockSpec((tm, a.shape[1]), lambda i: (i, 0))
    vec = pl.BlockSpec((1, d), lambda i: (0, 0))
    full = lambda a: pl.BlockSpec(a.shape, lambda i: (0, 0), pipeline_mode=once)
    return pl.pallas_call(
        _even_channel_body,
        grid=(n // tm,),
        in_specs=[row(h), row(oa), row(ob), full(wa), full(wb), vec, full(w1), full(w3), full(w2),
                  vec, full(wg), row(p), full(wp)],
        out_specs=row(h),
        out_shape=jax.ShapeDtypeStruct((n, d), F32),
        compiler_params=_params(("parallel",)),
        name="even_channel",
    )(h, oa, ob, wa, wb, gf.reshape(1, d), w1, w3, w2, gp.reshape(1, d), wg, p, wp)


def _sb_body(q_ref, k_ref, v_ref, later_ref, o_ref, *, tq, tk, q_pos0, n_kb_total):
    row0 = q_pos0 + pl.program_id(2) * tq
    q = q_ref[0]
    lane = lax.broadcasted_iota(I32, (1, LANES), 1)
    lo_half = lane < HEAD_DIM
    zero = jnp.zeros_like(q)
    q_halves = (jnp.where(lo_half, q, zero), jnp.where(lo_half, zero, q))
    qpos = row0 + lax.broadcasted_iota(I32, (tq, 1), 0)
    nkb = jnp.minimum((row0 + tq - 1 + tk - 1) // tk, n_kb_total)
    n_full = jnp.minimum(row0 // tk, nkb)
    later = later_ref[...]

    def block(half, kb, c, masked):
        ks = pl.multiple_of(kb * tk, tk)
        z = _dot_nt(q_halves[half], k_ref[0, pl.ds(ks, tk), :])
        sp = jnp.maximum(z, 0.0) + jnp.log2(1.0 + jnp.exp2(-jnp.abs(z)))
        if masked:
            vis = (ks + lax.broadcasted_iota(I32, (1, tk), 1)) < qpos
            sp = jnp.where(vis, sp, 0.0)
        between = _dot(sp.astype(BF16), later)
        w = jnp.exp2(z - sp - between - c)
        if masked:
            w = jnp.where(vis, w, 0.0)
        row_sum = between[:, 0:1] + sp[:, 0:1]
        return _dot(w.astype(BF16), v_ref[0, pl.ds(ks, tk), :]), row_sum

    def run(kbs, carry, masked):
        accs, cs = carry
        new_accs, new_cs = [], []
        for half in range(2):
            acc, c = accs[half], cs[half]
            for kb in kbs:
                pv, rs = block(half, kb, c, masked)
                acc, c = acc + pv, c + rs
            new_accs.append(acc)
            new_cs.append(c)
        return tuple(new_accs), tuple(new_cs)

    acc0 = jnp.zeros((tq, LANES), F32)
    c0 = jnp.zeros((tq, 1), F32)
    carry = ((acc0, acc0), (c0, c0))
    carry = lax.fori_loop(0, nkb - n_full,
                          lambda j, cr: run([nkb - 1 - j], cr, True), carry)
    def live(cr):
        return jnp.minimum(jnp.min(cr[1][0]), jnp.min(cr[1][1])) < EXP2_ZERO

    def steps(n_steps, kbs_of, carry):
        def cond(state):
            j, _, more = state
            return (j < n_steps) & more

        def body(state):
            j, cr, _ = state
            cr = run(kbs_of(j), cr, False)
            return j + 1, cr, live(cr)

        return lax.while_loop(cond, body, (0, carry, live(carry)))[1]

    grp = SB_BLOCKS_PER_STEP
    carry = steps(n_full // grp, lambda j: [n_full - 1 - grp * j - u for u in range(grp)], carry)
    rem = n_full % grp
    carry = steps(rem, lambda j: [rem - 1 - j], carry)
    accs, _ = carry
    o_ref[0] = jnp.where(lo_half, accs[0], accs[1]).astype(BF16)


def _sb_attention(q, q_cb, k, k_cb, v, v_cb, n_pairs, q_pos0, tq, tk):
    b, t = q.shape[:2]
    l = k.shape[1]
    later = (lax.broadcasted_iota(I32, (tk, tk), 0) >
             lax.broadcasted_iota(I32, (tk, tk), 1)).astype(BF16)
    return pl.pallas_call(
        functools.partial(_sb_body, tq=tq, tk=tk, q_pos0=q_pos0, n_kb_total=l // tk),
        grid=(b, n_pairs, t // tq),
        in_specs=[pl.BlockSpec((1, tq, LANES), lambda bi, p, i: (bi, i, q_cb + p)),
                  pl.BlockSpec((1, l, LANES), lambda bi, p, i: (bi, 0, k_cb + p)),
                  pl.BlockSpec((1, l, LANES), lambda bi, p, i: (bi, 0, v_cb + p)),
                  pl.BlockSpec((tk, tk), lambda bi, p, i: (0, 0))],
        out_specs=pl.BlockSpec((1, tq, LANES), lambda bi, p, i: (bi, i, p)),
        out_shape=jax.ShapeDtypeStruct((b, t, n_pairs * LANES), BF16),
        compiler_params=_params(("parallel", "parallel", "parallel")),
        name="sb_attention",
    )(q, k, v, later)


def _softmax_step(s, v, m, l, acc):
    m_new = jnp.maximum(m, jnp.max(s, axis=1, keepdims=True))
    alpha = jnp.exp2(m - m_new)
    p = jnp.exp2(s - m_new)
    l_new = alpha * l + jnp.sum(p, axis=1, keepdims=True)
    acc_new = alpha * acc + _dot(p.astype(BF16), v)
    return m_new, l_new, acc_new


def _fox_body(q_ref, k_ref, v_ref, fk_ref, j0_ref, o_ref, q_s,
              *, tq, tk, q_pos0, n_kb_total, npp):
    row0 = q_pos0 + pl.program_id(2) * tq
    lane = lax.broadcasted_iota(I32, (1, LANES), 1)
    lo_half = lane < HEAD_DIM
    for pp in range(npp):
        q = q_ref[0, :, pp * LANES:(pp + 1) * LANES]
        zero = jnp.zeros_like(q)
        q_s[2 * pp] = jnp.where(lo_half, q, zero)
        q_s[2 * pp + 1] = jnp.where(lo_half, zero, q)
    qpos = row0 + lax.broadcasted_iota(I32, (tq, 1), 0)
    nkb = jnp.minimum((row0 + tq + tk - 1) // tk, n_kb_total)
    n_full = jnp.minimum((row0 + 1) // tk, nkb)

    def body(j, carry, masked, width):
        ks = pl.multiple_of(j * tk, tk)
        out = []
        for h in range(2 * npp):
            pp, half = divmod(h, 2)
            k = k_ref[0, pl.ds(ks, width), pp * LANES:(pp + 1) * LANES]
            v = v_ref[0, pl.ds(ks, width), pp * LANES:(pp + 1) * LANES]
            m, l, acc = carry[h]
            fk = fk_ref[0, pp, half:half + 1, pl.ds(ks, width)] * LOG2E
            s = _dot_nt(q_s[h], k) - fk
            if masked:
                vis = (ks + lax.broadcasted_iota(I32, (1, width), 1)) <= qpos
                s = jnp.where(vis, s, NEG_BIG)
            out.append(_softmax_step(s, v, m, l, acc))
        return tuple(out)

    j0 = jnp.minimum(j0_ref[pl.program_id(0), pl.program_id(1), pl.program_id(2)], n_full)

    init = (jnp.full((tq, 1), M_INIT, F32), jnp.zeros((tq, 1), F32), jnp.zeros((tq, LANES), F32))
    n_wide = (n_full - j0) // 2
    carry = lax.fori_loop(0, n_wide,
                          lambda j, c: body(j0 + 2 * j, c, masked=False, width=2 * tk),
                          (init,) * (2 * npp))
    carry = lax.fori_loop(j0 + 2 * n_wide, n_full,
                          functools.partial(body, masked=False, width=tk), carry)
    carry = lax.fori_loop(n_full, nkb, functools.partial(body, masked=True, width=tk), carry)
    for pp in range(npp):
        (_, l0, a0), (_, l1, a1) = carry[2 * pp], carry[2 * pp + 1]
        o_ref[0, :, pp * LANES:(pp + 1) * LANES] = jnp.where(lo_half, a0 / l0, a1 / l1).astype(BF16)


def _fox_first_block(cum, decay_cut, q_pos0, t, tq, tk, npp):
    b, l, h = cum.shape
    f2 = cum * LOG2E
    f_tile = jnp.max(f2[:, q_pos0:q_pos0 + t].reshape(b, t // tq, tq, h), axis=2)
    gone = (f2[:, None] - f_tile[:, :, None]) > decay_cut[:, None, None, :]
    first_needed = jnp.min(jnp.where(gone, l, jnp.arange(l, dtype=I32)[None, None, :, None]), axis=2)
    j0 = jnp.min((first_needed // tk).reshape(b, t // tq, h // (2 * npp), 2 * npp), axis=-1)
    return j0.transpose(0, 2, 1).astype(I32)


def _fox_attention(q, q_cb, k, k_cb, v, v_cb, fk, j0, q_pos0, tq, tk):
    b, t = q.shape[:2]
    l = k.shape[1]
    n_pairs = H_FOX // 2
    npp = FOX_PAIRS
    w = npp * LANES
    assert q_cb % npp == 0 and k_cb % npp == 0 and v_cb % npp == 0
    return pl.pallas_call(
        functools.partial(_fox_body, tq=tq, tk=tk, q_pos0=q_pos0, n_kb_total=l // tk, npp=npp),
        grid=(b, n_pairs // npp, t // tq),
        in_specs=[pl.BlockSpec((1, tq, w), lambda bi, p, i: (bi, i, q_cb // npp + p)),
                  pl.BlockSpec((1, l, w), lambda bi, p, i: (bi, 0, k_cb // npp + p)),
                  pl.BlockSpec((1, l, w), lambda bi, p, i: (bi, 0, v_cb // npp + p)),
                  pl.BlockSpec((1, npp, 2, l), lambda bi, p, i: (bi, p, 0, 0)),
                  pl.BlockSpec(memory_space=pltpu.SMEM)],
        out_specs=pl.BlockSpec((1, tq, w), lambda bi, p, i: (bi, i, p)),
        out_shape=jax.ShapeDtypeStruct((b, t, n_pairs * LANES), BF16),
        scratch_shapes=[pltpu.VMEM((2 * npp, tq, LANES), BF16)],
        compiler_params=_params(("parallel", "parallel", "parallel")),
        name="fox_attention",
    )(q, k, v, fk, j0)


def _forget_body(raw_ref, b_ref, logf_ref, cum_ref, carry_ref, *, tb, n_given):
    j = pl.program_id(1)

    @pl.when(j == 0)
    def _():
        carry_ref[...] = jnp.zeros_like(carry_ref)

    raw = raw_ref[0]
    z = raw + b_ref[...]
    computed = _log_sigmoid_neg(-z)
    row = j * tb + lax.broadcasted_iota(I32, (tb, 1), 0)
    logf = jnp.where(row < n_given, raw, computed)
    incl = (lax.broadcasted_iota(I32, (tb, tb), 1) <=
            lax.broadcasted_iota(I32, (tb, tb), 0)).astype(BF16)
    hi, mid, lo = _split3(logf)
    cum = _dot(incl, hi) + _dot(incl, mid) + _dot(incl, lo) + carry_ref[0:1, :]
    logf_ref[0] = logf
    cum_ref[0] = cum
    carry_ref[0:1, :] = cum[tb - 1:tb, :]


def _forget_cumsum(raw, col_block, bias, n_given, tb):
    bias = jnp.pad(bias, (0, LANES - bias.shape[0]))
    b, l = raw.shape[:2]
    h = LANES
    return pl.pallas_call(
        functools.partial(_forget_body, tb=tb, n_given=n_given),
        grid=(b, l // tb),
        in_specs=[pl.BlockSpec((1, tb, h), lambda bi, j: (bi, j, col_block)),
                  pl.BlockSpec((1, h), lambda bi, j: (0, 0))],
        out_specs=[pl.BlockSpec((1, tb, h), lambda bi, j: (bi, j, 0)),
                   pl.BlockSpec((1, tb, h), lambda bi, j: (bi, j, 0))],
        out_shape=[jax.ShapeDtypeStruct((b, l, h), F32), jax.ShapeDtypeStruct((b, l, h), F32)],
        scratch_shapes=[pltpu.VMEM((8, h), F32)],
        compiler_params=_params(("parallel", "arbitrary")),
        name="forget_cumsum",
    )(raw, bias.reshape(1, h))


def _dsa_body(q_ref, qi_ref, wi_ref, k_ref, v_ref, ki_ref, o_ref,
              key_s, bias_s, qi_s, q_s, *, tq, tk, q_pos0, l_valid, n_kb_total, n_sel):
    row0 = q_pos0 + pl.program_id(1) * tq
    lane = lax.broadcasted_iota(I32, (1, LANES), 1)
    lo_half = lane < HEAD_DIM
    qpos = row0 + lax.broadcasted_iota(I32, (tq, 1), 0)
    qchunk = qpos // CHUNK
    kend = jnp.minimum(((row0 + tq - 1) // CHUNK + 1) * CHUNK, l_valid)
    nkb = jnp.minimum(jnp.maximum((kend + tk - 1) // tk, (n_sel + tk - 1) // tk), n_kb_total)

    n_stack = IDX_HEADS // 2
    for p in range(n_stack):
        blk = qi_ref[0, :, p * LANES:(p + 1) * LANES]
        zero = jnp.zeros_like(blk)
        qi_s[0, p * tq:(p + 1) * tq, :] = jnp.where(lo_half, blk, zero)
        qi_s[1, p * tq:(p + 1) * tq, :] = jnp.where(lo_half, zero, blk)
    wsc = wi_ref[0] * (IDX_DIM ** -0.5 * IDX_HEADS ** -0.5)

    def wide_then_single(step, init):
        carry = lax.fori_loop(0, nkb // 2, lambda j, c: step(2 * j, c, 2 * tk), init)
        return lax.fori_loop(nkb // 2 * 2, nkb, lambda j, c: step(j, c, tk), carry)

    def score_body(j, _, width):
        for u in range(width // SCORE_SUB):
            ks = pl.multiple_of(j * tk + u * SCORE_SUB, SCORE_SUB)
            ki = ki_ref[0, pl.ds(ks, SCORE_SUB), :]
            score = jnp.zeros((tq, SCORE_SUB), F32)
            for half in range(2):
                rel = jnp.maximum(_dot_nt(qi_s[half], ki), 0.0)
                for p in range(n_stack):
                    h = 2 * p + half
                    score = score + wsc[:, h:h + 1] * rel[p * tq:(p + 1) * tq]
            kpos = ks + lax.broadcasted_iota(I32, (1, SCORE_SUB), 1)
            vis = ((kpos // CHUNK) <= qchunk) & (kpos < l_valid)
            score = jnp.where(vis, score, -jnp.inf)
            bits = lax.bitcast_convert_type(score, I32)
            key = bits ^ ((bits >> 31) & INT_MAX)
            key_s[:, pl.ds(ks, SCORE_SUB)] = jnp.where(bits == INT_MIN, 0, key)
        return 0

    wide_then_single(score_body, 0)

    n_sel_f = float(n_sel)
    cr = min(COUNT_ROWS, tq)

    slabs = [slice(r * cr, (r + 1) * cr) for r in range(tq // cr)]

    def count(rs, pred):
        def cbody(j, acc):
            blk = key_s[rs, pl.ds(pl.multiple_of(j * tk, tk), tk)]
            hit = jnp.where(pred(blk, j), 1.0, 0.0)
            for c in range(tk // LANES):
                acc = acc + hit[:, c * LANES:(c + 1) * LANES]
            return acc
        acc = lax.fori_loop(0, nkb, cbody, jnp.zeros((cr, LANES), F32))
        return jnp.sum(acc, axis=1, keepdims=True)

    c_zero = [count(rs, lambda blk, j: blk >= 0) for rs in slabs]
    thrs = tuple(jnp.where(c >= n_sel_f, 0, INT_MIN) for c in c_zero)
    c_ges = tuple(jnp.where(c >= n_sel_f, c, (nkb * tk).astype(F32)) for c in c_zero)

    def unsettled(c_ges):
        open_rows = functools.reduce(jnp.maximum, [jnp.where(c == n_sel_f, 0.0, 1.0) for c in c_ges])
        return jnp.max(open_rows) > 0.0

    def try_bit(bit, thrs, c_ges):
        new_thrs, new_c_ges = [], []
        for rs, thr, c_ge in zip(slabs, thrs, c_ges):
            cand = thr | (1 << bit)
            c = count(rs, lambda blk, j, cand=cand: blk >= cand)
            keep = c >= n_sel_f
            new_thrs.append(jnp.where(keep, cand, thr))
            new_c_ges.append(jnp.where(keep, c, c_ge))
        return tuple(new_thrs), tuple(new_c_ges)

    def bit_cond(state):
        bit, _, _, more = state
        return (bit >= 0) & more

    def bit_body(state):
        bit, thrs, c_ges, _ = state
        thrs, c_ges = try_bit(bit, thrs, c_ges)
        thrs, c_ges = try_bit(bit - 1, thrs, c_ges)
        return bit - 2, thrs, c_ges, unsettled(c_ges)

    thrs, c_ges = try_bit(30, thrs, c_ges)
    _, thrs, c_ges, _ = lax.while_loop(bit_cond, bit_body, (29, thrs, c_ges, unsettled(c_ges)))

    for rs, thr, c_ge in zip(slabs, thrs, c_ges):
        tied = (c_ge > n_sel_f) & (thr > KEY_NEG_INF)
        any_tied = jnp.max(jnp.where(tied, 1.0, 0.0)) > 0.0

        def tie_index_bound(rs=rs, thr=thr, tied=tied):
            need = n_sel_f - count(rs, lambda blk, j: blk > thr)

            def kidx(j):
                return j * tk + lax.broadcasted_iota(I32, (1, tk), 1)

            n_bits = (n_kb_total * tk - 1).bit_length()

            def jbody(b, jb):
                cand = jb | (1 << (n_bits - 1 - b))
                cnt = count(rs, lambda blk, j: (blk == thr) & (kidx(j) < cand))
                return jnp.where(cnt < need, cand, jb)

            jb = lax.fori_loop(0, n_bits, jbody, jnp.zeros((cr, 1), I32))
            return jnp.where(tied, jb, INT_MAX)

        jbound = lax.cond(any_tied, tie_index_bound, lambda: jnp.full((cr, 1), INT_MAX, I32))

        def bias_body(j, _, rs=rs, thr=thr, jbound=jbound):
            ks = pl.multiple_of(j * tk, tk)
            blk = key_s[rs, pl.ds(ks, tk)]
            kpos = ks + lax.broadcasted_iota(I32, (1, tk), 1)
            sel = (blk > thr) | ((blk == thr) & (kpos <= jbound))
            sel = sel & (blk > KEY_NEG_INF)
            bias_s[rs, pl.ds(ks, tk)] = jnp.where(sel, 0.0, NEG_BIG)
            return 0

        lax.fori_loop(0, nkb, bias_body, 0)

    n_grp = H_DSA // KV_DSA
    for p in range(n_grp):
        blk = q_ref[0, :, p * LANES:(p + 1) * LANES]
        zero = jnp.zeros_like(blk)
        q_s[0, p * tq:(p + 1) * tq, :] = jnp.where(lo_half, blk, zero)
        q_s[1, p * tq:(p + 1) * tq, :] = jnp.where(lo_half, zero, blk)

    def att_body(j, carry, width):
        ks = pl.multiple_of(j * tk, tk)
        k = k_ref[0, pl.ds(ks, width), :]
        v = v_ref[0, pl.ds(ks, width), :]
        bias = bias_s[:, pl.ds(ks, width)]
        bias = jnp.concatenate([bias] * n_grp, axis=0)
        return tuple(_softmax_step(_dot_nt(q_s[half], k) + bias, v, *carry[half])
                     for half in range(2))

    rows = n_grp * tq
    init = (jnp.full((rows, 1), M_INIT, F32), jnp.zeros((rows, 1), F32),
            jnp.zeros((rows, LANES), F32))
    (_, l0, a0), (_, l1, a1) = wide_then_single(att_body, (init, init))
    o0, o1 = a0 / l0, a1 / l1
    for p in range(n_grp):
        o_ref[0, :, p * LANES:(p + 1) * LANES] = jnp.where(
            lo_half, o0[p * tq:(p + 1) * tq], o1[p * tq:(p + 1) * tq]).astype(BF16)


def _dsa_attention(q, qi, qi_cb, wi, wi_cb, k, k_cb, v, v_cb, ki, ki_cb,
                   q_pos0, l_valid, n_sel, tq, tk):
    b, t = q.shape[:2]
    l = k.shape[1]
    width = H_DSA * HEAD_DIM
    return pl.pallas_call(
        functools.partial(_dsa_body, tq=tq, tk=tk, q_pos0=q_pos0, l_valid=l_valid,
                          n_kb_total=l // tk, n_sel=n_sel),
        grid=(b, t // tq),
        in_specs=[pl.BlockSpec((1, tq, width), lambda bi, i: (bi, i, 0)),
                  pl.BlockSpec((1, tq, width), lambda bi, i: (bi, i, qi_cb)),
                  pl.BlockSpec((1, tq, LANES), lambda bi, i: (bi, i, wi_cb)),
                  pl.BlockSpec((1, l, LANES), lambda bi, i: (bi, 0, k_cb)),
                  pl.BlockSpec((1, l, LANES), lambda bi, i: (bi, 0, v_cb)),
                  pl.BlockSpec((1, l, LANES), lambda bi, i: (bi, 0, ki_cb))],
        out_specs=pl.BlockSpec((1, tq, width), lambda bi, i: (bi, i, 0)),
        out_shape=jax.ShapeDtypeStruct((b, t, width), BF16),
        scratch_shapes=[pltpu.VMEM((tq, l), I32), pltpu.VMEM((tq, l), F32),
                        pltpu.VMEM((2, IDX_HEADS // 2 * tq, LANES), BF16),
                        pltpu.VMEM((2, H_DSA // KV_DSA * tq, LANES), BF16)],
        compiler_params=_params(("parallel", "parallel")),
        name="dsa_attention",
    )(q, qi, wi, k, v, ki)


def _moe_body(h_ref, g_ref, wr_ref, w1_ref, w3_ref, w2_ref, o_ref,
              xn_s, xg_s, ye_s, rank_s, gate_s, rank_t_s, *, tm, ch, n_fc):
    e = pl.program_id(1)
    fc = pl.program_id(2)
    lane = lax.broadcasted_iota(I32, (1, LANES), 1)

    @pl.when((e == 0) & (fc == 0))
    def _route():
        x = h_ref[...]
        xn = _rms(x, g_ref[...])
        xn_s[...] = xn.astype(BF16)
        o_ref[...] = x
        x3 = _split3(xn)
        logits = jnp.zeros((tm, LANES), F32)
        for a, b in ((2, 0), (0, 2), (1, 1), (1, 0), (0, 1), (0, 0)):
            logits = logits + _dot(x3[a], wr_ref[b])
        lane_f = lane.astype(F32)
        logits = jnp.where(lane < N_EXPERTS, logits, -jnp.inf)
        m1 = jnp.max(logits, axis=1, keepdims=True)
        i1 = jnp.min(jnp.where(logits == m1, lane_f, float(LANES)), axis=1, keepdims=True)
        rest = jnp.where(lane_f == i1, -jnp.inf, logits)
        m2 = jnp.max(rest, axis=1, keepdims=True)
        i2 = jnp.min(jnp.where(rest == m2, lane_f, float(LANES)), axis=1, keepdims=True)
        e2 = jnp.exp(m2 - m1)
        g1 = 1.0 / (1.0 + e2)
        g2 = e2 / (1.0 + e2)
        sel1 = lane_f == i1
        sel2 = lane_f == i2
        gate_s[...] = jnp.where(sel1, g1, 0.0) + jnp.where(sel2, g2, 0.0)
        sel = jnp.where(sel1 | sel2, 1.0, 0.0)
        incl = (lax.broadcasted_iota(I32, (tm, tm), 1) <=
                lax.broadcasted_iota(I32, (tm, tm), 0)).astype(BF16)
        rank = _dot(incl, sel.astype(BF16)) * sel
        rank_s[...] = rank
        rank_t_s[...] = rank.T

    rank_row = rank_t_s[pl.ds(e, 1), :]
    cnt = jnp.max(rank_row).astype(I32)
    nch = (cnt + ch - 1) // ch

    @pl.when(fc == 0)
    def _gather():
        def gbody(c, _):
            base = pl.multiple_of(c * ch, ch)
            slot = (base + 1 + lax.broadcasted_iota(I32, (ch, 1), 0)).astype(F32)
            pick = jnp.where(rank_row == slot, 1.0, 0.0).astype(BF16)
            xg_s[pl.ds(base, ch), :] = _dot(pick, xn_s[...]).astype(BF16)
            return 0
        lax.fori_loop(0, nch, gbody, 0)

    def fbody(c, _):
        base = pl.multiple_of(c * ch, ch)
        xg = xg_s[pl.ds(base, ch), :]
        a = _dot(xg, w1_ref[0])
        b = _dot(xg, w3_ref[0])
        part = _dot((a * _sigmoid(a) * b).astype(BF16), w2_ref[0])

        @pl.when(fc == 0)
        def _():
            ye_s[pl.ds(base, ch), :] = part

        @pl.when(fc != 0)
        def _():
            ye_s[pl.ds(base, ch), :] += part
        return 0

    lax.fori_loop(0, nch, fbody, 0)

    @pl.when(fc == n_fc - 1)
    def _scatter():
        here = lane == e
        rank_col = jnp.sum(jnp.where(here, rank_s[...], 0.0), axis=1, keepdims=True)
        gate_col = jnp.sum(jnp.where(here, gate_s[...], 0.0), axis=1, keepdims=True)

        def sbody(c, _):
            base = pl.multiple_of(c * ch, ch)
            slot = (base + 1 + lax.broadcasted_iota(I32, (1, ch), 1)).astype(F32)
            place = jnp.where(rank_col == slot, 1.0, 0.0).astype(BF16)
            ye = ye_s[pl.ds(base, ch), :].astype(BF16)
            o_ref[...] += gate_col * _dot(place, ye)
            return 0
        lax.fori_loop(0, nch, sbody, 0)


def _moe(h, g, wr3, w1, w3, w2, tm, ch, tf):
    n, d = h.shape
    ne, _, f = w1.shape
    n_fc = f // tf
    return pl.pallas_call(
        functools.partial(_moe_body, tm=tm, ch=ch, n_fc=n_fc),
        grid=(n // tm, ne, n_fc),
        in_specs=[pl.BlockSpec((tm, d), lambda i, e, c: (i, 0), pipeline_mode=pl.Buffered(1)),
                  pl.BlockSpec((1, d), lambda i, e, c: (0, 0)),
                  pl.BlockSpec((3, d, LANES), lambda i, e, c: (0, 0, 0)),
                  pl.BlockSpec((1, d, tf), lambda i, e, c: (e, 0, c)),
                  pl.BlockSpec((1, d, tf), lambda i, e, c: (e, 0, c)),
                  pl.BlockSpec((1, tf, d), lambda i, e, c: (e, c, 0))],
        out_specs=pl.BlockSpec((tm, d), lambda i, e, c: (i, 0)),
        out_shape=jax.ShapeDtypeStruct((n, d), F32),
        scratch_shapes=[pltpu.VMEM((tm, d), BF16), pltpu.VMEM((_round_up(tm, ch), d), BF16),
                        pltpu.VMEM((_round_up(tm, ch), d), F32), pltpu.VMEM((tm, LANES), F32),
                        pltpu.VMEM((tm, LANES), F32), pltpu.VMEM((LANES, tm), F32)],
        compiler_params=_params(("parallel", "arbitrary", "arbitrary")),
        name="moe",
    )(h, g.reshape(1, d), wr3, w1, w3, w2)


def _round_up(x, m):
    return (x + m - 1) // m * m


def _pad_rows(a, l_pad):
    return jnp.pad(a, ((0, 0), (0, l_pad - a.shape[1]), (0, 0)))


def _prep_weights(g_q_dsa, g_k_dsa, g_q_fox, g_k_fox, w_in_even, w_out_even, w_in_odd, w_router):
    d = w_in_even.shape[0]
    perm = jnp.asarray(DSA_HEAD_PERM)
    qa, ka, va, qb, kb, vb, qi, ki, wi = jnp.split(
        w_in_even, [512, 1024, 1536, 2048, 2176, 2304, 2816, 2880], axis=1)
    qb = qb.reshape(d, H_DSA, HEAD_DIM)[:, perm].reshape(d, H_DSA * HEAD_DIM)
    wi = jnp.pad(wi, ((0, 0), (0, LANES - IDX_HEADS)))
    w_even = jnp.concatenate([qb, kb, vb, ki, ki, wi, qi, qa, ka, va], axis=1).astype(BF16)
    ones = lambda k: jnp.ones((k,), F32)
    q_scale = lambda k: jnp.full((k,), Q_SCALE, F32)
    scale_even = jnp.concatenate([ones(E_QA), q_scale(E_KA - E_QA), ones(E_END - E_KA)])
    scale_odd = ones(O_END)
    scale_norm_even = jnp.concatenate([q_scale(E_KB - E_QB), ones(E_VB - E_KB)])
    scale_norm_odd = jnp.concatenate([q_scale(O_K - O_Q), ones(O_V - O_K)])
    w_out_sb = w_out_even[:H_SB * HEAD_DIM].astype(BF16)
    w_out_dsa = w_out_even[H_SB * HEAD_DIM:].reshape(H_DSA, HEAD_DIM, d)[perm]
    w_out_dsa = w_out_dsa.reshape(H_DSA * HEAD_DIM, d).astype(BF16)
    w_odd = jnp.pad(w_in_odd, ((0, 0), (0, O_END - w_in_odd.shape[1]))).astype(BF16)
    gains_even = jnp.concatenate([jnp.tile(g_q_dsa, H_DSA), jnp.tile(g_k_dsa, KV_DSA)])
    gains_odd = jnp.concatenate([jnp.tile(g_q_fox, H_FOX), jnp.tile(g_k_fox, H_FOX)])
    wr = jnp.pad(w_router, ((0, 0), (0, LANES - N_EXPERTS)))
    wr_hi = wr.astype(BF16)
    wr_r1 = wr - wr_hi.astype(F32)
    wr_mid = wr_r1.astype(BF16)
    wr_lo = (wr_r1 - wr_mid.astype(F32)).astype(BF16)
    return dict(w_even=w_even, w_out_sb=w_out_sb, w_out_dsa=w_out_dsa, w_odd=w_odd,
                gains_even=gains_even, gains_odd=gains_odd,
                scale_even=scale_even, scale_odd=scale_odd,
                scale_norm_even=scale_norm_even, scale_norm_odd=scale_norm_odd,
                wr3=jnp.stack([wr_hi, wr_mid, wr_lo]))


def _rotary_tables(pos):
    half = ROT_DIM // 2
    inv = ROPE_THETA ** (-jnp.arange(half, dtype=F32) / half)
    ang = pos.astype(F32)[:, None] * inv[None, :]
    cos, sin = jnp.cos(ang), jnp.sin(ang)
    t = pos.shape[0]
    pad = HEAD_DIM - ROT_DIM
    cos_h = jnp.concatenate([cos, cos, jnp.ones((t, pad), F32)], axis=1)
    sin_h = jnp.concatenate([-sin, sin, jnp.zeros((t, pad), F32)], axis=1)
    return jnp.tile(cos_h, (1, 2)), jnp.tile(sin_h, (1, 2))


def _trunk(x, p, pos0, past_even, past_odd, w, wb):
    b, t, d = x.shape
    n = b * t
    tm = min(ROW_TILE, n)
    has_past = past_even is not None
    past_len = past_even[0].shape[1] if has_past else 0
    l_valid = past_len + t
    l_pad = _round_up(l_valid, KEY_PAD)
    n_sel = min(DSA_TOPK, l_valid // 4)
    h = x.reshape(n, d)

    def keys(cache, new):
        if not has_past:
            return new
        full = jnp.concatenate([cache.reshape(b, past_len, -1).astype(BF16), new], axis=1)
        return _pad_rows(full, l_pad)

    cos, sin = _rotary_tables(pos0 + jnp.arange(t, dtype=I32))
    if t % tm:
        cos, sin = jnp.tile(cos, (tm // t, 1)), jnp.tile(sin, (tm // t, 1))
    pf, pb, qkf, qkb = _proj(h, w["g_mix"][0], wb["w_even"], wb["scale_even"],
                             wb["gains_even"], wb["scale_norm_even"], cos, sin, 0, tm)
    pf3, pb3 = pf.reshape(b, t, E_END), pb.reshape(b, t, E_END)
    qkf3, qkb3 = qkf.reshape(b, t, E_NORM), qkb.reshape(b, t, E_NORM)
    cols = lambda a, c0, width: a[:, :, c0:c0 + width]
    sb_w, kv_w = H_SB * HEAD_DIM, KV_DSA * HEAD_DIM
    if has_past:
        c_sbk, c_sbv, c_dk, c_dv, c_ki = past_even
        ka, ka_cb = keys(c_sbk, cols(pb3, E_KA, sb_w)), 0
        va, va_cb = keys(c_sbv, cols(pb3, E_VA, sb_w)), 0
        kb, kb_cb = keys(c_dk, cols(qkb3, E_KB, kv_w)), 0
        vb, vb_cb = keys(c_dv, cols(pb3, E_VB, kv_w)), 0
        ki2 = jnp.concatenate([c_ki, c_ki], axis=-1)
        ki, ki_cb = keys(ki2, cols(pb3, E_KI, LANES)), 0
    else:
        ka, ka_cb = pb3, E_KA // LANES
        va, va_cb = pb3, E_VA // LANES
        kb, kb_cb = qkb3, E_KB // LANES
        vb, vb_cb = pb3, E_VB // LANES
        ki, ki_cb = pb3, E_KI // LANES
    oa = _sb_attention(pb3, E_QA // LANES, ka, ka_cb, va, va_cb, H_SB // 2, pos0,
                       min(SB_Q_TILE, t), SB_K_TILE)
    ob = _dsa_attention(qkb3, pb3, E_QI // 512, pf3, E_WI // LANES, kb, kb_cb, vb, vb_cb,
                        ki, ki_cb, pos0, l_valid, n_sel, min(DSA_Q_TILE, t), DSA_K_TILE)
    h = _even_channel(h, oa.reshape(n, -1), ob.reshape(n, -1), wb["w_out_sb"], wb["w_out_dsa"],
                      w["g_ffn"][0], wb["w_ff1"], wb["w_ff3"], wb["w_ff2"],
                      w["g_ple"][0], wb["w_ple_gate"][0], p[0].reshape(n, -1), wb["w_ple_in"][0], tm)
    even_state = (cols(pf3, E_KA, sb_w).reshape(b, t, H_SB, HEAD_DIM),
                  cols(pf3, E_VA, sb_w).reshape(b, t, H_SB, HEAD_DIM),
                  cols(qkf3, E_KB, kv_w).reshape(b, t, KV_DSA, HEAD_DIM),
                  cols(pf3, E_VB, kv_w).reshape(b, t, KV_DSA, HEAD_DIM),
                  cols(pf3, E_KI, IDX_DIM))

    fox_w = H_FOX * HEAD_DIM
    pf, pb, qkf, qkb = _proj(h, w["g_mix"][1], wb["w_odd"], wb["scale_odd"],
                             wb["gains_odd"], wb["scale_norm_odd"], None, None, O_V, tm)
    r_f = O_F - O_V
    pf3, pb3 = pf.reshape(b, t, O_END - O_V), pb.reshape(b, t, O_END - O_V)
    qkf3, qkb3 = qkf.reshape(b, t, O_V), qkb.reshape(b, t, O_V)
    if has_past:
        c_fk, c_fv, c_lf = past_odd
        kf, kf_cb = keys(c_fk, cols(qkb3, O_K, fox_w)), 0
        vf, vf_cb = keys(c_fv, cols(pb3, 0, fox_w)), 0
        raw = _pad_rows(jnp.concatenate([c_lf, cols(pf3, r_f, H_FOX)], axis=1), l_pad)
        raw, raw_cb = jnp.pad(raw, ((0, 0), (0, 0), (0, LANES - H_FOX))), 0
    else:
        kf, kf_cb = qkb3, O_K // LANES
        vf, vf_cb = pb3, 0
        raw, raw_cb = pf3, r_f // LANES
    logf, cum = _forget_cumsum(raw, raw_cb, w["b_forget"][0], past_len, min(KEY_PAD, l_pad))
    logf, cum = logf[:, :, :H_FOX], cum[:, :, :H_FOX]
    fk = cum.reshape(b, l_pad, H_FOX // 2, 2).transpose(0, 2, 3, 1)

    normed_max = lambda g: HEAD_DIM ** 0.5 * jnp.max(jnp.abs(g))
    q_max = jnp.full((b, H_FOX), Q_SCALE * normed_max(w["g_q_fox"]), F32)
    if has_past:
        kx = kf.astype(F32).reshape(b, l_pad, H_FOX, HEAD_DIM)
        k_max = jnp.sqrt(jnp.max(jnp.sum(kx * kx, axis=-1), axis=1))
    else:
        k_max = jnp.full((b, H_FOX), normed_max(w["g_k_fox"]), F32)
    decay_cut = EXP2_ZERO + 2.0 * 1.02 * q_max * k_max
    fox_tq = min(FOX_Q_TILE, t)
    j0 = _fox_first_block(cum, decay_cut, pos0, t, fox_tq, FOX_K_TILE, FOX_PAIRS)
    of = _fox_attention(qkb3, O_Q // LANES, kf, kf_cb, vf, vf_cb, fk, j0, pos0, fox_tq, FOX_K_TILE)
    h = _residual_matmul(h, [of.reshape(n, -1)], [wb["w_out_odd"]], tm)
    h = _moe(h, w["g_ffn"][1], wb["wr3"], wb["w_exp1"], wb["w_exp3"], wb["w_exp2"],
             min(MOE_TILE, n), min(MOE_CHUNK, n), MOE_F_TILE)
    h = _ple(h, w["g_ple"][1], wb["w_ple_gate"][1], p[1].reshape(n, -1), wb["w_ple_in"][1], tm)
    odd_state = (cols(qkf3, O_K, fox_w).reshape(b, t, H_FOX, HEAD_DIM),
                 cols(pf3, 0, fox_w).reshape(b, t, H_FOX, HEAD_DIM),
                 logf[:, past_len:l_valid])
    return h.reshape(b, t, d), even_state, odd_state


def kernel(x_prompt, x_sample, p_prompt, p_sample, cache_sb_k, cache_sb_v, cache_dsa_k, cache_dsa_v, cache_dsa_kidx, cache_fox_k, cache_fox_v, cache_fox_logf, g_mix, g_ffn, g_ple, w_in_even, g_q_dsa, g_k_dsa, w_out_even, w_ff1, w_ff3, w_ff2, w_in_odd, b_forget, g_q_fox, g_k_fox, w_out_odd, w_router, w_exp1, w_exp3, w_exp2, w_ple_in, w_ple_gate):
    assert g_mix.shape[0] == 2, "two layers: one even (stick-breaking + DSA), one odd (FoX + experts)"
    past_len = cache_sb_k.shape[2]
    w = dict(g_mix=g_mix, g_ffn=g_ffn, g_ple=g_ple, b_forget=b_forget,
             g_q_fox=g_q_fox[0], g_k_fox=g_k_fox[0])
    wb = _prep_weights(g_q_dsa[0], g_k_dsa[0], g_q_fox[0], g_k_fox[0],
                       w_in_even[0], w_out_even[0], w_in_odd[0], w_router[0])
    wb.update(w_ff1=w_ff1[0].astype(BF16), w_ff3=w_ff3[0].astype(BF16), w_ff2=w_ff2[0].astype(BF16),
              w_out_odd=w_out_odd[0].astype(BF16),
              w_exp1=w_exp1[0].astype(BF16), w_exp3=w_exp3[0].astype(BF16),
              w_exp2=w_exp2[0].astype(BF16),
              w_ple_in=w_ple_in.astype(BF16), w_ple_gate=w_ple_gate.astype(BF16))

    y_p, even_p, odd_p = _trunk(x_prompt, p_prompt, 0, None, None, w, wb)
    y_s, even_s, odd_s = _trunk(
        x_sample, p_sample, past_len,
        (cache_sb_k[0], cache_sb_v[0], cache_dsa_k[0], cache_dsa_v[0], cache_dsa_kidx[0]),
        (cache_fox_k[0], cache_fox_v[0], cache_fox_logf[0]), w, wb)
    tail = lambda a: a[:, -past_len:][None]
    whole = lambda a: a[None]
    return (y_p, y_s,
            *(tail(a) for a in even_p), *(tail(a) for a in odd_p),
            *(whole(a) for a in even_s), *(whole(a) for a in odd_s))
```

```python
import functools

import jax
import jax.numpy as jnp
from jax import lax
from jax.experimental import pallas as pl
from jax.experimental.pallas import tpu as pltpu

F32 = jnp.float32
BF16 = jnp.bfloat16
I32 = jnp.int32

EPS = 1e-6
HEAD_DIM = 64
CHUNK = 64
H_SB = 8
H_DSA = 8
KV_DSA = 2
IDX_HEADS = 8
IDX_DIM = 64
H_FOX = 16
DSA_TOPK = 256
ROT_DIM = HEAD_DIM // 4
ROPE_THETA = 500000.0
N_EXPERTS = 8
ATT_SCALE = HEAD_DIM ** -0.5
LOG2E = 1.4426950408889634
Q_SCALE = ATT_SCALE * LOG2E

LANES = 128
MIB = 1024 * 1024
VMEM_LIMIT = 56 * MIB

ROW_TILE = 512
SB_Q_TILE, SB_K_TILE = 256, 256
SB_BLOCKS_PER_STEP = 1
EXP2_ZERO = 150.0
FOX_Q_TILE, FOX_K_TILE = 512, 512
FOX_PAIRS = 1
DSA_Q_TILE, DSA_K_TILE = 256, 512
COUNT_ROWS = 128
SCORE_SUB = 256
BIT_GROUP = 32 * LANES
KEY_PAD = BIT_GROUP
FORGET_TILE = 512
MOE_TILE = 1024
MOE_CHUNK = 288
MOE_F_TILE = 1792

NEG_BIG = -1e30
M_INIT = -1e29
INT_MIN = -(2 ** 31)
INT_MAX = 2 ** 31 - 1
KEY_NEG_INF = -2139095041

E_QB, E_KB, E_VB, E_KI, E_WI, E_QI, E_QA, E_KA, E_VA, E_END = (
    0, 512, 640, 768, 896, 1024, 1536, 2048, 2560, 3072)
E_NORM = E_VB - E_QB
O_Q, O_K, O_V, O_F, O_END = 0, 1024, 2048, 3072, 3200
DSA_HEAD_PERM = (0, 4, 1, 5, 2, 6, 3, 7)


def _params(sem, vmem=VMEM_LIMIT):
    return pltpu.CompilerParams(dimension_semantics=sem, vmem_limit_bytes=vmem)


def _dot(a, b):
    return jnp.dot(a, b, preferred_element_type=F32)


def _dot_nt(a, b):
    return lax.dot_general(a, b, (((1,), (1,)), ((), ())), preferred_element_type=F32)


def _rms(x, g):
    return x * lax.rsqrt(jnp.mean(x * x, axis=-1, keepdims=True) + EPS) * g


def _split3(x):
    hi = x.astype(BF16)
    r1 = x - hi.astype(F32)
    mid = r1.astype(BF16)
    lo = (r1 - mid.astype(F32)).astype(BF16)
    return hi, mid, lo


def _sigmoid(x):
    return 1.0 / (1.0 + jnp.exp(-x))


def _log_sigmoid_neg(z):
    return -(jnp.maximum(z, 0.0) + jnp.log(1.0 + jnp.exp(-jnp.abs(z))))


def _proj_body(x_ref, g_ref, w_ref, sc_ref, gn_ref, scn_ref, cos_ref, sin_ref, s_ref,
               of_ref, ob_ref, nf_ref, nb_ref, *, rotary, width, keep_from):
    y = _rms(x_ref[...], g_ref[...]).astype(BF16)
    r = _dot(y, w_ref[...])
    of_ref[...] = r[:, keep_from:]
    ob_ref[...] = (r[:, keep_from:] * sc_ref[:, keep_from:]).astype(BF16)
    s = s_ref[...]
    for j in range(width // LANES):
        cols = slice(j * LANES, (j + 1) * LANES)
        x = r[:, cols]
        hi, mid, lo = _split3(x * x)
        ms = (_dot(hi, s) + _dot(mid, s) + _dot(lo, s)) * (1.0 / HEAD_DIM)
        y = x * lax.rsqrt(ms + EPS) * gn_ref[:, cols]
        if rotary:
            lane = lax.broadcasted_iota(I32, (1, LANES), 1) % HEAD_DIM
            partner = jnp.where(lane < ROT_DIM // 2,
                                pltpu.roll(y, LANES - ROT_DIM // 2, 1),
                                pltpu.roll(y, ROT_DIM // 2, 1))
            y = y * cos_ref[...] + partner * sin_ref[...]
        nf_ref[:, cols] = y
        nb_ref[:, cols] = (y * scn_ref[:, cols]).astype(BF16)


def _proj(x, g, w, scales, gains_n, scales_n, cos, sin, keep_from, tm):
    n, d = x.shape
    c = w.shape[1]
    width = gains_n.shape[0]
    kept = c - keep_from
    rotary = cos is not None
    if not rotary:
        cos = jnp.zeros((8, LANES), F32)
        sin = cos
        tab_spec = pl.BlockSpec((8, LANES), lambda i: (0, 0))
    else:
        nt = cos.shape[0] // tm
        tab_spec = pl.BlockSpec((tm, LANES), lambda i: (i % nt, 0))
    rr = lax.broadcasted_iota(I32, (LANES, LANES), 0) // HEAD_DIM
    cc = lax.broadcasted_iota(I32, (LANES, LANES), 1) // HEAD_DIM
    seg = (rr == cc).astype(BF16)
    row = lambda width_: pl.BlockSpec((tm, width_), lambda i: (i, 0))
    vec = lambda width_: pl.BlockSpec((1, width_), lambda i: (0, 0))
    return pl.pallas_call(
        functools.partial(_proj_body, rotary=rotary, width=width, keep_from=keep_from),
        grid=(n // tm,),
        in_specs=[row(d), vec(d),
                  pl.BlockSpec((d, c), lambda i: (0, 0), pipeline_mode=pl.Buffered(1)),
                  vec(c), vec(width), vec(width), tab_spec, tab_spec,
                  pl.BlockSpec((LANES, LANES), lambda i: (0, 0))],
        out_specs=[row(kept), row(kept), row(width), row(width)],
        out_shape=[jax.ShapeDtypeStruct((n, kept), F32), jax.ShapeDtypeStruct((n, kept), BF16),
                   jax.ShapeDtypeStruct((n, width), F32), jax.ShapeDtypeStruct((n, width), BF16)],
        compiler_params=_params(("parallel",)),
        name="proj",
    )(x, g.reshape(1, d), w, scales.reshape(1, c), gains_n.reshape(1, width),
      scales_n.reshape(1, width), cos, sin, seg)


def _res_body(*refs, n_in):
    h_ref, o_ref = refs[0], refs[-1]
    acc = h_ref[...]
    for t in range(n_in):
        acc = acc + _dot(refs[1 + t][...], refs[1 + n_in + t][...])
    o_ref[...] = acc


def _residual_matmul(h, acts, ws, tm):
    n, d = h.shape
    n_in = len(acts)
    in_specs = [pl.BlockSpec((tm, d), lambda i: (i, 0))]
    in_specs += [pl.BlockSpec((tm, a.shape[1]), lambda i: (i, 0)) for a in acts]
    in_specs += [pl.BlockSpec(w.shape, lambda i: (0, 0)) for w in ws]
    return pl.pallas_call(
        functools.partial(_res_body, n_in=n_in),
        grid=(n // tm,),
        in_specs=in_specs,
        out_specs=pl.BlockSpec((tm, d), lambda i: (i, 0)),
        out_shape=jax.ShapeDtypeStruct((n, d), F32),
        compiler_params=_params(("parallel",)),
        name="residual_matmul",
    )(h, *acts, *ws)


def _ple_body(h_ref, g_ref, wg_ref, p_ref, wp_ref, o_ref):
    x = h_ref[...]
    xn = _rms(x, g_ref[...]).astype(BF16)
    gate = _sigmoid(_dot(xn, wg_ref[...]))
    o_ref[...] = x + gate * _dot(p_ref[...].astype(BF16), wp_ref[...])


def _ple(h, g, wg, p, wp, tm):
    n, d = h.shape
    e = p.shape[1]
    return pl.pallas_call(
        _ple_body,
        grid=(n // tm,),
        in_specs=[pl.BlockSpec((tm, d), lambda i: (i, 0)),
                  pl.BlockSpec((1, d), lambda i: (0, 0)),
                  pl.BlockSpec((d, d), lambda i: (0, 0)),
                  pl.BlockSpec((tm, e), lambda i: (i, 0)),
                  pl.BlockSpec((e, d), lambda i: (0, 0))],
        out_specs=pl.BlockSpec((tm, d), lambda i: (i, 0)),
        out_shape=jax.ShapeDtypeStruct((n, d), F32),
        compiler_params=_params(("parallel",)),
        name="ple",
    )(h, g.reshape(1, d), wg, p, wp)


def _even_channel_body(h_ref, oa_ref, ob_ref, wa_ref, wb_ref, gf_ref, w1_ref, w3_ref, w2_ref,
                       gp_ref, wg_ref, p_ref, wp_ref, o_ref):
    x = h_ref[...] + _dot(oa_ref[...], wa_ref[...]) + _dot(ob_ref[...], wb_ref[...])
    xn = _rms(x, gf_ref[...]).astype(BF16)
    a = _dot(xn, w1_ref[...])
    b = _dot(xn, w3_ref[...])
    x = x + _dot((a * _sigmoid(a) * b).astype(BF16), w2_ref[...])
    xn = _rms(x, gp_ref[...]).astype(BF16)
    gate = _sigmoid(_dot(xn, wg_ref[...]))
    o_ref[...] = x + gate * _dot(p_ref[...].astype(BF16), wp_ref[...])


def _even_channel(h, oa, ob, wa, wb, gf, w1, w3, w2, gp, wg, p, wp, tm):
    n, d = h.shape
    once = pl.Buffered(1)
    row = lambda a: pl.BlockSpec((tm, a.shape[1]), lambda i: (i, 0))
    vec = pl.BlockSpec((1, d), lambda i: (0, 0))
    full = lambda a: pl.BlockSpec(a.shape, lambda i: (0, 0), pipeline_mode=once)
    return pl.pallas_call(
        _even_channel_body,
        grid=(n // tm,),
        in_specs=[row(h), row(oa), row(ob), full(wa), full(wb), vec, full(w1), full(w3), full(w2),
                  vec, full(wg), row(p), full(wp)],
        out_specs=row(h),
        out_shape=jax.ShapeDtypeStruct((n, d), F32),
        compiler_params=_params(("parallel",)),
        name="even_channel",
    )(h, oa, ob, wa, wb, gf.reshape(1, d), w1, w3, w2, gp.reshape(1, d), wg, p, wp)


def _sb_body(q_ref, k_ref, v_ref, later_ref, o_ref, *, tq, tk, q_pos0, n_kb_total):
    row0 = q_pos0 + pl.program_id(2) * tq
    q = q_ref[0]
    lane = lax.broadcasted_iota(I32, (1, LANES), 1)
    lo_half = lane < HEAD_DIM
    zero = jnp.zeros_like(q)
    q_halves = (jnp.where(lo_half, q, zero), jnp.where(lo_half, zero, q))
    qpos = row0 + lax.broadcasted_iota(I32, (tq, 1), 0)
    nkb = jnp.minimum((row0 + tq - 1 + tk - 1) // tk, n_kb_total)
    n_full = jnp.minimum(row0 // tk, nkb)
    later = later_ref[...]

    def block(half, kb, c, masked):
        ks = pl.multiple_of(kb * tk, tk)
        z = _dot_nt(q_halves[half], k_ref[0, pl.ds(ks, tk), :])
        sp = jnp.maximum(z, 0.0) + jnp.log2(1.0 + jnp.exp2(-jnp.abs(z)))
        if masked:
            vis = (ks + lax.broadcasted_iota(I32, (1, tk), 1)) < qpos
            sp = jnp.where(vis, sp, 0.0)
        between = _dot(sp.astype(BF16), later)
        w = jnp.exp2(z - sp - between - c)
        if masked:
            w = jnp.where(vis, w, 0.0)
        row_sum = between[:, 0:1] + sp[:, 0:1]
        return _dot(w.astype(BF16), v_ref[0, pl.ds(ks, tk), :]), row_sum

    def run(kbs, carry, masked):
        accs, cs = carry
        new_accs, new_cs = [], []
        for half in range(2):
            acc, c = accs[half], cs[half]
            for kb in kbs:
                pv, rs = block(half, kb, c, masked)
                acc, c = acc + pv, c + rs
            new_accs.append(acc)
            new_cs.append(c)
        return tuple(new_accs), tuple(new_cs)

    acc0 = jnp.zeros((tq, LANES), F32)
    c0 = jnp.zeros((tq, 1), F32)
    carry = ((acc0, acc0), (c0, c0))
    carry = lax.fori_loop(0, nkb - n_full,
                          lambda j, cr: run([nkb - 1 - j], cr, True), carry)
    def live(cr):
        return jnp.minimum(jnp.min(cr[1][0]), jnp.min(cr[1][1])) < EXP2_ZERO

    def steps(n_steps, kbs_of, carry):
        def cond(state):
            j, _, more = state
            return (j < n_steps) & more

        def body(state):
            j, cr, _ = state
            cr = run(kbs_of(j), cr, False)
            return j + 1, cr, live(cr)

        return lax.while_loop(cond, body, (0, carry, live(carry)))[1]

    grp = SB_BLOCKS_PER_STEP
    carry = steps(n_full // grp, lambda j: [n_full - 1 - grp * j - u for u in range(grp)], carry)
    rem = n_full % grp
    carry = steps(rem, lambda j: [rem - 1 - j], carry)
    accs, _ = carry
    o_ref[0] = jnp.where(lo_half, accs[0], accs[1]).astype(BF16)


def _sb_attention(q, q_cb, k, k_cb, v, v_cb, n_pairs, q_pos0, tq, tk):
    b, t = q.shape[:2]
    l = k.shape[1]
    later = (lax.broadcasted_iota(I32, (tk, tk), 0) >
             lax.broadcasted_iota(I32, (tk, tk), 1)).astype(BF16)
    return pl.pallas_call(
        functools.partial(_sb_body, tq=tq, tk=tk, q_pos0=q_pos0, n_kb_total=l // tk),
        grid=(b, n_pairs, t // tq),
        in_specs=[pl.BlockSpec((1, tq, LANES), lambda bi, p, i: (bi, i, q_cb + p)),
                  pl.BlockSpec((1, l, LANES), lambda bi, p, i: (bi, 0, k_cb + p)),
                  pl.BlockSpec((1, l, LANES), lambda bi, p, i: (bi, 0, v_cb + p)),
                  pl.BlockSpec((tk, tk), lambda bi, p, i: (0, 0))],
        out_specs=pl.BlockSpec((1, tq, LANES), lambda bi, p, i: (bi, i, p)),
        out_shape=jax.ShapeDtypeStruct((b, t, n_pairs * LANES), BF16),
        compiler_params=_params(("parallel", "parallel", "parallel")),
        name="sb_attention",
    )(q, k, v, later)


def _softmax_step(s, v, m, l, acc):
    m_new = jnp.maximum(m, jnp.max(s, axis=1, keepdims=True))
    alpha = jnp.exp2(m - m_new)
    p = jnp.exp2(s - m_new)
    l_new = alpha * l + jnp.sum(p, axis=1, keepdims=True)
    acc_new = alpha * acc + _dot(p.astype(BF16), v)
    return m_new, l_new, acc_new


def _fox_body(q_ref, k_ref, v_ref, fk_ref, j0_ref, o_ref, q_s,
              *, tq, tk, q_pos0, n_kb_total, npp):
    row0 = q_pos0 + pl.program_id(2) * tq
    lane = lax.broadcasted_iota(I32, (1, LANES), 1)
    lo_half = lane < HEAD_DIM
    for pp in range(npp):
        q = q_ref[0, :, pp * LANES:(pp + 1) * LANES]
        zero = jnp.zeros_like(q)
        q_s[2 * pp] = jnp.where(lo_half, q, zero)
        q_s[2 * pp + 1] = jnp.where(lo_half, zero, q)
    qpos = row0 + lax.broadcasted_iota(I32, (tq, 1), 0)
    nkb = jnp.minimum((row0 + tq + tk - 1) // tk, n_kb_total)
    n_full = jnp.minimum((row0 + 1) // tk, nkb)

    def body(j, carry, masked, width):
        ks = pl.multiple_of(j * tk, tk)
        out = []
        for h in range(2 * npp):
            pp, half = divmod(h, 2)
            k = k_ref[0, pl.ds(ks, width), pp * LANES:(pp + 1) * LANES]
            v = v_ref[0, pl.ds(ks, width), pp * LANES:(pp + 1) * LANES]
            m, l, acc = carry[h]
            fk = fk_ref[0, pp, half:half + 1, pl.ds(ks, width)] * LOG2E
            s = _dot_nt(q_s[h], k) - fk
            if masked:
                vis = (ks + lax.broadcasted_iota(I32, (1, width), 1)) <= qpos
                s = jnp.where(vis, s, NEG_BIG)
            out.append(_softmax_step(s, v, m, l, acc))
        return tuple(out)

    j0 = jnp.minimum(j0_ref[pl.program_id(0), pl.program_id(1), pl.program_id(2)], n_full)

    init = (jnp.full((tq, 1), M_INIT, F32), jnp.zeros((tq, 1), F32), jnp.zeros((tq, LANES), F32))
    n_wide = (n_full - j0) // 2
    carry = lax.fori_loop(0, n_wide,
                          lambda j, c: body(j0 + 2 * j, c, masked=False, width=2 * tk),
                          (init,) * (2 * npp))
    carry = lax.fori_loop(j0 + 2 * n_wide, n_full,
                          functools.partial(body, masked=False, width=tk), carry)
    carry = lax.fori_loop(n_full, nkb, functools.partial(body, masked=True, width=tk), carry)
    for pp in range(npp):
        (_, l0, a0), (_, l1, a1) = carry[2 * pp], carry[2 * pp + 1]
        o_ref[0, :, pp * LANES:(pp + 1) * LANES] = jnp.where(lo_half, a0 / l0, a1 / l1).astype(BF16)


def _fox_first_block(cum, decay_cut, q_pos0, t, tq, tk, npp):
    b, l, h = cum.shape
    f2 = cum * LOG2E
    f_tile = jnp.max(f2[:, q_pos0:q_pos0 + t].reshape(b, t // tq, tq, h), axis=2)
    gone = (f2[:, None] - f_tile[:, :, None]) > decay_cut[:, None, None, :]
    first_needed = jnp.min(jnp.where(gone, l, jnp.arange(l, dtype=I32)[None, None, :, None]), axis=2)
    j0 = jnp.min((first_needed // tk).reshape(b, t // tq, h // (2 * npp), 2 * npp), axis=-1)
    return j0.transpose(0, 2, 1).astype(I32)


def _fox_attention(q, q_cb, k, k_cb, v, v_cb, fk, j0, q_pos0, tq, tk):
    b, t = q.shape[:2]
    l = k.shape[1]
    n_pairs = H_FOX // 2
    npp = FOX_PAIRS
    w = npp * LANES
    assert q_cb % npp == 0 and k_cb % npp == 0 and v_cb % npp == 0
    return pl.pallas_call(
        functools.partial(_fox_body, tq=tq, tk=tk, q_pos0=q_pos0, n_kb_total=l // tk, npp=npp),
        grid=(b, n_pairs // npp, t // tq),
        in_specs=[pl.BlockSpec((1, tq, w), lambda bi, p, i: (bi, i, q_cb // npp + p)),
                  pl.BlockSpec((1, l, w), lambda bi, p, i: (bi, 0, k_cb // npp + p)),
                  pl.BlockSpec((1, l, w), lambda bi, p, i: (bi, 0, v_cb // npp + p)),
                  pl.BlockSpec((1, npp, 2, l), lambda bi, p, i: (bi, p, 0, 0)),
                  pl.BlockSpec(memory_space=pltpu.SMEM)],
        out_specs=pl.BlockSpec((1, tq, w), lambda bi, p, i: (bi, i, p)),
        out_shape=jax.ShapeDtypeStruct((b, t, n_pairs * LANES), BF16),
        scratch_shapes=[pltpu.VMEM((2 * npp, tq, LANES), BF16)],
        compiler_params=_params(("parallel", "parallel", "parallel")),
        name="fox_attention",
    )(q, k, v, fk, j0)


def _forget_body(raw_ref, b_ref, logf_ref, cum_ref, carry_ref, *, tb, n_given):
    j = pl.program_id(1)

    @pl.when(j == 0)
    def _():
        carry_ref[...] = jnp.zeros_like(carry_ref)

    raw = raw_ref[0]
    z = raw + b_ref[...]
    computed = _log_sigmoid_neg(-z)
    row = j * tb + lax.broadcasted_iota(I32, (tb, 1), 0)
    logf = jnp.where(row < n_given, raw, computed)
    incl = (lax.broadcasted_iota(I32, (tb, tb), 1) <=
            lax.broadcasted_iota(I32, (tb, tb), 0)).astype(BF16)
    hi, mid, lo = _split3(logf)
    cum = _dot(incl, hi) + _dot(incl, mid) + _dot(incl, lo) + carry_ref[0:1, :]
    logf_ref[0] = logf
    cum_ref[0] = cum
    carry_ref[0:1, :] = cum[tb - 1:tb, :]


def _forget_cumsum(raw, col_block, bias, n_given, tb):
    bias = jnp.pad(bias, (0, LANES - bias.shape[0]))
    b, l = raw.shape[:2]
    h = LANES
    return pl.pallas_call(
        functools.partial(_forget_body, tb=tb, n_given=n_given),
        grid=(b, l // tb),
        in_specs=[pl.BlockSpec((1, tb, h), lambda bi, j: (bi, j, col_block)),
                  pl.BlockSpec((1, h), lambda bi, j: (0, 0))],
        out_specs=[pl.BlockSpec((1, tb, h), lambda bi, j: (bi, j, 0)),
                   pl.BlockSpec((1, tb, h), lambda bi, j: (bi, j, 0))],
        out_shape=[jax.ShapeDtypeStruct((b, l, h), F32), jax.ShapeDtypeStruct((b, l, h), F32)],
        scratch_shapes=[pltpu.VMEM((8, h), F32)],
        compiler_params=_params(("parallel", "arbitrary")),
        name="forget_cumsum",
    )(raw, bias.reshape(1, h))


def _dsa_body(q_ref, qi_ref, wi_ref, k_ref, v_ref, ki_ref, o_ref,
              key_s, bias_s, qi_s, q_s, plane_s, *, tq, tk, q_pos0, l_valid, n_kb_total, n_sel):
    row0 = q_pos0 + pl.program_id(1) * tq
    lane = lax.broadcasted_iota(I32, (1, LANES), 1)
    lo_half = lane < HEAD_DIM
    qpos = row0 + lax.broadcasted_iota(I32, (tq, 1), 0)
    qchunk = qpos // CHUNK
    kend = jnp.minimum(((row0 + tq - 1) // CHUNK + 1) * CHUNK, l_valid)
    nkb = jnp.minimum(jnp.maximum((kend + tk - 1) // tk, (n_sel + tk - 1) // tk), n_kb_total)

    n_stack = IDX_HEADS // 2
    for p in range(n_stack):
        blk = qi_ref[0, :, p * LANES:(p + 1) * LANES]
        zero = jnp.zeros_like(blk)
        qi_s[0, p * tq:(p + 1) * tq, :] = jnp.where(lo_half, blk, zero)
        qi_s[1, p * tq:(p + 1) * tq, :] = jnp.where(lo_half, zero, blk)
    wsc = wi_ref[0] * (IDX_DIM ** -0.5 * IDX_HEADS ** -0.5)

    def wide_then_single(step, init):
        carry = lax.fori_loop(0, nkb // 2, lambda j, c: step(2 * j, c, 2 * tk), init)
        return lax.fori_loop(nkb // 2 * 2, nkb, lambda j, c: step(j, c, tk), carry)

    def score_body(j, _, width):
        for u in range(width // SCORE_SUB):
            ks = pl.multiple_of(j * tk + u * SCORE_SUB, SCORE_SUB)
            ki = ki_ref[0, pl.ds(ks, SCORE_SUB), :]
            score = jnp.zeros((tq, SCORE_SUB), F32)
            for half in range(2):
                rel = jnp.maximum(_dot_nt(qi_s[half], ki), 0.0)
                for p in range(n_stack):
                    h = 2 * p + half
                    score = score + wsc[:, h:h + 1] * rel[p * tq:(p + 1) * tq]
            kpos = ks + lax.broadcasted_iota(I32, (1, SCORE_SUB), 1)
            vis = ((kpos // CHUNK) <= qchunk) & (kpos < l_valid)
            score = jnp.where(vis, score, -jnp.inf)
            bits = lax.bitcast_convert_type(score, I32)
            key = bits ^ ((bits >> 31) & INT_MAX)
            key_s[:, pl.ds(ks, SCORE_SUB)] = jnp.where(bits == INT_MIN, 0, key)
        return 0

    wide_then_single(score_body, 0)

    n_sel_f = float(n_sel)
    cr = min(COUNT_ROWS, tq)

    slabs = [slice(r * cr, (r + 1) * cr) for r in range(tq // cr)]

    def count(rs, pred):
        def cbody(j, acc):
            blk = key_s[rs, pl.ds(pl.multiple_of(j * tk, tk), tk)]
            hit = jnp.where(pred(blk, j), 1.0, 0.0)
            for c in range(tk // LANES):
                acc = acc + hit[:, c * LANES:(c + 1) * LANES]
            return acc
        acc = lax.fori_loop(0, nkb, cbody, jnp.zeros((cr, LANES), F32))
        return jnp.sum(acc, axis=1, keepdims=True)

    n_grp_total = n_kb_total * tk // BIT_GROUP
    n_bgrp = jnp.minimum((nkb * tk + BIT_GROUP - 1) // BIT_GROUP, n_grp_total)

    def fill_body(j, _):
        key_s[:, pl.ds(pl.multiple_of(j * tk, tk), tk)] = jnp.full((tq, tk), KEY_NEG_INF, I32)
        return 0

    lax.fori_loop(nkb, n_bgrp * (BIT_GROUP // tk), fill_body, 0)

    def plane_body(it, _):
        g, r = it // (tq // 8), it % (tq // 8)
        rows = pl.ds(pl.multiple_of(r * 8, 8), 8)
        base = pl.multiple_of(g * BIT_GROUP, BIT_GROUP)
        a = [key_s[rows, pl.ds(base + c * LANES, LANES)] ^ INT_MIN for c in range(32)]
        j, m = 16, 0x0000FFFF
        while j:
            k = 0
            while k < 32:
                t = (a[k] ^ lax.shift_right_logical(a[k + j], jnp.full((8, LANES), j, I32))) & m
                a[k] = a[k] ^ t
                a[k + j] = a[k + j] ^ (t << j)
                k = (k + j + 1) & ~j
            j >>= 1
            m = m ^ (m << j)
        for i in range(32):
            plane_s[i, rows, pl.ds(pl.multiple_of(g * LANES, LANES), LANES)] = a[i]
        return 0

    lax.fori_loop(0, n_bgrp * (tq // 8), plane_body, 0)

    plane_w = n_grp_total * LANES
    in_use = (lax.broadcasted_iota(I32, (1, plane_w), 1) // LANES) < n_bgrp

    def bit_pass(i, state):
        alive, above, thr_u = state
        plane = plane_s[i]
        ones = alive & plane
        cnt = jnp.sum(lax.population_count(ones).astype(F32), axis=1, keepdims=True)
        keep = above + cnt >= n_sel_f
        alive = jnp.where(keep, ones, alive & ~plane)
        above = jnp.where(keep, above, above + cnt)
        bit = lax.shift_right_logical(jnp.full((tq, 1), INT_MIN, I32), jnp.full((tq, 1), i, I32))
        thr_u = jnp.where(keep, thr_u | bit, thr_u)
        return alive, above, thr_u

    alive0 = jnp.where(in_use, -1, 0) + jnp.zeros((tq, plane_w), I32)
    alive, above, thr_u = lax.fori_loop(
        0, 32, bit_pass, (alive0, jnp.zeros((tq, 1), F32), jnp.zeros((tq, 1), I32)))
    thr_all = thr_u ^ INT_MIN
    c_ge_all = above + jnp.sum(lax.population_count(alive).astype(F32), axis=1, keepdims=True)
    thrs = [thr_all[rs] for rs in slabs]
    c_ges = [c_ge_all[rs] for rs in slabs]

    for rs, thr, c_ge in zip(slabs, thrs, c_ges):
        tied = (c_ge > n_sel_f) & (thr > KEY_NEG_INF)
        any_tied = jnp.max(jnp.where(tied, 1.0, 0.0)) > 0.0

        def tie_index_bound(rs=rs, thr=thr, tied=tied):
            need = n_sel_f - count(rs, lambda blk, j: blk > thr)

            def kidx(j):
                return j * tk + lax.broadcasted_iota(I32, (1, tk), 1)

            n_bits = (n_kb_total * tk - 1).bit_length()

            def jbody(b, jb):
                cand = jb | (1 << (n_bits - 1 - b))
                cnt = count(rs, lambda blk, j: (blk == thr) & (kidx(j) < cand))
                return jnp.where(cnt < need, cand, jb)

            jb = lax.fori_loop(0, n_bits, jbody, jnp.zeros((cr, 1), I32))
            return jnp.where(tied, jb, INT_MAX)

        jbound = lax.cond(any_tied, tie_index_bound, lambda: jnp.full((cr, 1), INT_MAX, I32))

        def bias_body(j, _, rs=rs, thr=thr, jbound=jbound):
            ks = pl.multiple_of(j * tk, tk)
            blk = key_s[rs, pl.ds(ks, tk)]
            kpos = ks + lax.broadcasted_iota(I32, (1, tk), 1)
            sel = (blk > thr) | ((blk == thr) & (kpos <= jbound))
            sel = sel & (blk > KEY_NEG_INF)
            bias_s[rs, pl.ds(ks, tk)] = jnp.where(sel, 0.0, NEG_BIG)
            return 0

        lax.fori_loop(0, nkb, bias_body, 0)

    n_grp = H_DSA // KV_DSA
    for p in range(n_grp):
        blk = q_ref[0, :, p * LANES:(p + 1) * LANES]
        zero = jnp.zeros_like(blk)
        q_s[0, p * tq:(p + 1) * tq, :] = jnp.where(lo_half, blk, zero)
        q_s[1, p * tq:(p + 1) * tq, :] = jnp.where(lo_half, zero, blk)

    def att_body(j, carry, width):
        ks = pl.multiple_of(j * tk, tk)
        k = k_ref[0, pl.ds(ks, width), :]
        v = v_ref[0, pl.ds(ks, width), :]
        bias = bias_s[:, pl.ds(ks, width)]
        bias = jnp.concatenate([bias] * n_grp, axis=0)
        return tuple(_softmax_step(_dot_nt(q_s[half], k) + bias, v, *carry[half])
                     for half in range(2))

    rows = n_grp * tq
    init = (jnp.full((rows, 1), M_INIT, F32), jnp.zeros((rows, 1), F32),
            jnp.zeros((rows, LANES), F32))
    (_, l0, a0), (_, l1, a1) = wide_then_single(att_body, (init, init))
    o0, o1 = a0 / l0, a1 / l1
    for p in range(n_grp):
        o_ref[0, :, p * LANES:(p + 1) * LANES] = jnp.where(
            lo_half, o0[p * tq:(p + 1) * tq], o1[p * tq:(p + 1) * tq]).astype(BF16)


def _dsa_attention(q, qi, qi_cb, wi, wi_cb, k, k_cb, v, v_cb, ki, ki_cb,
                   q_pos0, l_valid, n_sel, tq, tk):
    b, t = q.shape[:2]
    l = k.shape[1]
    width = H_DSA * HEAD_DIM
    return pl.pallas_call(
        functools.partial(_dsa_body, tq=tq, tk=tk, q_pos0=q_pos0, l_valid=l_valid,
                          n_kb_total=l // tk, n_sel=n_sel),
        grid=(b, t // tq),
        in_specs=[pl.BlockSpec((1, tq, width), lambda bi, i: (bi, i, 0)),
                  pl.BlockSpec((1, tq, width), lambda bi, i: (bi, i, qi_cb)),
                  pl.BlockSpec((1, tq, LANES), lambda bi, i: (bi, i, wi_cb)),
                  pl.BlockSpec((1, l, LANES), lambda bi, i: (bi, 0, k_cb)),
                  pl.BlockSpec((1, l, LANES), lambda bi, i: (bi, 0, v_cb)),
                  pl.BlockSpec((1, l, LANES), lambda bi, i: (bi, 0, ki_cb))],
        out_specs=pl.BlockSpec((1, tq, width), lambda bi, i: (bi, i, 0)),
        out_shape=jax.ShapeDtypeStruct((b, t, width), BF16),
        scratch_shapes=[pltpu.VMEM((tq, l), I32), pltpu.VMEM((tq, l), F32),
                        pltpu.VMEM((2, IDX_HEADS // 2 * tq, LANES), BF16),
                        pltpu.VMEM((2, H_DSA // KV_DSA * tq, LANES), BF16),
                        pltpu.VMEM((32, tq, l // BIT_GROUP * LANES), I32)],
        compiler_params=_params(("parallel", "parallel")),
        name="dsa_attention",
    )(q, qi, wi, k, v, ki)


def _moe_body(h_ref, g_ref, wr_ref, w1_ref, w3_ref, w2_ref, o_ref,
              xn_s, xg_s, ye_s, rank_s, gate_s, rank_t_s, *, tm, ch, n_fc):
    e = pl.program_id(1)
    fc = pl.program_id(2)
    lane = lax.broadcasted_iota(I32, (1, LANES), 1)

    @pl.when((e == 0) & (fc == 0))
    def _route():
        x = h_ref[...]
        xn = _rms(x, g_ref[...])
        xn_s[...] = xn.astype(BF16)
        o_ref[...] = x
        x3 = _split3(xn)
        logits = jnp.zeros((tm, LANES), F32)
        for a, b in ((2, 0), (0, 2), (1, 1), (1, 0), (0, 1), (0, 0)):
            logits = logits + _dot(x3[a], wr_ref[b])
        lane_f = lane.astype(F32)
        logits = jnp.where(lane < N_EXPERTS, logits, -jnp.inf)
        m1 = jnp.max(logits, axis=1, keepdims=True)
        i1 = jnp.min(jnp.where(logits == m1, lane_f, float(LANES)), axis=1, keepdims=True)
        rest = jnp.where(lane_f == i1, -jnp.inf, logits)
        m2 = jnp.max(rest, axis=1, keepdims=True)
        i2 = jnp.min(jnp.where(rest == m2, lane_f, float(LANES)), axis=1, keepdims=True)
        e2 = jnp.exp(m2 - m1)
        g1 = 1.0 / (1.0 + e2)
        g2 = e2 / (1.0 + e2)
        sel1 = lane_f == i1
        sel2 = lane_f == i2
        gate_s[...] = jnp.where(sel1, g1, 0.0) + jnp.where(sel2, g2, 0.0)
        sel = jnp.where(sel1 | sel2, 1.0, 0.0)
        incl = (lax.broadcasted_iota(I32, (tm, tm), 1) <=
                lax.broadcasted_iota(I32, (tm, tm), 0)).astype(BF16)
        rank = _dot(incl, sel.astype(BF16)) * sel
        rank_s[...] = rank
        rank_t_s[...] = rank.T

    rank_row = rank_t_s[pl.ds(e, 1), :]
    cnt = jnp.max(rank_row).astype(I32)
    nch = (cnt + ch - 1) // ch

    @pl.when(fc == 0)
    def _gather():
        def gbody(c, _):
            base = pl.multiple_of(c * ch, ch)
            slot = (base + 1 + lax.broadcasted_iota(I32, (ch, 1), 0)).astype(F32)
            pick = jnp.where(rank_row == slot, 1.0, 0.0).astype(BF16)
            xg_s[pl.ds(base, ch), :] = _dot(pick, xn_s[...]).astype(BF16)
            return 0
        lax.fori_loop(0, nch, gbody, 0)

    def fbody(c, _):
        base = pl.multiple_of(c * ch, ch)
        xg = xg_s[pl.ds(base, ch), :]
        a = _dot(xg, w1_ref[0])
        b = _dot(xg, w3_ref[0])
        part = _dot((a * _sigmoid(a) * b).astype(BF16), w2_ref[0])

        @pl.when(fc == 0)
        def _():
            ye_s[pl.ds(base, ch), :] = part

        @pl.when(fc != 0)
        def _():
            ye_s[pl.ds(base, ch), :] += part
        return 0

    lax.fori_loop(0, nch, fbody, 0)

    @pl.when(fc == n_fc - 1)
    def _scatter():
        here = lane == e
        rank_col = jnp.sum(jnp.where(here, rank_s[...], 0.0), axis=1, keepdims=True)
        gate_col = jnp.sum(jnp.where(here, gate_s[...], 0.0), axis=1, keepdims=True)

        def sbody(c, _):
            base = pl.multiple_of(c * ch, ch)
            slot = (base + 1 + lax.broadcasted_iota(I32, (1, ch), 1)).astype(F32)
            place = jnp.where(rank_col == slot, 1.0, 0.0).astype(BF16)
            ye = ye_s[pl.ds(base, ch), :].astype(BF16)
            o_ref[...] += gate_col * _dot(place, ye)
            return 0
        lax.fori_loop(0, nch, sbody, 0)


def _moe(h, g, wr3, w1, w3, w2, tm, ch, tf):
    n, d = h.shape
    ne, _, f = w1.shape
    n_fc = f // tf
    return pl.pallas_call(
        functools.partial(_moe_body, tm=tm, ch=ch, n_fc=n_fc),
        grid=(n // tm, ne, n_fc),
        in_specs=[pl.BlockSpec((tm, d), lambda i, e, c: (i, 0), pipeline_mode=pl.Buffered(1)),
                  pl.BlockSpec((1, d), lambda i, e, c: (0, 0)),
                  pl.BlockSpec((3, d, LANES), lambda i, e, c: (0, 0, 0)),
                  pl.BlockSpec((1, d, tf), lambda i, e, c: (e, 0, c)),
                  pl.BlockSpec((1, d, tf), lambda i, e, c: (e, 0, c)),
                  pl.BlockSpec((1, tf, d), lambda i, e, c: (e, c, 0))],
        out_specs=pl.BlockSpec((tm, d), lambda i, e, c: (i, 0)),
        out_shape=jax.ShapeDtypeStruct((n, d), F32),
        scratch_shapes=[pltpu.VMEM((tm, d), BF16), pltpu.VMEM((_round_up(tm, ch), d), BF16),
                        pltpu.VMEM((_round_up(tm, ch), d), F32), pltpu.VMEM((tm, LANES), F32),
                        pltpu.VMEM((tm, LANES), F32), pltpu.VMEM((LANES, tm), F32)],
        compiler_params=_params(("parallel", "arbitrary", "arbitrary")),
        name="moe",
    )(h, g.reshape(1, d), wr3, w1, w3, w2)


def _round_up(x, m):
    return (x + m - 1) // m * m


def _pad_rows(a, l_pad):
    return jnp.pad(a, ((0, 0), (0, l_pad - a.shape[1]), (0, 0)))


def _prep_weights(g_q_dsa, g_k_dsa, g_q_fox, g_k_fox, w_in_even, w_out_even, w_in_odd, w_router):
    d = w_in_even.shape[0]
    perm = jnp.asarray(DSA_HEAD_PERM)
    qa, ka, va, qb, kb, vb, qi, ki, wi = jnp.split(
        w_in_even, [512, 1024, 1536, 2048, 2176, 2304, 2816, 2880], axis=1)
    qb = qb.reshape(d, H_DSA, HEAD_DIM)[:, perm].reshape(d, H_DSA * HEAD_DIM)
    wi = jnp.pad(wi, ((0, 0), (0, LANES - IDX_HEADS)))
    w_even = jnp.concatenate([qb, kb, vb, ki, ki, wi, qi, qa, ka, va], axis=1).astype(BF16)
    ones = lambda k: jnp.ones((k,), F32)
    q_scale = lambda k: jnp.full((k,), Q_SCALE, F32)
    scale_even = jnp.concatenate([ones(E_QA), q_scale(E_KA - E_QA), ones(E_END - E_KA)])
    scale_odd = ones(O_END)
    scale_norm_even = jnp.concatenate([q_scale(E_KB - E_QB), ones(E_VB - E_KB)])
    scale_norm_odd = jnp.concatenate([q_scale(O_K - O_Q), ones(O_V - O_K)])
    w_out_sb = w_out_even[:H_SB * HEAD_DIM].astype(BF16)
    w_out_dsa = w_out_even[H_SB * HEAD_DIM:].reshape(H_DSA, HEAD_DIM, d)[perm]
    w_out_dsa = w_out_dsa.reshape(H_DSA * HEAD_DIM, d).astype(BF16)
    w_odd = jnp.pad(w_in_odd, ((0, 0), (0, O_END - w_in_odd.shape[1]))).astype(BF16)
    gains_even = jnp.concatenate([jnp.tile(g_q_dsa, H_DSA), jnp.tile(g_k_dsa, KV_DSA)])
    gains_odd = jnp.concatenate([jnp.tile(g_q_fox, H_FOX), jnp.tile(g_k_fox, H_FOX)])
    wr = jnp.pad(w_router, ((0, 0), (0, LANES - N_EXPERTS)))
    wr_hi = wr.astype(BF16)
    wr_r1 = wr - wr_hi.astype(F32)
    wr_mid = wr_r1.astype(BF16)
    wr_lo = (wr_r1 - wr_mid.astype(F32)).astype(BF16)
    return dict(w_even=w_even, w_out_sb=w_out_sb, w_out_dsa=w_out_dsa, w_odd=w_odd,
                gains_even=gains_even, gains_odd=gains_odd,
                scale_even=scale_even, scale_odd=scale_odd,
                scale_norm_even=scale_norm_even, scale_norm_odd=scale_norm_odd,
                wr3=jnp.stack([wr_hi, wr_mid, wr_lo]))


def _rotary_tables(pos):
    half = ROT_DIM // 2
    inv = ROPE_THETA ** (-jnp.arange(half, dtype=F32) / half)
    ang = pos.astype(F32)[:, None] * inv[None, :]
    cos, sin = jnp.cos(ang), jnp.sin(ang)
    t = pos.shape[0]
    pad = HEAD_DIM - ROT_DIM
    cos_h = jnp.concatenate([cos, cos, jnp.ones((t, pad), F32)], axis=1)
    sin_h = jnp.concatenate([-sin, sin, jnp.zeros((t, pad), F32)], axis=1)
    return jnp.tile(cos_h, (1, 2)), jnp.tile(sin_h, (1, 2))


def _trunk(x, p, pos0, past_even, past_odd, keep, w, wb):
    b, t, d = x.shape
    n = b * t
    tm = min(ROW_TILE, n)
    has_past = past_even is not None
    past_len = past_even[0].shape[1] if has_past else 0
    l_valid = past_len + t
    l_pad = _round_up(l_valid, KEY_PAD)
    n_sel = min(DSA_TOPK, l_valid // 4)
    h = x.reshape(n, d)
    assert n % tm == 0 and (t % tm == 0 or tm % t == 0), (n, t, tm)
    assert all(t % min(tile, t) == 0 for tile in (SB_Q_TILE, FOX_Q_TILE, DSA_Q_TILE)), t
    assert has_past or t % KEY_PAD == 0, "without a cache the keys are this step's rows, unpadded"
    assert n % min(MOE_TILE, n) == 0 and n_sel <= l_pad

    def keys(cache, new):
        if not has_past:
            return new
        full = jnp.concatenate([cache.reshape(b, past_len, -1).astype(BF16), new], axis=1)
        return _pad_rows(full, l_pad)

    cos, sin = _rotary_tables(pos0 + jnp.arange(t, dtype=I32))
    if t % tm:
        cos, sin = jnp.tile(cos, (tm // t, 1)), jnp.tile(sin, (tm // t, 1))
    pf, pb, qkf, qkb = _proj(h, w["g_mix"][0], wb["w_even"], wb["scale_even"],
                             wb["gains_even"], wb["scale_norm_even"], cos, sin, 0, tm)
    pf3, pb3 = pf.reshape(b, t, E_END), pb.reshape(b, t, E_END)
    qkf3, qkb3 = qkf.reshape(b, t, E_NORM), qkb.reshape(b, t, E_NORM)
    cols = lambda a, c0, width: a[:, :, c0:c0 + width]
    sb_w, kv_w = H_SB * HEAD_DIM, KV_DSA * HEAD_DIM
    if has_past:
        c_sbk, c_sbv, c_dk, c_dv, c_ki = past_even
        ka, ka_cb = keys(c_sbk, cols(pb3, E_KA, sb_w)), 0
        va, va_cb = keys(c_sbv, cols(pb3, E_VA, sb_w)), 0
        kb, kb_cb = keys(c_dk, cols(qkb3, E_KB, kv_w)), 0
        vb, vb_cb = keys(c_dv, cols(pb3, E_VB, kv_w)), 0
        ki2 = jnp.concatenate([c_ki, c_ki], axis=-1)
        ki, ki_cb = keys(ki2, cols(pb3, E_KI, LANES)), 0
    else:
        ka, ka_cb = pb3, E_KA // LANES
        va, va_cb = pb3, E_VA // LANES
        kb, kb_cb = qkb3, E_KB // LANES
        vb, vb_cb = pb3, E_VB // LANES
        ki, ki_cb = pb3, E_KI // LANES
    oa = _sb_attention(pb3, E_QA // LANES, ka, ka_cb, va, va_cb, H_SB // 2, pos0,
                       min(SB_Q_TILE, t), SB_K_TILE)
    ob = _dsa_attention(qkb3, pb3, E_QI // 512, pf3, E_WI // LANES, kb, kb_cb, vb, vb_cb,
                        ki, ki_cb, pos0, l_valid, n_sel, min(DSA_Q_TILE, t), DSA_K_TILE)
    h = _even_channel(h, oa.reshape(n, -1), ob.reshape(n, -1), wb["w_out_sb"], wb["w_out_dsa"],
                      w["g_ffn"][0], wb["w_ff1"], wb["w_ff3"], wb["w_ff2"],
                      w["g_ple"][0], wb["w_ple_gate"][0], p[0].reshape(n, -1), wb["w_ple_in"][0], tm)
    kept = lambda a, c0, width: a[:, t - keep:, c0:c0 + width]
    even_state = (kept(pf3, E_KA, sb_w).reshape(b, keep, H_SB, HEAD_DIM),
                  kept(pf3, E_VA, sb_w).reshape(b, keep, H_SB, HEAD_DIM),
                  kept(qkf3, E_KB, kv_w).reshape(b, keep, KV_DSA, HEAD_DIM),
                  kept(pf3, E_VB, kv_w).reshape(b, keep, KV_DSA, HEAD_DIM),
                  kept(pf3, E_KI, IDX_DIM))

    fox_w = H_FOX * HEAD_DIM
    pf, pb, qkf, qkb = _proj(h, w["g_mix"][1], wb["w_odd"], wb["scale_odd"],
                             wb["gains_odd"], wb["scale_norm_odd"], None, None, O_V, tm)
    r_f = O_F - O_V
    pf3, pb3 = pf.reshape(b, t, O_END - O_V), pb.reshape(b, t, O_END - O_V)
    qkf3, qkb3 = qkf.reshape(b, t, O_V), qkb.reshape(b, t, O_V)
    if has_past:
        c_fk, c_fv, c_lf = past_odd
        kf, kf_cb = keys(c_fk, cols(qkb3, O_K, fox_w)), 0
        vf, vf_cb = keys(c_fv, cols(pb3, 0, fox_w)), 0
        raw = _pad_rows(jnp.concatenate([c_lf, cols(pf3, r_f, H_FOX)], axis=1), l_pad)
        raw, raw_cb = jnp.pad(raw, ((0, 0), (0, 0), (0, LANES - H_FOX))), 0
    else:
        kf, kf_cb = qkb3, O_K // LANES
        vf, vf_cb = pb3, 0
        raw, raw_cb = pf3, r_f // LANES
    logf, cum = _forget_cumsum(raw, raw_cb, w["b_forget"][0], past_len, min(FORGET_TILE, l_pad))
    logf, cum = logf[:, :, :H_FOX], cum[:, :, :H_FOX]
    fk = cum.reshape(b, l_pad, H_FOX // 2, 2).transpose(0, 2, 3, 1)

    normed_max = lambda g: HEAD_DIM ** 0.5 * jnp.max(jnp.abs(g))
    q_max = jnp.full((b, H_FOX), Q_SCALE * normed_max(w["g_q_fox"]), F32)
    if has_past:
        kx = kf.astype(F32).reshape(b, l_pad, H_FOX, HEAD_DIM)
        k_max = jnp.sqrt(jnp.max(jnp.sum(kx * kx, axis=-1), axis=1))
    else:
        k_max = jnp.full((b, H_FOX), normed_max(w["g_k_fox"]), F32)
    decay_cut = EXP2_ZERO + 2.0 * 1.02 * q_max * k_max
    fox_tq = min(FOX_Q_TILE, t)
    j0 = _fox_first_block(cum, decay_cut, pos0, t, fox_tq, FOX_K_TILE, FOX_PAIRS)
    of = _fox_attention(qkb3, O_Q // LANES, kf, kf_cb, vf, vf_cb, fk, j0, pos0, fox_tq, FOX_K_TILE)
    h = _residual_matmul(h, [of.reshape(n, -1)], [wb["w_out_odd"]], tm)
    h = _moe(h, w["g_ffn"][1], wb["wr3"], wb["w_exp1"], wb["w_exp3"], wb["w_exp2"],
             min(MOE_TILE, n), min(MOE_CHUNK, n), MOE_F_TILE)
    h = _ple(h, w["g_ple"][1], wb["w_ple_gate"][1], p[1].reshape(n, -1), wb["w_ple_in"][1], tm)
    odd_state = (kept(qkf3, O_K, fox_w).reshape(b, keep, H_FOX, HEAD_DIM),
                 kept(pf3, 0, fox_w).reshape(b, keep, H_FOX, HEAD_DIM),
                 logf[:, l_valid - keep:l_valid])
    return h.reshape(b, t, d), even_state, odd_state


def kernel(x_prompt, x_sample, p_prompt, p_sample, cache_sb_k, cache_sb_v, cache_dsa_k, cache_dsa_v, cache_dsa_kidx, cache_fox_k, cache_fox_v, cache_fox_logf, g_mix, g_ffn, g_ple, w_in_even, g_q_dsa, g_k_dsa, w_out_even, w_ff1, w_ff3, w_ff2, w_in_odd, b_forget, g_q_fox, g_k_fox, w_out_odd, w_router, w_exp1, w_exp3, w_exp2, w_ple_in, w_ple_gate):
    assert g_mix.shape[0] == 2, "two layers: one even (stick-breaking + DSA), one odd (FoX + experts)"
    past_len = cache_sb_k.shape[2]
    w = dict(g_mix=g_mix, g_ffn=g_ffn, g_ple=g_ple, b_forget=b_forget,
             g_q_fox=g_q_fox[0], g_k_fox=g_k_fox[0])
    wb = _prep_weights(g_q_dsa[0], g_k_dsa[0], g_q_fox[0], g_k_fox[0],
                       w_in_even[0], w_out_even[0], w_in_odd[0], w_router[0])
    wb.update(w_ff1=w_ff1[0].astype(BF16), w_ff3=w_ff3[0].astype(BF16), w_ff2=w_ff2[0].astype(BF16),
              w_out_odd=w_out_odd[0].astype(BF16),
              w_exp1=w_exp1[0].astype(BF16), w_exp3=w_exp3[0].astype(BF16),
              w_exp2=w_exp2[0].astype(BF16),
              w_ple_in=w_ple_in.astype(BF16), w_ple_gate=w_ple_gate.astype(BF16))

    y_p, even_p, odd_p = _trunk(x_prompt, p_prompt, 0, None, None,
                                min(x_prompt.shape[1], past_len), w, wb)
    y_s, even_s, odd_s = _trunk(
        x_sample, p_sample, past_len,
        (cache_sb_k[0], cache_sb_v[0], cache_dsa_k[0], cache_dsa_v[0], cache_dsa_kidx[0]),
        (cache_fox_k[0], cache_fox_v[0], cache_fox_logf[0]), x_sample.shape[1], w, wb)
    layer = lambda a: a[None]
    return (y_p, y_s,
            *(layer(a) for a in even_p), *(layer(a) for a in odd_p),
            *(layer(a) for a in even_s), *(layer(a) for a in odd_s))
```

```python
import functools

import jax
import jax.numpy as jnp
from jax import lax
from jax.experimental import pallas as pl
from jax.experimental.pallas import tpu as pltpu

F32 = jnp.float32
BF16 = jnp.bfloat16
I32 = jnp.int32

EPS = 1e-6
HEAD_DIM = 64
CHUNK = 64
H_SB = 8
H_DSA = 8
KV_DSA = 2
IDX_HEADS = 8
IDX_DIM = 64
H_FOX = 16
DSA_TOPK = 256
ROT_DIM = HEAD_DIM // 4
ROPE_THETA = 500000.0
N_EXPERTS = 8
ATT_SCALE = HEAD_DIM ** -0.5
LOG2E = 1.4426950408889634
Q_SCALE = ATT_SCALE * LOG2E

LANES = 128
MIB = 1024 * 1024
VMEM_LIMIT = 56 * MIB

ROW_TILE = 512
SB_Q_TILE, SB_K_TILE = 256, 256
SB_BLOCKS_PER_STEP = 1
EXP2_ZERO = 150.0
FOX_Q_TILE, FOX_K_TILE = 512, 512
FOX_PAIRS = 1
DSA_Q_TILE, DSA_K_TILE = 256, 512
COUNT_ROWS = 128
SCORE_SUB = 256
BIT_GROUP = 32 * LANES
KEY_PAD = BIT_GROUP
FORGET_TILE = 512
MOE_TILE = 1024
MOE_CHUNK = 288
MOE_F_TILE = 1792

NEG_BIG = -1e30
M_INIT = -1e29
INT_MIN = -(2 ** 31)
INT_MAX = 2 ** 31 - 1
KEY_NEG_INF = -2139095041

E_QB, E_KB, E_VB, E_KI, E_WI, E_QI, E_QA, E_KA, E_VA, E_END = (
    0, 512, 640, 768, 896, 1024, 1536, 2048, 2560, 3072)
E_NORM = E_VB - E_QB
O_Q, O_K, O_V, O_F, O_END = 0, 1024, 2048, 3072, 3200
DSA_HEAD_PERM = (0, 4, 1, 5, 2, 6, 3, 7)


def _params(sem, vmem=VMEM_LIMIT):
    return pltpu.CompilerParams(dimension_semantics=sem, vmem_limit_bytes=vmem)


def _dot(a, b):
    return jnp.dot(a, b, preferred_element_type=F32)


def _dot_nt(a, b):
    return lax.dot_general(a, b, (((1,), (1,)), ((), ())), preferred_element_type=F32)


def _rms(x, g):
    return x * lax.rsqrt(jnp.mean(x * x, axis=-1, keepdims=True) + EPS) * g


def _split3(x):
    hi = x.astype(BF16)
    r1 = x - hi.astype(F32)
    mid = r1.astype(BF16)
    lo = (r1 - mid.astype(F32)).astype(BF16)
    return hi, mid, lo


def _sigmoid(x):
    return 1.0 / (1.0 + jnp.exp(-x))


def _log_sigmoid_neg(z):
    return -(jnp.maximum(z, 0.0) + jnp.log(1.0 + jnp.exp(-jnp.abs(z))))


def _proj_body(x_ref, g_ref, w_ref, sc_ref, gn_ref, scn_ref, cos_ref, sin_ref, s_ref,
               of_ref, ob_ref, nf_ref, nb_ref, *, rotary, width, keep_from):
    y = _rms(x_ref[...], g_ref[...]).astype(BF16)
    r = _dot(y, w_ref[...])
    of_ref[...] = r[:, keep_from:]
    ob_ref[...] = (r[:, keep_from:] * sc_ref[:, keep_from:]).astype(BF16)
    s = s_ref[...]
    for j in range(width // LANES):
        cols = slice(j * LANES, (j + 1) * LANES)
        x = r[:, cols]
        hi, mid, lo = _split3(x * x)
        ms = (_dot(hi, s) + _dot(mid, s) + _dot(lo, s)) * (1.0 / HEAD_DIM)
        y = x * lax.rsqrt(ms + EPS) * gn_ref[:, cols]
        if rotary:
            lane = lax.broadcasted_iota(I32, (1, LANES), 1) % HEAD_DIM
            partner = jnp.where(lane < ROT_DIM // 2,
                                pltpu.roll(y, LANES - ROT_DIM // 2, 1),
                                pltpu.roll(y, ROT_DIM // 2, 1))
            y = y * cos_ref[...] + partner * sin_ref[...]
        nf_ref[:, cols] = y
        nb_ref[:, cols] = (y * scn_ref[:, cols]).astype(BF16)


def _proj(x, g, w, scales, gains_n, scales_n, cos, sin, keep_from, tm):
    n, d = x.shape
    c = w.shape[1]
    width = gains_n.shape[0]
    kept = c - keep_from
    rotary = cos is not None
    if not rotary:
        cos = jnp.zeros((8, LANES), F32)
        sin = cos
        tab_spec = pl.BlockSpec((8, LANES), lambda i: (0, 0))
    else:
        nt = cos.shape[0] // tm
        tab_spec = pl.BlockSpec((tm, LANES), lambda i: (i % nt, 0))
    rr = lax.broadcasted_iota(I32, (LANES, LANES), 0) // HEAD_DIM
    cc = lax.broadcasted_iota(I32, (LANES, LANES), 1) // HEAD_DIM
    seg = (rr == cc).astype(BF16)
    row = lambda width_: pl.BlockSpec((tm, width_), lambda i: (i, 0))
    vec = lambda width_: pl.BlockSpec((1, width_), lambda i: (0, 0))
    return pl.pallas_call(
        functools.partial(_proj_body, rotary=rotary, width=width, keep_from=keep_from),
        grid=(n // tm,),
        in_specs=[row(d), vec(d),
                  pl.BlockSpec((d, c), lambda i: (0, 0), pipeline_mode=pl.Buffered(1)),
                  vec(c), vec(width), vec(width), tab_spec, tab_spec,
                  pl.BlockSpec((LANES, LANES), lambda i: (0, 0))],
        out_specs=[row(kept), row(kept), row(width), row(width)],
        out_shape=[jax.ShapeDtypeStruct((n, kept), F32), jax.ShapeDtypeStruct((n, kept), BF16),
                   jax.ShapeDtypeStruct((n, width), F32), jax.ShapeDtypeStruct((n, width), BF16)],
        compiler_params=_params(("parallel",)),
        name="proj",
    )(x, g.reshape(1, d), w, scales.reshape(1, c), gains_n.reshape(1, width),
      scales_n.reshape(1, width), cos, sin, seg)


def _res_body(*refs, n_in):
    h_ref, o_ref = refs[0], refs[-1]
    acc = h_ref[...]
    for t in range(n_in):
        acc = acc + _dot(refs[1 + t][...], refs[1 + n_in + t][...])
    o_ref[...] = acc


def _residual_matmul(h, acts, ws, tm):
    n, d = h.shape
    n_in = len(acts)
    in_specs = [pl.BlockSpec((tm, d), lambda i: (i, 0))]
    in_specs += [pl.BlockSpec((tm, a.shape[1]), lambda i: (i, 0)) for a in acts]
    in_specs += [pl.BlockSpec(w.shape, lambda i: (0, 0)) for w in ws]
    return pl.pallas_call(
        functools.partial(_res_body, n_in=n_in),
        grid=(n // tm,),
        in_specs=in_specs,
        out_specs=pl.BlockSpec((tm, d), lambda i: (i, 0)),
        out_shape=jax.ShapeDtypeStruct((n, d), F32),
        compiler_params=_params(("parallel",)),
        name="residual_matmul",
    )(h, *acts, *ws)


def _ple_body(h_ref, g_ref, wg_ref, p_ref, wp_ref, o_ref):
    x = h_ref[...]
    xn = _rms(x, g_ref[...]).astype(BF16)
    gate = _sigmoid(_dot(xn, wg_ref[...]))
    o_ref[...] = x + gate * _dot(p_ref[...].astype(BF16), wp_ref[...])


def _ple(h, g, wg, p, wp, tm):
    n, d = h.shape
    e = p.shape[1]
    return pl.pallas_call(
        _ple_body,
        grid=(n // tm,),
        in_specs=[pl.BlockSpec((tm, d), lambda i: (i, 0)),
                  pl.BlockSpec((1, d), lambda i: (0, 0)),
                  pl.BlockSpec((d, d), lambda i: (0, 0)),
                  pl.BlockSpec((tm, e), lambda i: (i, 0)),
                  pl.BlockSpec((e, d), lambda i: (0, 0))],
        out_specs=pl.BlockSpec((tm, d), lambda i: (i, 0)),
        out_shape=jax.ShapeDtypeStruct((n, d), F32),
        compiler_params=_params(("parallel",)),
        name="ple",
    )(h, g.reshape(1, d), wg, p, wp)


def _even_channel_body(h_ref, oa_ref, ob_ref, wa_ref, wb_ref, gf_ref, w1_ref, w3_ref, w2_ref,
                       gp_ref, wg_ref, p_ref, wp_ref, o_ref):
    x = h_ref[...] + _dot(oa_ref[...], wa_ref[...]) + _dot(ob_ref[...], wb_ref[...])
    xn = _rms(x, gf_ref[...]).astype(BF16)
    a = _dot(xn, w1_ref[...])
    b = _dot(xn, w3_ref[...])
    x = x + _dot((a * _sigmoid(a) * b).astype(BF16), w2_ref[...])
    xn = _rms(x, gp_ref[...]).astype(BF16)
    gate = _sigmoid(_dot(xn, wg_ref[...]))
    o_ref[...] = x + gate * _dot(p_ref[...].astype(BF16), wp_ref[...])


def _even_channel(h, oa, ob, wa, wb, gf, w1, w3, w2, gp, wg, p, wp, tm):
    n, d = h.shape
    once = pl.Buffered(1)
    row = lambda a: pl.BlockSpec((tm, a.shape[1]), lambda i: (i, 0))
    vec = pl.BlockSpec((1, d), lambda i: (0, 0))
    full = lambda a: pl.BlockSpec(a.shape, lambda i: (0, 0), pipeline_mode=once)
    return pl.pallas_call(
        _even_channel_body,
        grid=(n // tm,),
        in_specs=[row(h), row(oa), row(ob), full(wa), full(wb), vec, full(w1), full(w3), full(w2),
                  vec, full(wg), row(p), full(wp)],
        out_specs=row(h),
        out_shape=jax.ShapeDtypeStruct((n, d), F32),
        compiler_params=_params(("parallel",)),
        name="even_channel",
    )(h, oa, ob, wa, wb, gf.reshape(1, d), w1, w3, w2, gp.reshape(1, d), wg, p, wp)


def _sb_body(q_ref, k_ref, v_ref, later_ref, o_ref, *, tq, tk, q_pos0, n_kb_total):
    row0 = q_pos0 + pl.program_id(2) * tq
    q = q_ref[0]
    lane = lax.broadcasted_iota(I32, (1, LANES), 1)
    lo_half = lane < HEAD_DIM
    zero = jnp.zeros_like(q)
    q_halves = (jnp.where(lo_half, q, zero), jnp.where(lo_half, zero, q))
    qpos = row0 + lax.broadcasted_iota(I32, (tq, 1), 0)
    nkb = jnp.minimum((row0 + tq - 1 + tk - 1) // tk, n_kb_total)
    n_full = jnp.minimum(row0 // tk, nkb)
    later = later_ref[...]

    def block(half, kb, c, masked):
        ks = pl.multiple_of(kb * tk, tk)
        z = _dot_nt(q_halves[half], k_ref[0, pl.ds(ks, tk), :])
        sp = jnp.maximum(z, 0.0) + jnp.log2(1.0 + jnp.exp2(-jnp.abs(z)))
        if masked:
            vis = (ks + lax.broadcasted_iota(I32, (1, tk), 1)) < qpos
            sp = jnp.where(vis, sp, 0.0)
        between = _dot(sp.astype(BF16), later)
        w = jnp.exp2(z - sp - between - c)
        if masked:
            w = jnp.where(vis, w, 0.0)
        row_sum = between[:, 0:1] + sp[:, 0:1]
        return _dot(w.astype(BF16), v_ref[0, pl.ds(ks, tk), :]), row_sum

    def run(kbs, carry, masked):
        accs, cs = carry
        new_accs, new_cs = [], []
        for half in range(2):
            acc, c = accs[half], cs[half]
            for kb in kbs:
                pv, rs = block(half, kb, c, masked)
                acc, c = acc + pv, c + rs
            new_accs.append(acc)
            new_cs.append(c)
        return tuple(new_accs), tuple(new_cs)

    acc0 = jnp.zeros((tq, LANES), F32)
    c0 = jnp.zeros((tq, 1), F32)
    carry = ((acc0, acc0), (c0, c0))
    carry = lax.fori_loop(0, nkb - n_full,
                          lambda j, cr: run([nkb - 1 - j], cr, True), carry)
    def live(cr):
        return jnp.minimum(jnp.min(cr[1][0]), jnp.min(cr[1][1])) < EXP2_ZERO

    def steps(n_steps, kbs_of, carry):
        def cond(state):
            j, _, more = state
            return (j < n_steps) & more

        def body(state):
            j, cr, _ = state
            cr = run(kbs_of(j), cr, False)
            return j + 1, cr, live(cr)

        return lax.while_loop(cond, body, (0, carry, live(carry)))[1]

    grp = SB_BLOCKS_PER_STEP
    carry = steps(n_full // grp, lambda j: [n_full - 1 - grp * j - u for u in range(grp)], carry)
    rem = n_full % grp
    carry = steps(rem, lambda j: [rem - 1 - j], carry)
    accs, _ = carry
    o_ref[0] = jnp.where(lo_half, accs[0], accs[1]).astype(BF16)


def _sb_attention(q, q_cb, k, k_cb, v, v_cb, n_pairs, q_pos0, tq, tk):
    b, t = q.shape[:2]
    l = k.shape[1]
    later = (lax.broadcasted_iota(I32, (tk, tk), 0) >
             lax.broadcasted_iota(I32, (tk, tk), 1)).astype(BF16)
    return pl.pallas_call(
        functools.partial(_sb_body, tq=tq, tk=tk, q_pos0=q_pos0, n_kb_total=l // tk),
        grid=(b, n_pairs, t // tq),
        in_specs=[pl.BlockSpec((1, tq, LANES), lambda bi, p, i: (bi, i, q_cb + p)),
                  pl.BlockSpec((1, l, LANES), lambda bi, p, i: (bi, 0, k_cb + p)),
                  pl.BlockSpec((1, l, LANES), lambda bi, p, i: (bi, 0, v_cb + p)),
                  pl.BlockSpec((tk, tk), lambda bi, p, i: (0, 0))],
        out_specs=pl.BlockSpec((1, tq, LANES), lambda bi, p, i: (bi, i, p)),
        out_shape=jax.ShapeDtypeStruct((b, t, n_pairs * LANES), BF16),
        compiler_params=_params(("parallel", "parallel", "parallel")),
        name="sb_attention",
    )(q, k, v, later)


def _softmax_step(s, v, m, l, acc):
    m_new = jnp.maximum(m, jnp.max(s, axis=1, keepdims=True))
    alpha = jnp.exp2(m - m_new)
    p = jnp.exp2(s - m_new)
    l_new = alpha * l + jnp.sum(p, axis=1, keepdims=True)
    acc_new = alpha * acc + _dot(p.astype(BF16), v)
    return m_new, l_new, acc_new


def _fox_body(q_ref, k_ref, v_ref, fk_ref, j0_ref, o_ref, q_s,
              *, tq, tk, q_pos0, n_kb_total, npp):
    row0 = q_pos0 + pl.program_id(2) * tq
    lane = lax.broadcasted_iota(I32, (1, LANES), 1)
    lo_half = lane < HEAD_DIM
    for pp in range(npp):
        q = q_ref[0, :, pp * LANES:(pp + 1) * LANES]
        zero = jnp.zeros_like(q)
        q_s[2 * pp] = jnp.where(lo_half, q, zero)
        q_s[2 * pp + 1] = jnp.where(lo_half, zero, q)
    qpos = row0 + lax.broadcasted_iota(I32, (tq, 1), 0)
    nkb = jnp.minimum((row0 + tq + tk - 1) // tk, n_kb_total)
    n_full = jnp.minimum((row0 + 1) // tk, nkb)

    def body(j, carry, masked, width):
        ks = pl.multiple_of(j * tk, tk)
        out = []
        for h in range(2 * npp):
            pp, half = divmod(h, 2)
            k = k_ref[0, pl.ds(ks, width), pp * LANES:(pp + 1) * LANES]
            v = v_ref[0, pl.ds(ks, width), pp * LANES:(pp + 1) * LANES]
            m, l, acc = carry[h]
            fk = fk_ref[0, pp, half:half + 1, pl.ds(ks, width)] * LOG2E
            s = _dot_nt(q_s[h], k) - fk
            if masked:
                vis = (ks + lax.broadcasted_iota(I32, (1, width), 1)) <= qpos
                s = jnp.where(vis, s, NEG_BIG)
            out.append(_softmax_step(s, v, m, l, acc))
        return tuple(out)

    j0 = jnp.minimum(j0_ref[pl.program_id(0), pl.program_id(1), pl.program_id(2)], n_full)

    init = (jnp.full((tq, 1), M_INIT, F32), jnp.zeros((tq, 1), F32), jnp.zeros((tq, LANES), F32))
    n_wide = (n_full - j0) // 2
    carry = lax.fori_loop(0, n_wide,
                          lambda j, c: body(j0 + 2 * j, c, masked=False, width=2 * tk),
                          (init,) * (2 * npp))
    carry = lax.fori_loop(j0 + 2 * n_wide, n_full,
                          functools.partial(body, masked=False, width=tk), carry)
    carry = lax.fori_loop(n_full, nkb, functools.partial(body, masked=True, width=tk), carry)
    for pp in range(npp):
        (_, l0, a0), (_, l1, a1) = carry[2 * pp], carry[2 * pp + 1]
        o_ref[0, :, pp * LANES:(pp + 1) * LANES] = jnp.where(lo_half, a0 / l0, a1 / l1).astype(BF16)


def _fox_first_block(cum, decay_cut, q_pos0, t, tq, tk, npp):
    b, l, h = cum.shape
    f2 = cum * LOG2E
    f_tile = jnp.max(f2[:, q_pos0:q_pos0 + t].reshape(b, t // tq, tq, h), axis=2)
    gone = (f2[:, None] - f_tile[:, :, None]) > decay_cut[:, None, None, :]
    first_needed = jnp.min(jnp.where(gone, l, jnp.arange(l, dtype=I32)[None, None, :, None]), axis=2)
    j0 = jnp.min((first_needed // tk).reshape(b, t // tq, h // (2 * npp), 2 * npp), axis=-1)
    return j0.transpose(0, 2, 1).astype(I32)


def _fox_attention(q, q_cb, k, k_cb, v, v_cb, fk, j0, q_pos0, tq, tk):
    b, t = q.shape[:2]
    l = k.shape[1]
    n_pairs = H_FOX // 2
    npp = FOX_PAIRS
    w = npp * LANES
    assert q_cb % npp == 0 and k_cb % npp == 0 and v_cb % npp == 0
    return pl.pallas_call(
        functools.partial(_fox_body, tq=tq, tk=tk, q_pos0=q_pos0, n_kb_total=l // tk, npp=npp),
        grid=(b, n_pairs // npp, t // tq),
        in_specs=[pl.BlockSpec((1, tq, w), lambda bi, p, i: (bi, i, q_cb // npp + p)),
                  pl.BlockSpec((1, l, w), lambda bi, p, i: (bi, 0, k_cb // npp + p)),
                  pl.BlockSpec((1, l, w), lambda bi, p, i: (bi, 0, v_cb // npp + p)),
                  pl.BlockSpec((1, npp, 2, l), lambda bi, p, i: (bi, p, 0, 0)),
                  pl.BlockSpec(memory_space=pltpu.SMEM)],
        out_specs=pl.BlockSpec((1, tq, w), lambda bi, p, i: (bi, i, p)),
        out_shape=jax.ShapeDtypeStruct((b, t, n_pairs * LANES), BF16),
        scratch_shapes=[pltpu.VMEM((2 * npp, tq, LANES), BF16)],
        compiler_params=_params(("parallel", "parallel", "parallel")),
        name="fox_attention",
    )(q, k, v, fk, j0)


def _forget_body(raw_ref, b_ref, logf_ref, cum_ref, carry_ref, *, tb, n_given):
    j = pl.program_id(1)

    @pl.when(j == 0)
    def _():
        carry_ref[...] = jnp.zeros_like(carry_ref)

    raw = raw_ref[0]
    z = raw + b_ref[...]
    computed = _log_sigmoid_neg(-z)
    row = j * tb + lax.broadcasted_iota(I32, (tb, 1), 0)
    logf = jnp.where(row < n_given, raw, computed)
    incl = (lax.broadcasted_iota(I32, (tb, tb), 1) <=
            lax.broadcasted_iota(I32, (tb, tb), 0)).astype(BF16)
    hi, mid, lo = _split3(logf)
    cum = _dot(incl, hi) + _dot(incl, mid) + _dot(incl, lo) + carry_ref[0:1, :]
    logf_ref[0] = logf
    cum_ref[0] = cum
    carry_ref[0:1, :] = cum[tb - 1:tb, :]


def _forget_cumsum(raw, col_block, bias, n_given, tb):
    bias = jnp.pad(bias, (0, LANES - bias.shape[0]))
    b, l = raw.shape[:2]
    h = LANES
    return pl.pallas_call(
        functools.partial(_forget_body, tb=tb, n_given=n_given),
        grid=(b, l // tb),
        in_specs=[pl.BlockSpec((1, tb, h), lambda bi, j: (bi, j, col_block)),
                  pl.BlockSpec((1, h), lambda bi, j: (0, 0))],
        out_specs=[pl.BlockSpec((1, tb, h), lambda bi, j: (bi, j, 0)),
                   pl.BlockSpec((1, tb, h), lambda bi, j: (bi, j, 0))],
        out_shape=[jax.ShapeDtypeStruct((b, l, h), F32), jax.ShapeDtypeStruct((b, l, h), F32)],
        scratch_shapes=[pltpu.VMEM((8, h), F32)],
        compiler_params=_params(("parallel", "arbitrary")),
        name="forget_cumsum",
    )(raw, bias.reshape(1, h))


def _dsa_body(q_ref, qi_ref, wi_ref, k_ref, v_ref, ki_ref, o_ref,
              key_s, bias_s, qi_s, q_s, plane_s, *, tq, tk, q_pos0, l_valid, n_kb_total, n_sel):
    row0 = q_pos0 + pl.program_id(1) * tq
    lane = lax.broadcasted_iota(I32, (1, LANES), 1)
    lo_half = lane < HEAD_DIM
    qpos = row0 + lax.broadcasted_iota(I32, (tq, 1), 0)
    qchunk = qpos // CHUNK
    kend = jnp.minimum(((row0 + tq - 1) // CHUNK + 1) * CHUNK, l_valid)
    nkb = jnp.minimum(jnp.maximum((kend + tk - 1) // tk, (n_sel + tk - 1) // tk), n_kb_total)

    n_stack = IDX_HEADS // 2
    for p in range(n_stack):
        blk = qi_ref[0, :, p * LANES:(p + 1) * LANES]
        zero = jnp.zeros_like(blk)
        qi_s[0, p * tq:(p + 1) * tq, :] = jnp.where(lo_half, blk, zero)
        qi_s[1, p * tq:(p + 1) * tq, :] = jnp.where(lo_half, zero, blk)
    wsc = wi_ref[0] * (IDX_DIM ** -0.5 * IDX_HEADS ** -0.5)

    def wide_then_single(step, init):
        carry = lax.fori_loop(0, nkb // 2, lambda j, c: step(2 * j, c, 2 * tk), init)
        return lax.fori_loop(nkb // 2 * 2, nkb, lambda j, c: step(j, c, tk), carry)

    def score_body(j, _, width):
        for u in range(width // SCORE_SUB):
            ks = pl.multiple_of(j * tk + u * SCORE_SUB, SCORE_SUB)
            ki = ki_ref[0, pl.ds(ks, SCORE_SUB), :]
            score = jnp.zeros((tq, SCORE_SUB), F32)
            for half in range(2):
                rel = jnp.maximum(_dot_nt(qi_s[half], ki), 0.0)
                for p in range(n_stack):
                    h = 2 * p + half
                    score = score + wsc[:, h:h + 1] * rel[p * tq:(p + 1) * tq]
            kpos = ks + lax.broadcasted_iota(I32, (1, SCORE_SUB), 1)
            vis = ((kpos // CHUNK) <= qchunk) & (kpos < l_valid)
            score = jnp.where(vis, score, -jnp.inf)
            bits = lax.bitcast_convert_type(score, I32)
            key = bits ^ ((bits >> 31) & INT_MAX)
            key_s[:, pl.ds(ks, SCORE_SUB)] = jnp.where(bits == INT_MIN, 0, key)
        return 0

    wide_then_single(score_body, 0)

    n_sel_f = float(n_sel)
    cr = min(COUNT_ROWS, tq)

    slabs = [slice(r * cr, (r + 1) * cr) for r in range(tq // cr)]

    def count(rs, pred):
        def cbody(j, acc):
            blk = key_s[rs, pl.ds(pl.multiple_of(j * tk, tk), tk)]
            hit = jnp.where(pred(blk, j), 1.0, 0.0)
            for c in range(tk // LANES):
                acc = acc + hit[:, c * LANES:(c + 1) * LANES]
            return acc
        acc = lax.fori_loop(0, nkb, cbody, jnp.zeros((cr, LANES), F32))
        return jnp.sum(acc, axis=1, keepdims=True)

    n_grp_total = n_kb_total * tk // BIT_GROUP
    n_bgrp = jnp.minimum((nkb * tk + BIT_GROUP - 1) // BIT_GROUP, n_grp_total)

    def fill_body(j, _):
        key_s[:, pl.ds(pl.multiple_of(j * tk, tk), tk)] = jnp.full((tq, tk), KEY_NEG_INF, I32)
        return 0

    lax.fori_loop(nkb, n_bgrp * (BIT_GROUP // tk), fill_body, 0)

    def plane_body(it, _):
        g, r = it // (tq // 8), it % (tq // 8)
        rows = pl.ds(pl.multiple_of(r * 8, 8), 8)
        base = pl.multiple_of(g * BIT_GROUP, BIT_GROUP)
        a = [key_s[rows, pl.ds(base + c * LANES, LANES)] ^ INT_MIN for c in range(32)]
        j, m = 16, 0x0000FFFF
        while j:
            k = 0
            while k < 32:
                t = (a[k] ^ lax.shift_right_logical(a[k + j], jnp.full((8, LANES), j, I32))) & m
                a[k] = a[k] ^ t
                a[k + j] = a[k + j] ^ (t << j)
                k = (k + j + 1) & ~j
            j >>= 1
            m = m ^ (m << j)
        for i in range(32):
            plane_s[i, rows, pl.ds(pl.multiple_of(g * LANES, LANES), LANES)] = a[i]
        return 0

    lax.fori_loop(0, n_bgrp * (tq // 8), plane_body, 0)

    plane_w = n_grp_total * LANES
    in_use = (lax.broadcasted_iota(I32, (1, plane_w), 1) // LANES) < n_bgrp

    def popsum(words):
        return jnp.sum(lax.population_count(words).astype(F32), axis=1, keepdims=True)

    def bit_pass(i, state):
        alive, above, thr_u = state
        p1, p2 = plane_s[2 * i], plane_s[2 * i + 1]
        a1, a0 = alive & p1, alive & ~p1
        a11, a10, a01, a00 = a1 & p2, a1 & ~p2, a0 & p2, a0 & ~p2
        n11, n10, n01 = popsum(a11), popsum(a10), popsum(a01)
        keep1 = above + n11 + n10 >= n_sel_f
        above1 = jnp.where(keep1, above, above + n11 + n10)
        n_hi = jnp.where(keep1, n11, n01)
        keep2 = above1 + n_hi >= n_sel_f
        above = jnp.where(keep2, above1, above1 + n_hi)
        alive = jnp.where(keep1, jnp.where(keep2, a11, a10), jnp.where(keep2, a01, a00))
        top = lax.shift_right_logical(jnp.full((tq, 1), INT_MIN, I32), jnp.full((tq, 1), 2 * i, I32))
        thr_u = thr_u | jnp.where(keep1, top, 0) | jnp.where(keep2, lax.shift_right_logical(
            top, jnp.ones((tq, 1), I32)), 0)
        return alive, above, thr_u

    alive0 = jnp.where(in_use, -1, 0) + jnp.zeros((tq, plane_w), I32)
    alive, above, thr_u = lax.fori_loop(
        0, 16, bit_pass, (alive0, jnp.zeros((tq, 1), F32), jnp.zeros((tq, 1), I32)))
    thr_all = thr_u ^ INT_MIN
    c_ge_all = above + jnp.sum(lax.population_count(alive).astype(F32), axis=1, keepdims=True)
    thrs = [thr_all[rs] for rs in slabs]
    c_ges = [c_ge_all[rs] for rs in slabs]

    for rs, thr, c_ge in zip(slabs, thrs, c_ges):
        tied = (c_ge > n_sel_f) & (thr > KEY_NEG_INF)
        any_tied = jnp.max(jnp.where(tied, 1.0, 0.0)) > 0.0

        def tie_index_bound(rs=rs, thr=thr, tied=tied):
            need = n_sel_f - count(rs, lambda blk, j: blk > thr)

            def kidx(j):
                return j * tk + lax.broadcasted_iota(I32, (1, tk), 1)

            n_bits = (n_kb_total * tk - 1).bit_length()

            def jbody(b, jb):
                cand = jb | (1 << (n_bits - 1 - b))
                cnt = count(rs, lambda blk, j: (blk == thr) & (kidx(j) < cand))
                return jnp.where(cnt < need, cand, jb)

            jb = lax.fori_loop(0, n_bits, jbody, jnp.zeros((cr, 1), I32))
            return jnp.where(tied, jb, INT_MAX)

        jbound = lax.cond(any_tied, tie_index_bound, lambda: jnp.full((cr, 1), INT_MAX, I32))

        def bias_body(j, _, rs=rs, thr=thr, jbound=jbound):
            ks = pl.multiple_of(j * tk, tk)
            blk = key_s[rs, pl.ds(ks, tk)]
            kpos = ks + lax.broadcasted_iota(I32, (1, tk), 1)
            sel = (blk > thr) | ((blk == thr) & (kpos <= jbound))
            sel = sel & (blk > KEY_NEG_INF)
            bias_s[rs, pl.ds(ks, tk)] = jnp.where(sel, 0.0, NEG_BIG)
            return 0

        lax.fori_loop(0, nkb, bias_body, 0)

    n_grp = H_DSA // KV_DSA
    for p in range(n_grp):
        blk = q_ref[0, :, p * LANES:(p + 1) * LANES]
        zero = jnp.zeros_like(blk)
        q_s[0, p * tq:(p + 1) * tq, :] = jnp.where(lo_half, blk, zero)
        q_s[1, p * tq:(p + 1) * tq, :] = jnp.where(lo_half, zero, blk)

    def att_body(j, carry, width):
        ks = pl.multiple_of(j * tk, tk)
        k = k_ref[0, pl.ds(ks, width), :]
        v = v_ref[0, pl.ds(ks, width), :]
        bias = bias_s[:, pl.ds(ks, width)]
        bias = jnp.concatenate([bias] * n_grp, axis=0)
        return tuple(_softmax_step(_dot_nt(q_s[half], k) + bias, v, *carry[half])
                     for half in range(2))

    rows = n_grp * tq
    init = (jnp.full((rows, 1), M_INIT, F32), jnp.zeros((rows, 1), F32),
            jnp.zeros((rows, LANES), F32))
    (_, l0, a0), (_, l1, a1) = wide_then_single(att_body, (init, init))
    o0, o1 = a0 / l0, a1 / l1
    for p in range(n_grp):
        o_ref[0, :, p * LANES:(p + 1) * LANES] = jnp.where(
            lo_half, o0[p * tq:(p + 1) * tq], o1[p * tq:(p + 1) * tq]).astype(BF16)


def _dsa_attention(q, qi, qi_cb, wi, wi_cb, k, k_cb, v, v_cb, ki, ki_cb,
                   q_pos0, l_valid, n_sel, tq, tk):
    b, t = q.shape[:2]
    l = k.shape[1]
    width = H_DSA * HEAD_DIM
    return pl.pallas_call(
        functools.partial(_dsa_body, tq=tq, tk=tk, q_pos0=q_pos0, l_valid=l_valid,
                          n_kb_total=l // tk, n_sel=n_sel),
        grid=(b, t // tq),
        in_specs=[pl.BlockSpec((1, tq, width), lambda bi, i: (bi, i, 0)),
                  pl.BlockSpec((1, tq, width), lambda bi, i: (bi, i, qi_cb)),
                  pl.BlockSpec((1, tq, LANES), lambda bi, i: (bi, i, wi_cb)),
                  pl.BlockSpec((1, l, LANES), lambda bi, i: (bi, 0, k_cb)),
                  pl.BlockSpec((1, l, LANES), lambda bi, i: (bi, 0, v_cb)),
                  pl.BlockSpec((1, l, LANES), lambda bi, i: (bi, 0, ki_cb))],
        out_specs=pl.BlockSpec((1, tq, width), lambda bi, i: (bi, i, 0)),
        out_shape=jax.ShapeDtypeStruct((b, t, width), BF16),
        scratch_shapes=[pltpu.VMEM((tq, l), I32), pltpu.VMEM((tq, l), F32),
                        pltpu.VMEM((2, IDX_HEADS // 2 * tq, LANES), BF16),
                        pltpu.VMEM((2, H_DSA // KV_DSA * tq, LANES), BF16),
                        pltpu.VMEM((32, tq, l // BIT_GROUP * LANES), I32)],
        compiler_params=_params(("parallel", "parallel")),
        name="dsa_attention",
    )(q, qi, wi, k, v, ki)


def _moe_body(h_ref, g_ref, wr_ref, w1_ref, w3_ref, w2_ref, o_ref,
              xn_s, xg_s, ye_s, rank_s, gate_s, rank_t_s, *, tm, ch, n_fc):
    e = pl.program_id(1)
    fc = pl.program_id(2)
    lane = lax.broadcasted_iota(I32, (1, LANES), 1)

    @pl.when((e == 0) & (fc == 0))
    def _route():
        x = h_ref[...]
        xn = _rms(x, g_ref[...])
        xn_s[...] = xn.astype(BF16)
        o_ref[...] = x
        x3 = _split3(xn)
        logits = jnp.zeros((tm, LANES), F32)
        for a, b in ((2, 0), (0, 2), (1, 1), (1, 0), (0, 1), (0, 0)):
            logits = logits + _dot(x3[a], wr_ref[b])
        lane_f = lane.astype(F32)
        logits = jnp.where(lane < N_EXPERTS, logits, -jnp.inf)
        m1 = jnp.max(logits, axis=1, keepdims=True)
        i1 = jnp.min(jnp.where(logits == m1, lane_f, float(LANES)), axis=1, keepdims=True)
        rest = jnp.where(lane_f == i1, -jnp.inf, logits)
        m2 = jnp.max(rest, axis=1, keepdims=True)
        i2 = jnp.min(jnp.where(rest == m2, lane_f, float(LANES)), axis=1, keepdims=True)
        e2 = jnp.exp(m2 - m1)
        g1 = 1.0 / (1.0 + e2)
        g2 = e2 / (1.0 + e2)
        sel1 = lane_f == i1
        sel2 = lane_f == i2
        gate_s[...] = jnp.where(sel1, g1, 0.0) + jnp.where(sel2, g2, 0.0)
        sel = jnp.where(sel1 | sel2, 1.0, 0.0)
        incl = (lax.broadcasted_iota(I32, (tm, tm), 1) <=
                lax.broadcasted_iota(I32, (tm, tm), 0)).astype(BF16)
        rank = _dot(incl, sel.astype(BF16)) * sel
        rank_s[...] = rank
        rank_t_s[...] = rank.T

    rank_row = rank_t_s[pl.ds(e, 1), :]
    cnt = jnp.max(rank_row).astype(I32)
    nch = (cnt + ch - 1) // ch

    @pl.when(fc == 0)
    def _gather():
        def gbody(c, _):
            base = pl.multiple_of(c * ch, ch)
            slot = (base + 1 + lax.broadcasted_iota(I32, (ch, 1), 0)).astype(F32)
            pick = jnp.where(rank_row == slot, 1.0, 0.0).astype(BF16)
            xg_s[pl.ds(base, ch), :] = _dot(pick, xn_s[...]).astype(BF16)
            return 0
        lax.fori_loop(0, nch, gbody, 0)

    def fbody(c, _):
        base = pl.multiple_of(c * ch, ch)
        xg = xg_s[pl.ds(base, ch), :]
        a = _dot(xg, w1_ref[0])
        b = _dot(xg, w3_ref[0])
        part = _dot((a * _sigmoid(a) * b).astype(BF16), w2_ref[0])

        @pl.when(fc == 0)
        def _():
            ye_s[pl.ds(base, ch), :] = part

        @pl.when(fc != 0)
        def _():
            ye_s[pl.ds(base, ch), :] += part
        return 0

    lax.fori_loop(0, nch, fbody, 0)

    @pl.when(fc == n_fc - 1)
    def _scatter():
        here = lane == e
        rank_col = jnp.sum(jnp.where(here, rank_s[...], 0.0), axis=1, keepdims=True)
        gate_col = jnp.sum(jnp.where(here, gate_s[...], 0.0), axis=1, keepdims=True)

        def sbody(c, _):
            base = pl.multiple_of(c * ch, ch)
            slot = (base + 1 + lax.broadcasted_iota(I32, (1, ch), 1)).astype(F32)
            place = jnp.where(rank_col == slot, 1.0, 0.0).astype(BF16)
            ye = ye_s[pl.ds(base, ch), :].astype(BF16)
            o_ref[...] += gate_col * _dot(place, ye)
            return 0
        lax.fori_loop(0, nch, sbody, 0)


def _moe(h, g, wr3, w1, w3, w2, tm, ch, tf):
    n, d = h.shape
    ne, _, f = w1.shape
    n_fc = f // tf
    return pl.pallas_call(
        functools.partial(_moe_body, tm=tm, ch=ch, n_fc=n_fc),
        grid=(n // tm, ne, n_fc),
        in_specs=[pl.BlockSpec((tm, d), lambda i, e, c: (i, 0), pipeline_mode=pl.Buffered(1)),
                  pl.BlockSpec((1, d), lambda i, e, c: (0, 0)),
                  pl.BlockSpec((3, d, LANES), lambda i, e, c: (0, 0, 0)),
                  pl.BlockSpec((1, d, tf), lambda i, e, c: (e, 0, c)),
                  pl.BlockSpec((1, d, tf), lambda i, e, c: (e, 0, c)),
                  pl.BlockSpec((1, tf, d), lambda i, e, c: (e, c, 0))],
        out_specs=pl.BlockSpec((tm, d), lambda i, e, c: (i, 0)),
        out_shape=jax.ShapeDtypeStruct((n, d), F32),
        scratch_shapes=[pltpu.VMEM((tm, d), BF16), pltpu.VMEM((_round_up(tm, ch), d), BF16),
                        pltpu.VMEM((_round_up(tm, ch), d), F32), pltpu.VMEM((tm, LANES), F32),
                        pltpu.VMEM((tm, LANES), F32), pltpu.VMEM((LANES, tm), F32)],
        compiler_params=_params(("parallel", "arbitrary", "arbitrary")),
        name="moe",
    )(h, g.reshape(1, d), wr3, w1, w3, w2)


def _round_up(x, m):
    return (x + m - 1) // m * m


def _pad_rows(a, l_pad):
    return jnp.pad(a, ((0, 0), (0, l_pad - a.shape[1]), (0, 0)))


def _prep_weights(g_q_dsa, g_k_dsa, g_q_fox, g_k_fox, w_in_even, w_out_even, w_in_odd, w_router):
    d = w_in_even.shape[0]
    perm = jnp.asarray(DSA_HEAD_PERM)
    qa, ka, va, qb, kb, vb, qi, ki, wi = jnp.split(
        w_in_even, [512, 1024, 1536, 2048, 2176, 2304, 2816, 2880], axis=1)
    qb = qb.reshape(d, H_DSA, HEAD_DIM)[:, perm].reshape(d, H_DSA * HEAD_DIM)
    wi = jnp.pad(wi, ((0, 0), (0, LANES - IDX_HEADS)))
    w_even = jnp.concatenate([qb, kb, vb, ki, ki, wi, qi, qa, ka, va], axis=1).astype(BF16)
    ones = lambda k: jnp.ones((k,), F32)
    q_scale = lambda k: jnp.full((k,), Q_SCALE, F32)
    scale_even = jnp.concatenate([ones(E_QA), q_scale(E_KA - E_QA), ones(E_END - E_KA)])
    scale_odd = ones(O_END)
    scale_norm_even = jnp.concatenate([q_scale(E_KB - E_QB), ones(E_VB - E_KB)])
    scale_norm_odd = jnp.concatenate([q_scale(O_K - O_Q), ones(O_V - O_K)])
    w_out_sb = w_out_even[:H_SB * HEAD_DIM].astype(BF16)
    w_out_dsa = w_out_even[H_SB * HEAD_DIM:].reshape(H_DSA, HEAD_DIM, d)[perm]
    w_out_dsa = w_out_dsa.reshape(H_DSA * HEAD_DIM, d).astype(BF16)
    w_odd = jnp.pad(w_in_odd, ((0, 0), (0, O_END - w_in_odd.shape[1]))).astype(BF16)
    gains_even = jnp.concatenate([jnp.tile(g_q_dsa, H_DSA), jnp.tile(g_k_dsa, KV_DSA)])
    gains_odd = jnp.concatenate([jnp.tile(g_q_fox, H_FOX), jnp.tile(g_k_fox, H_FOX)])
    wr = jnp.pad(w_router, ((0, 0), (0, LANES - N_EXPERTS)))
    wr_hi = wr.astype(BF16)
    wr_r1 = wr - wr_hi.astype(F32)
    wr_mid = wr_r1.astype(BF16)
    wr_lo = (wr_r1 - wr_mid.astype(F32)).astype(BF16)
    return dict(w_even=w_even, w_out_sb=w_out_sb, w_out_dsa=w_out_dsa, w_odd=w_odd,
                gains_even=gains_even, gains_odd=gains_odd,
                scale_even=scale_even, scale_odd=scale_odd,
                scale_norm_even=scale_norm_even, scale_norm_odd=scale_norm_odd,
                wr3=jnp.stack([wr_hi, wr_mid, wr_lo]))


def _rotary_tables(pos):
    half = ROT_DIM // 2
    inv = ROPE_THETA ** (-jnp.arange(half, dtype=F32) / half)
    ang = pos.astype(F32)[:, None] * inv[None, :]
    cos, sin = jnp.cos(ang), jnp.sin(ang)
    t = pos.shape[0]
    pad = HEAD_DIM - ROT_DIM
    cos_h = jnp.concatenate([cos, cos, jnp.ones((t, pad), F32)], axis=1)
    sin_h = jnp.concatenate([-sin, sin, jnp.zeros((t, pad), F32)], axis=1)
    return jnp.tile(cos_h, (1, 2)), jnp.tile(sin_h, (1, 2))


def _trunk(x, p, pos0, past_even, past_odd, keep, w, wb):
    b, t, d = x.shape
    n = b * t
    tm = min(ROW_TILE, n)
    has_past = past_even is not None
    past_len = past_even[0].shape[1] if has_past else 0
    l_valid = past_len + t
    l_pad = _round_up(l_valid, KEY_PAD)
    n_sel = min(DSA_TOPK, l_valid // 4)
    h = x.reshape(n, d)
    assert n % tm == 0 and (t % tm == 0 or tm % t == 0), (n, t, tm)
    assert all(t % min(tile, t) == 0 for tile in (SB_Q_TILE, FOX_Q_TILE, DSA_Q_TILE)), t
    assert has_past or t % KEY_PAD == 0, "without a cache the keys are this step's rows, unpadded"
    assert n % min(MOE_TILE, n) == 0 and n_sel <= l_pad

    def keys(cache, new):
        if not has_past:
            return new
        full = jnp.concatenate([cache.reshape(b, past_len, -1).astype(BF16), new], axis=1)
        return _pad_rows(full, l_pad)

    cos, sin = _rotary_tables(pos0 + jnp.arange(t, dtype=I32))
    if t % tm:
        cos, sin = jnp.tile(cos, (tm // t, 1)), jnp.tile(sin, (tm // t, 1))
    pf, pb, qkf, qkb = _proj(h, w["g_mix"][0], wb["w_even"], wb["scale_even"],
                             wb["gains_even"], wb["scale_norm_even"], cos, sin, 0, tm)
    pf3, pb3 = pf.reshape(b, t, E_END), pb.reshape(b, t, E_END)
    qkf3, qkb3 = qkf.reshape(b, t, E_NORM), qkb.reshape(b, t, E_NORM)
    cols = lambda a, c0, width: a[:, :, c0:c0 + width]
    sb_w, kv_w = H_SB * HEAD_DIM, KV_DSA * HEAD_DIM
    if has_past:
        c_sbk, c_sbv, c_dk, c_dv, c_ki = past_even
        ka, ka_cb = keys(c_sbk, cols(pb3, E_KA, sb_w)), 0
        va, va_cb = keys(c_sbv, cols(pb3, E_VA, sb_w)), 0
        kb, kb_cb = keys(c_dk, cols(qkb3, E_KB, kv_w)), 0
        vb, vb_cb = keys(c_dv, cols(pb3, E_VB, kv_w)), 0
        ki2 = jnp.concatenate([c_ki, c_ki], axis=-1)
        ki, ki_cb = keys(ki2, cols(pb3, E_KI, LANES)), 0
    else:
        ka, ka_cb = pb3, E_KA // LANES
        va, va_cb = pb3, E_VA // LANES
        kb, kb_cb = qkb3, E_KB // LANES
        vb, vb_cb = pb3, E_VB // LANES
        ki, ki_cb = pb3, E_KI // LANES
    oa = _sb_attention(pb3, E_QA // LANES, ka, ka_cb, va, va_cb, H_SB // 2, pos0,
                       min(SB_Q_TILE, t), SB_K_TILE)
    ob = _dsa_attention(qkb3, pb3, E_QI // 512, pf3, E_WI // LANES, kb, kb_cb, vb, vb_cb,
                        ki, ki_cb, pos0, l_valid, n_sel, min(DSA_Q_TILE, t), DSA_K_TILE)
    h = _even_channel(h, oa.reshape(n, -1), ob.reshape(n, -1), wb["w_out_sb"], wb["w_out_dsa"],
                      w["g_ffn"][0], wb["w_ff1"], wb["w_ff3"], wb["w_ff2"],
                      w["g_ple"][0], wb["w_ple_gate"][0], p[0].reshape(n, -1), wb["w_ple_in"][0], tm)
    kept = lambda a, c0, width: a[:, t - keep:, c0:c0 + width]
    even_state = (kept(pf3, E_KA, sb_w).reshape(b, keep, H_SB, HEAD_DIM),
                  kept(pf3, E_VA, sb_w).reshape(b, keep, H_SB, HEAD_DIM),
                  kept(qkf3, E_KB, kv_w).reshape(b, keep, KV_DSA, HEAD_DIM),
                  kept(pf3, E_VB, kv_w).reshape(b, keep, KV_DSA, HEAD_DIM),
                  kept(pf3, E_KI, IDX_DIM))

    fox_w = H_FOX * HEAD_DIM
    pf, pb, qkf, qkb = _proj(h, w["g_mix"][1], wb["w_odd"], wb["scale_odd"],
                             wb["gains_odd"], wb["scale_norm_odd"], None, None, O_V, tm)
    r_f = O_F - O_V
    pf3, pb3 = pf.reshape(b, t, O_END - O_V), pb.reshape(b, t, O_END - O_V)
    qkf3, qkb3 = qkf.reshape(b, t, O_V), qkb.reshape(b, t, O_V)
    if has_past:
        c_fk, c_fv, c_lf = past_odd
        kf, kf_cb = keys(c_fk, cols(qkb3, O_K, fox_w)), 0
        vf, vf_cb = keys(c_fv, cols(pb3, 0, fox_w)), 0
        raw = _pad_rows(jnp.concatenate([c_lf, cols(pf3, r_f, H_FOX)], axis=1), l_pad)
        raw, raw_cb = jnp.pad(raw, ((0, 0), (0, 0), (0, LANES - H_FOX))), 0
    else:
        kf, kf_cb = qkb3, O_K // LANES
        vf, vf_cb = pb3, 0
        raw, raw_cb = pf3, r_f // LANES
    logf, cum = _forget_cumsum(raw, raw_cb, w["b_forget"][0], past_len, min(FORGET_TILE, l_pad))
    logf, cum = logf[:, :, :H_FOX], cum[:, :, :H_FOX]
    fk = cum.reshape(b, l_pad, H_FOX // 2, 2).transpose(0, 2, 3, 1)

    normed_max = lambda g: HEAD_DIM ** 0.5 * jnp.max(jnp.abs(g))
    q_max = jnp.full((b, H_FOX), Q_SCALE * normed_max(w["g_q_fox"]), F32)
    if has_past:
        kx = kf.astype(F32).reshape(b, l_pad, H_FOX, HEAD_DIM)
        k_max = jnp.sqrt(jnp.max(jnp.sum(kx * kx, axis=-1), axis=1))
    else:
        k_max = jnp.full((b, H_FOX), normed_max(w["g_k_fox"]), F32)
    decay_cut = EXP2_ZERO + 2.0 * 1.02 * q_max * k_max
    fox_tq = min(FOX_Q_TILE, t)
    j0 = _fox_first_block(cum, decay_cut, pos0, t, fox_tq, FOX_K_TILE, FOX_PAIRS)
    of = _fox_attention(qkb3, O_Q // LANES, kf, kf_cb, vf, vf_cb, fk, j0, pos0, fox_tq, FOX_K_TILE)
    h = _residual_matmul(h, [of.reshape(n, -1)], [wb["w_out_odd"]], tm)
    h = _moe(h, w["g_ffn"][1], wb["wr3"], wb["w_exp1"], wb["w_exp3"], wb["w_exp2"],
             min(MOE_TILE, n), min(MOE_CHUNK, n), MOE_F_TILE)
    h = _ple(h, w["g_ple"][1], wb["w_ple_gate"][1], p[1].reshape(n, -1), wb["w_ple_in"][1], tm)
    odd_state = (kept(qkf3, O_K, fox_w).reshape(b, keep, H_FOX, HEAD_DIM),
                 kept(pf3, 0, fox_w).reshape(b, keep, H_FOX, HEAD_DIM),
                 logf[:, l_valid - keep:l_valid])
    return h.reshape(b, t, d), even_state, odd_state


def kernel(x_prompt, x_sample, p_prompt, p_sample, cache_sb_k, cache_sb_v, cache_dsa_k, cache_dsa_v, cache_dsa_kidx, cache_fox_k, cache_fox_v, cache_fox_logf, g_mix, g_ffn, g_ple, w_in_even, g_q_dsa, g_k_dsa, w_out_even, w_ff1, w_ff3, w_ff2, w_in_odd, b_forget, g_q_fox, g_k_fox, w_out_odd, w_router, w_exp1, w_exp3, w_exp2, w_ple_in, w_ple_gate):
    assert g_mix.shape[0] == 2, "two layers: one even (stick-breaking + DSA), one odd (FoX + experts)"
    past_len = cache_sb_k.shape[2]
    w = dict(g_mix=g_mix, g_ffn=g_ffn, g_ple=g_ple, b_forget=b_forget,
             g_q_fox=g_q_fox[0], g_k_fox=g_k_fox[0])
    wb = _prep_weights(g_q_dsa[0], g_k_dsa[0], g_q_fox[0], g_k_fox[0],
                       w_in_even[0], w_out_even[0], w_in_odd[0], w_router[0])
    wb.update(w_ff1=w_ff1[0].astype(BF16), w_ff3=w_ff3[0].astype(BF16), w_ff2=w_ff2[0].astype(BF16),
              w_out_odd=w_out_odd[0].astype(BF16),
              w_exp1=w_exp1[0].astype(BF16), w_exp3=w_exp3[0].astype(BF16),
              w_exp2=w_exp2[0].astype(BF16),
              w_ple_in=w_ple_in.astype(BF16), w_ple_gate=w_ple_gate.astype(BF16))

    y_p, even_p, odd_p = _trunk(x_prompt, p_prompt, 0, None, None,
                                min(x_prompt.shape[1], past_len), w, wb)
    y_s, even_s, odd_s = _trunk(
        x_sample, p_sample, past_len,
        (cache_sb_k[0], cache_sb_v[0], cache_dsa_k[0], cache_dsa_v[0], cache_dsa_kidx[0]),
        (cache_fox_k[0], cache_fox_v[0], cache_fox_logf[0]), x_sample.shape[1], w, wb)
    layer = lambda a: a[None]
    return (y_p, y_s,
            *(layer(a) for a in even_p), *(layer(a) for a in odd_p),
            *(layer(a) for a in even_s), *(layer(a) for a in odd_s))
```

```python
import functools

import jax
import jax.numpy as jnp
from jax import lax
from jax.experimental import pallas as pl
from jax.experimental.pallas import tpu as pltpu

F32 = jnp.float32
BF16 = jnp.bfloat16
I32 = jnp.int32

EPS = 1e-6
HEAD_DIM = 64
CHUNK = 64
H_SB = 8
H_DSA = 8
KV_DSA = 2
IDX_HEADS = 8
IDX_DIM = 64
H_FOX = 16
DSA_TOPK = 256
ROT_DIM = HEAD_DIM // 4
ROPE_THETA = 500000.0
N_EXPERTS = 8
ATT_SCALE = HEAD_DIM ** -0.5
LOG2E = 1.4426950408889634
Q_SCALE = ATT_SCALE * LOG2E

LANES = 128
MIB = 1024 * 1024
VMEM_LIMIT = 56 * MIB

ROW_TILE = 512
SB_Q_TILE, SB_K_TILE = 256, 256
SB_BLOCKS_PER_STEP = 1
EXP2_ZERO = 150.0
FOX_Q_TILE, FOX_K_TILE = 512, 512
FOX_PAIRS = 1
DSA_Q_TILE, DSA_K_TILE = 256, 512
COUNT_ROWS = 128
SCORE_SUB = 256
BIT_GROUP = 32 * LANES
KEY_PAD = BIT_GROUP
FORGET_TILE = 512
MOE_TILE = 1024
MOE_CHUNK = 288
MOE_F_TILE = 1792

NEG_BIG = -1e30
M_INIT = -1e29
INT_MIN = -(2 ** 31)
INT_MAX = 2 ** 31 - 1
KEY_NEG_INF = -2139095041

E_QB, E_KB, E_VB, E_KI, E_WI, E_QI, E_QA, E_KA, E_VA, E_END = (
    0, 512, 640, 768, 896, 1024, 1536, 2048, 2560, 3072)
E_NORM = E_VB - E_QB
O_Q, O_K, O_V, O_F, O_END = 0, 1024, 2048, 3072, 3200
DSA_HEAD_PERM = (0, 4, 1, 5, 2, 6, 3, 7)


def _params(sem, vmem=VMEM_LIMIT):
    return pltpu.CompilerParams(dimension_semantics=sem, vmem_limit_bytes=vmem)


def _dot(a, b):
    return jnp.dot(a, b, preferred_element_type=F32)


def _dot_nt(a, b):
    return lax.dot_general(a, b, (((1,), (1,)), ((), ())), preferred_element_type=F32)


def _rms(x, g):
    return x * lax.rsqrt(jnp.mean(x * x, axis=-1, keepdims=True) + EPS) * g


def _split3(x):
    hi = x.astype(BF16)
    r1 = x - hi.astype(F32)
    mid = r1.astype(BF16)
    lo = (r1 - mid.astype(F32)).astype(BF16)
    return hi, mid, lo


def _sigmoid(x):
    return 1.0 / (1.0 + jnp.exp(-x))


def _log_sigmoid_neg(z):
    return -(jnp.maximum(z, 0.0) + jnp.log(1.0 + jnp.exp(-jnp.abs(z))))


def _proj_body(x_ref, g_ref, w_ref, sc_ref, gn_ref, scn_ref, cos_ref, sin_ref, s_ref,
               of_ref, ob_ref, nf_ref, nb_ref, *, rotary, width, keep_from):
    y = _rms(x_ref[...], g_ref[...]).astype(BF16)
    r = _dot(y, w_ref[...])
    of_ref[...] = r[:, keep_from:]
    ob_ref[...] = (r[:, keep_from:] * sc_ref[:, keep_from:]).astype(BF16)
    s = s_ref[...]
    for j in range(width // LANES):
        cols = slice(j * LANES, (j + 1) * LANES)
        x = r[:, cols]
        hi, mid, lo = _split3(x * x)
        ms = (_dot(hi, s) + _dot(mid, s) + _dot(lo, s)) * (1.0 / HEAD_DIM)
        y = x * lax.rsqrt(ms + EPS) * gn_ref[:, cols]
        if rotary:
            lane = lax.broadcasted_iota(I32, (1, LANES), 1) % HEAD_DIM
            partner = jnp.where(lane < ROT_DIM // 2,
                                pltpu.roll(y, LANES - ROT_DIM // 2, 1),
                                pltpu.roll(y, ROT_DIM // 2, 1))
            y = y * cos_ref[...] + partner * sin_ref[...]
        nf_ref[:, cols] = y
        nb_ref[:, cols] = (y * scn_ref[:, cols]).astype(BF16)


def _proj(x, g, w, scales, gains_n, scales_n, cos, sin, keep_from, tm):
    n, d = x.shape
    c = w.shape[1]
    width = gains_n.shape[0]
    kept = c - keep_from
    rotary = cos is not None
    if not rotary:
        cos = jnp.zeros((8, LANES), F32)
        sin = cos
        tab_spec = pl.BlockSpec((8, LANES), lambda i: (0, 0))
    else:
        nt = cos.shape[0] // tm
        tab_spec = pl.BlockSpec((tm, LANES), lambda i: (i % nt, 0))
    rr = lax.broadcasted_iota(I32, (LANES, LANES), 0) // HEAD_DIM
    cc = lax.broadcasted_iota(I32, (LANES, LANES), 1) // HEAD_DIM
    seg = (rr == cc).astype(BF16)
    row = lambda width_: pl.BlockSpec((tm, width_), lambda i: (i, 0))
    vec = lambda width_: pl.BlockSpec((1, width_), lambda i: (0, 0))
    return pl.pallas_call(
        functools.partial(_proj_body, rotary=rotary, width=width, keep_from=keep_from),
        grid=(n // tm,),
        in_specs=[row(d), vec(d),
                  pl.BlockSpec((d, c), lambda i: (0, 0), pipeline_mode=pl.Buffered(1)),
                  vec(c), vec(width), vec(width), tab_spec, tab_spec,
                  pl.BlockSpec((LANES, LANES), lambda i: (0, 0))],
        out_specs=[row(kept), row(kept), row(width), row(width)],
        out_shape=[jax.ShapeDtypeStruct((n, kept), F32), jax.ShapeDtypeStruct((n, kept), BF16),
                   jax.ShapeDtypeStruct((n, width), F32), jax.ShapeDtypeStruct((n, width), BF16)],
        compiler_params=_params(("parallel",)),
        name="proj",
    )(x, g.reshape(1, d), w, scales.reshape(1, c), gains_n.reshape(1, width),
      scales_n.reshape(1, width), cos, sin, seg)


def _res_body(*refs, n_in):
    h_ref, o_ref = refs[0], refs[-1]
    acc = h_ref[...]
    for t in range(n_in):
        acc = acc + _dot(refs[1 + t][...], refs[1 + n_in + t][...])
    o_ref[...] = acc


def _residual_matmul(h, acts, ws, tm):
    n, d = h.shape
    n_in = len(acts)
    in_specs = [pl.BlockSpec((tm, d), lambda i: (i, 0))]
    in_specs += [pl.BlockSpec((tm, a.shape[1]), lambda i: (i, 0)) for a in acts]
    in_specs += [pl.BlockSpec(w.shape, lambda i: (0, 0)) for w in ws]
    return pl.pallas_call(
        functools.partial(_res_body, n_in=n_in),
        grid=(n // tm,),
        in_specs=in_specs,
        out_specs=pl.BlockSpec((tm, d), lambda i: (i, 0)),
        out_shape=jax.ShapeDtypeStruct((n, d), F32),
        compiler_params=_params(("parallel",)),
        name="residual_matmul",
    )(h, *acts, *ws)


def _ple_body(h_ref, g_ref, wg_ref, p_ref, wp_ref, o_ref):
    x = h_ref[...]
    xn = _rms(x, g_ref[...]).astype(BF16)
    gate = _sigmoid(_dot(xn, wg_ref[...]))
    o_ref[...] = x + gate * _dot(p_ref[...].astype(BF16), wp_ref[...])


def _ple(h, g, wg, p, wp, tm):
    n, d = h.shape
    e = p.shape[1]
    return pl.pallas_call(
        _ple_body,
        grid=(n // tm,),
        in_specs=[pl.BlockSpec((tm, d), lambda i: (i, 0)),
                  pl.BlockSpec((1, d), lambda i: (0, 0)),
                  pl.BlockSpec((d, d), lambda i: (0, 0)),
                  pl.BlockSpec((tm, e), lambda i: (i, 0)),
                  pl.BlockSpec((e, d), lambda i: (0, 0))],
        out_specs=pl.BlockSpec((tm, d), lambda i: (i, 0)),
        out_shape=jax.ShapeDtypeStruct((n, d), F32),
        compiler_params=_params(("parallel",)),
        name="ple",
    )(h, g.reshape(1, d), wg, p, wp)


def _even_channel_body(h_ref, oa_ref, ob_ref, wa_ref, wb_ref, gf_ref, w1_ref, w3_ref, w2_ref,
                       gp_ref, wg_ref, p_ref, wp_ref, o_ref):
    x = h_ref[...] + _dot(oa_ref[...], wa_ref[...]) + _dot(ob_ref[...], wb_ref[...])
    xn = _rms(x, gf_ref[...]).astype(BF16)
    a = _dot(xn, w1_ref[...])
    b = _dot(xn, w3_ref[...])
    x = x + _dot((a * _sigmoid(a) * b).astype(BF16), w2_ref[...])
    xn = _rms(x, gp_ref[...]).astype(BF16)
    gate = _sigmoid(_dot(xn, wg_ref[...]))
    o_ref[...] = x + gate * _dot(p_ref[...].astype(BF16), wp_ref[...])


def _even_channel(h, oa, ob, wa, wb, gf, w1, w3, w2, gp, wg, p, wp, tm):
    n, d = h.shape
    once = pl.Buffered(1)
    row = lambda a: pl.BlockSpec((tm, a.shape[1]), lambda i: (i, 0))
    vec = pl.BlockSpec((1, d), lambda i: (0, 0))
    full = lambda a: pl.BlockSpec(a.shape, lambda i: (0, 0), pipeline_mode=once)
    return pl.pallas_call(
        _even_channel_body,
        grid=(n // tm,),
        in_specs=[row(h), row(oa), row(ob), full(wa), full(wb), vec, full(w1), full(w3), full(w2),
                  vec, full(wg), row(p), full(wp)],
        out_specs=row(h),
        out_shape=jax.ShapeDtypeStruct((n, d), F32),
        compiler_params=_params(("parallel",)),
        name="even_channel",
    )(h, oa, ob, wa, wb, gf.reshape(1, d), w1, w3, w2, gp.reshape(1, d), wg, p, wp)


def _sb_body(q_ref, k_ref, v_ref, later_ref, o_ref, *, tq, tk, q_pos0, n_kb_total):
    row0 = q_pos0 + pl.program_id(2) * tq
    q = q_ref[0]
    lane = lax.broadcasted_iota(I32, (1, LANES), 1)
    lo_half = lane < HEAD_DIM
    zero = jnp.zeros_like(q)
    q_halves = (jnp.where(lo_half, q, zero), jnp.where(lo_half, zero, q))
    qpos = row0 + lax.broadcasted_iota(I32, (tq, 1), 0)
    nkb = jnp.minimum((row0 + tq - 1 + tk - 1) // tk, n_kb_total)
    n_full = jnp.minimum(row0 // tk, nkb)
    later = later_ref[...]

    def block(half, kb, c, masked):
        ks = pl.multiple_of(kb * tk, tk)
        z = _dot_nt(q_halves[half], k_ref[0, pl.ds(ks, tk), :])
        sp = jnp.maximum(z, 0.0) + jnp.log2(1.0 + jnp.exp2(-jnp.abs(z)))
        if masked:
            vis = (ks + lax.broadcasted_iota(I32, (1, tk), 1)) < qpos
            sp = jnp.where(vis, sp, 0.0)
        between = _dot(sp.astype(BF16), later)
        w = jnp.exp2(z - sp - between - c)
        if masked:
            w = jnp.where(vis, w, 0.0)
        row_sum = between[:, 0:1] + sp[:, 0:1]
        return _dot(w.astype(BF16), v_ref[0, pl.ds(ks, tk), :]), row_sum

    def run(kbs, carry, masked):
        accs, cs = carry
        new_accs, new_cs = [], []
        for half in range(2):
            acc, c = accs[half], cs[half]
            for kb in kbs:
                pv, rs = block(half, kb, c, masked)
                acc, c = acc + pv, c + rs
            new_accs.append(acc)
            new_cs.append(c)
        return tuple(new_accs), tuple(new_cs)

    acc0 = jnp.zeros((tq, LANES), F32)
    c0 = jnp.zeros((tq, 1), F32)
    carry = ((acc0, acc0), (c0, c0))
    carry = lax.fori_loop(0, nkb - n_full,
                          lambda j, cr: run([nkb - 1 - j], cr, True), carry)
    def live(cr):
        return jnp.minimum(jnp.min(cr[1][0]), jnp.min(cr[1][1])) < EXP2_ZERO

    def steps(n_steps, kbs_of, carry):
        def cond(state):
            j, _, more = state
            return (j < n_steps) & more

        def body(state):
            j, cr, _ = state
            cr = run(kbs_of(j), cr, False)
            return j + 1, cr, live(cr)

        return lax.while_loop(cond, body, (0, carry, live(carry)))[1]

    grp = SB_BLOCKS_PER_STEP
    carry = steps(n_full // grp, lambda j: [n_full - 1 - grp * j - u for u in range(grp)], carry)
    rem = n_full % grp
    carry = steps(rem, lambda j: [rem - 1 - j], carry)
    accs, _ = carry
    o_ref[0] = jnp.where(lo_half, accs[0], accs[1]).astype(BF16)


def _sb_attention(q, q_cb, k, k_cb, v, v_cb, n_pairs, q_pos0, tq, tk):
    b, t = q.shape[:2]
    l = k.shape[1]
    later = (lax.broadcasted_iota(I32, (tk, tk), 0) >
             lax.broadcasted_iota(I32, (tk, tk), 1)).astype(BF16)
    return pl.pallas_call(
        functools.partial(_sb_body, tq=tq, tk=tk, q_pos0=q_pos0, n_kb_total=l // tk),
        grid=(b, n_pairs, t // tq),
        in_specs=[pl.BlockSpec((1, tq, LANES), lambda bi, p, i: (bi, i, q_cb + p)),
                  pl.BlockSpec((1, l, LANES), lambda bi, p, i: (bi, 0, k_cb + p)),
                  pl.BlockSpec((1, l, LANES), lambda bi, p, i: (bi, 0, v_cb + p)),
                  pl.BlockSpec((tk, tk), lambda bi, p, i: (0, 0))],
        out_specs=pl.BlockSpec((1, tq, LANES), lambda bi, p, i: (bi, i, p)),
        out_shape=jax.ShapeDtypeStruct((b, t, n_pairs * LANES), BF16),
        compiler_params=_params(("parallel", "parallel", "parallel")),
        name="sb_attention",
    )(q, k, v, later)


def _softmax_step(s, v, m, l, acc):
    m_new = jnp.maximum(m, jnp.max(s, axis=1, keepdims=True))
    alpha = jnp.exp2(m - m_new)
    p = jnp.exp2(s - m_new)
    l_new = alpha * l + jnp.sum(p, axis=1, keepdims=True)
    acc_new = alpha * acc + _dot(p.astype(BF16), v)
    return m_new, l_new, acc_new


def _fox_body(q_ref, k_ref, v_ref, fk_ref, j0_ref, o_ref, q_s,
              *, tq, tk, q_pos0, n_kb_total, npp):
    row0 = q_pos0 + pl.program_id(2) * tq
    lane = lax.broadcasted_iota(I32, (1, LANES), 1)
    lo_half = lane < HEAD_DIM
    for pp in range(npp):
        q = q_ref[0, :, pp * LANES:(pp + 1) * LANES]
        zero = jnp.zeros_like(q)
        q_s[2 * pp] = jnp.where(lo_half, q, zero)
        q_s[2 * pp + 1] = jnp.where(lo_half, zero, q)
    qpos = row0 + lax.broadcasted_iota(I32, (tq, 1), 0)
    nkb = jnp.minimum((row0 + tq + tk - 1) // tk, n_kb_total)
    n_full = jnp.minimum((row0 + 1) // tk, nkb)

    def body(j, carry, masked, width):
        ks = pl.multiple_of(j * tk, tk)
        out = []
        for h in range(2 * npp):
            pp, half = divmod(h, 2)
            k = k_ref[0, pl.ds(ks, width), pp * LANES:(pp + 1) * LANES]
            v = v_ref[0, pl.ds(ks, width), pp * LANES:(pp + 1) * LANES]
            m, l, acc = carry[h]
            fk = fk_ref[0, pp, half:half + 1, pl.ds(ks, width)] * LOG2E
            s = _dot_nt(q_s[h], k) - fk
            if masked:
                vis = (ks + lax.broadcasted_iota(I32, (1, width), 1)) <= qpos
                s = jnp.where(vis, s, NEG_BIG)
            out.append(_softmax_step(s, v, m, l, acc))
        return tuple(out)

    j0 = jnp.minimum(j0_ref[pl.program_id(0), pl.program_id(1), pl.program_id(2)], n_full)

    init = (jnp.full((tq, 1), M_INIT, F32), jnp.zeros((tq, 1), F32), jnp.zeros((tq, LANES), F32))
    n_wide = (n_full - j0) // 2
    carry = lax.fori_loop(0, n_wide,
                          lambda j, c: body(j0 + 2 * j, c, masked=False, width=2 * tk),
                          (init,) * (2 * npp))
    carry = lax.fori_loop(j0 + 2 * n_wide, nkb, functools.partial(body, masked=True, width=tk), carry)
    for pp in range(npp):
        (_, l0, a0), (_, l1, a1) = carry[2 * pp], carry[2 * pp + 1]
        o_ref[0, :, pp * LANES:(pp + 1) * LANES] = jnp.where(lo_half, a0 / l0, a1 / l1).astype(BF16)


def _fox_first_block(cum, decay_cut, q_pos0, t, tq, tk, npp):
    b, l, h = cum.shape
    f2 = cum * LOG2E
    f_tile = jnp.max(f2[:, q_pos0:q_pos0 + t].reshape(b, t // tq, tq, h), axis=2)
    gone = (f2[:, None] - f_tile[:, :, None]) > decay_cut[:, None, None, :]
    first_needed = jnp.min(jnp.where(gone, l, jnp.arange(l, dtype=I32)[None, None, :, None]), axis=2)
    j0 = jnp.min((first_needed // tk).reshape(b, t // tq, h // (2 * npp), 2 * npp), axis=-1)
    return j0.transpose(0, 2, 1).astype(I32)


def _fox_attention(q, q_cb, k, k_cb, v, v_cb, fk, j0, q_pos0, tq, tk):
    b, t = q.shape[:2]
    l = k.shape[1]
    n_pairs = H_FOX // 2
    npp = FOX_PAIRS
    w = npp * LANES
    assert q_cb % npp == 0 and k_cb % npp == 0 and v_cb % npp == 0
    return pl.pallas_call(
        functools.partial(_fox_body, tq=tq, tk=tk, q_pos0=q_pos0, n_kb_total=l // tk, npp=npp),
        grid=(b, n_pairs // npp, t // tq),
        in_specs=[pl.BlockSpec((1, tq, w), lambda bi, p, i: (bi, i, q_cb // npp + p)),
                  pl.BlockSpec((1, l, w), lambda bi, p, i: (bi, 0, k_cb // npp + p)),
                  pl.BlockSpec((1, l, w), lambda bi, p, i: (bi, 0, v_cb // npp + p)),
                  pl.BlockSpec((1, npp, 2, l), lambda bi, p, i: (bi, p, 0, 0)),
                  pl.BlockSpec(memory_space=pltpu.SMEM)],
        out_specs=pl.BlockSpec((1, tq, w), lambda bi, p, i: (bi, i, p)),
        out_shape=jax.ShapeDtypeStruct((b, t, n_pairs * LANES), BF16),
        scratch_shapes=[pltpu.VMEM((2 * npp, tq, LANES), BF16)],
        compiler_params=_params(("parallel", "parallel", "parallel")),
        name="fox_attention",
    )(q, k, v, fk, j0)


def _forget_body(raw_ref, b_ref, logf_ref, cum_ref, carry_ref, *, tb, n_given):
    j = pl.program_id(1)

    @pl.when(j == 0)
    def _():
        carry_ref[...] = jnp.zeros_like(carry_ref)

    raw = raw_ref[0]
    z = raw + b_ref[...]
    computed = _log_sigmoid_neg(-z)
    row = j * tb + lax.broadcasted_iota(I32, (tb, 1), 0)
    logf = jnp.where(row < n_given, raw, computed)
    incl = (lax.broadcasted_iota(I32, (tb, tb), 1) <=
            lax.broadcasted_iota(I32, (tb, tb), 0)).astype(BF16)
    hi, mid, lo = _split3(logf)
    cum = _dot(incl, hi) + _dot(incl, mid) + _dot(incl, lo) + carry_ref[0:1, :]
    logf_ref[0] = logf
    cum_ref[0] = cum
    carry_ref[0:1, :] = cum[tb - 1:tb, :]


def _forget_cumsum(raw, col_block, bias, n_given, tb):
    bias = jnp.pad(bias, (0, LANES - bias.shape[0]))
    b, l = raw.shape[:2]
    h = LANES
    return pl.pallas_call(
        functools.partial(_forget_body, tb=tb, n_given=n_given),
        grid=(b, l // tb),
        in_specs=[pl.BlockSpec((1, tb, h), lambda bi, j: (bi, j, col_block)),
                  pl.BlockSpec((1, h), lambda bi, j: (0, 0))],
        out_specs=[pl.BlockSpec((1, tb, h), lambda bi, j: (bi, j, 0)),
                   pl.BlockSpec((1, tb, h), lambda bi, j: (bi, j, 0))],
        out_shape=[jax.ShapeDtypeStruct((b, l, h), F32), jax.ShapeDtypeStruct((b, l, h), F32)],
        scratch_shapes=[pltpu.VMEM((8, h), F32)],
        compiler_params=_params(("parallel", "arbitrary")),
        name="forget_cumsum",
    )(raw, bias.reshape(1, h))


def _dsa_body(q_ref, qi_ref, wi_ref, k_ref, v_ref, ki_ref, o_ref,
              key_s, bias_s, qi_s, q_s, plane_s, *, tq, tk, q_pos0, l_valid, n_kb_total, n_sel):
    row0 = q_pos0 + pl.program_id(1) * tq
    lane = lax.broadcasted_iota(I32, (1, LANES), 1)
    lo_half = lane < HEAD_DIM
    qpos = row0 + lax.broadcasted_iota(I32, (tq, 1), 0)
    qchunk = qpos // CHUNK
    kend = jnp.minimum(((row0 + tq - 1) // CHUNK + 1) * CHUNK, l_valid)
    nkb = jnp.minimum(jnp.maximum((kend + tk - 1) // tk, (n_sel + tk - 1) // tk), n_kb_total)

    n_stack = IDX_HEADS // 2
    for p in range(n_stack):
        blk = qi_ref[0, :, p * LANES:(p + 1) * LANES]
        zero = jnp.zeros_like(blk)
        qi_s[0, p * tq:(p + 1) * tq, :] = jnp.where(lo_half, blk, zero)
        qi_s[1, p * tq:(p + 1) * tq, :] = jnp.where(lo_half, zero, blk)
    wsc = wi_ref[0] * (IDX_DIM ** -0.5 * IDX_HEADS ** -0.5)

    def wide_then_single(step, init):
        carry = lax.fori_loop(0, nkb // 2, lambda j, c: step(2 * j, c, 2 * tk), init)
        return lax.fori_loop(nkb // 2 * 2, nkb, lambda j, c: step(j, c, tk), carry)

    def score_body(j, _, width):
        for u in range(width // SCORE_SUB):
            ks = pl.multiple_of(j * tk + u * SCORE_SUB, SCORE_SUB)
            ki = ki_ref[0, pl.ds(ks, SCORE_SUB), :]
            score = jnp.zeros((tq, SCORE_SUB), F32)
            for half in range(2):
                rel = jnp.maximum(_dot_nt(qi_s[half], ki), 0.0)
                for p in range(n_stack):
                    h = 2 * p + half
                    score = score + wsc[:, h:h + 1] * rel[p * tq:(p + 1) * tq]
            kpos = ks + lax.broadcasted_iota(I32, (1, SCORE_SUB), 1)
            vis = ((kpos // CHUNK) <= qchunk) & (kpos < l_valid)
            score = jnp.where(vis, score, -jnp.inf)
            bits = lax.bitcast_convert_type(score, I32)
            key = bits ^ ((bits >> 31) & INT_MAX)
            key_s[:, pl.ds(ks, SCORE_SUB)] = jnp.where(bits == INT_MIN, 0, key)
        return 0

    wide_then_single(score_body, 0)

    n_sel_f = float(n_sel)
    cr = min(COUNT_ROWS, tq)

    slabs = [slice(r * cr, (r + 1) * cr) for r in range(tq // cr)]

    def count(rs, pred):
        def cbody(j, acc):
            blk = key_s[rs, pl.ds(pl.multiple_of(j * tk, tk), tk)]
            hit = jnp.where(pred(blk, j), 1.0, 0.0)
            for c in range(tk // LANES):
                acc = acc + hit[:, c * LANES:(c + 1) * LANES]
            return acc
        acc = lax.fori_loop(0, nkb, cbody, jnp.zeros((cr, LANES), F32))
        return jnp.sum(acc, axis=1, keepdims=True)

    n_grp_total = n_kb_total * tk // BIT_GROUP
    n_bgrp = jnp.minimum((nkb * tk + BIT_GROUP - 1) // BIT_GROUP, n_grp_total)

    def fill_body(j, _):
        key_s[:, pl.ds(pl.multiple_of(j * tk, tk), tk)] = jnp.full((tq, tk), KEY_NEG_INF, I32)
        return 0

    lax.fori_loop(nkb, n_bgrp * (BIT_GROUP // tk), fill_body, 0)

    def plane_body(it, _):
        g, r = it // (tq // 8), it % (tq // 8)
        rows = pl.ds(pl.multiple_of(r * 8, 8), 8)
        base = pl.multiple_of(g * BIT_GROUP, BIT_GROUP)
        a = [key_s[rows, pl.ds(base + c * LANES, LANES)] ^ INT_MIN for c in range(32)]
        j, m = 16, 0x0000FFFF
        while j:
            k = 0
            while k < 32:
                t = (a[k] ^ lax.shift_right_logical(a[k + j], jnp.full((8, LANES), j, I32))) & m
                a[k] = a[k] ^ t
                a[k + j] = a[k + j] ^ (t << j)
                k = (k + j + 1) & ~j
            j >>= 1
            m = m ^ (m << j)
        for i in range(32):
            plane_s[i, rows, pl.ds(pl.multiple_of(g * LANES, LANES), LANES)] = a[i]
        return 0

    lax.fori_loop(0, n_bgrp * (tq // 8), plane_body, 0)

    plane_w = n_grp_total * LANES
    in_use = (lax.broadcasted_iota(I32, (1, plane_w), 1) // LANES) < n_bgrp

    def popsum(words):
        return jnp.sum(lax.population_count(words).astype(F32), axis=1, keepdims=True)

    def bit_pass(i, state):
        alive, above, thr_u = state
        p1, p2 = plane_s[2 * i], plane_s[2 * i + 1]
        a1, a0 = alive & p1, alive & ~p1
        a11, a10, a01, a00 = a1 & p2, a1 & ~p2, a0 & p2, a0 & ~p2
        n11, n10, n01 = popsum(a11), popsum(a10), popsum(a01)
        keep1 = above + n11 + n10 >= n_sel_f
        above1 = jnp.where(keep1, above, above + n11 + n10)
        n_hi = jnp.where(keep1, n11, n01)
        keep2 = above1 + n_hi >= n_sel_f
        above = jnp.where(keep2, above1, above1 + n_hi)
        alive = jnp.where(keep1, jnp.where(keep2, a11, a10), jnp.where(keep2, a01, a00))
        top = lax.shift_right_logical(jnp.full((tq, 1), INT_MIN, I32), jnp.full((tq, 1), 2 * i, I32))
        thr_u = thr_u | jnp.where(keep1, top, 0) | jnp.where(keep2, lax.shift_right_logical(
            top, jnp.ones((tq, 1), I32)), 0)
        return alive, above, thr_u

    alive0 = jnp.where(in_use, -1, 0) + jnp.zeros((tq, plane_w), I32)
    alive, above, thr_u = lax.fori_loop(
        0, 16, bit_pass, (alive0, jnp.zeros((tq, 1), F32), jnp.zeros((tq, 1), I32)))
    thr_all = thr_u ^ INT_MIN
    c_ge_all = above + jnp.sum(lax.population_count(alive).astype(F32), axis=1, keepdims=True)
    thrs = [thr_all[rs] for rs in slabs]
    c_ges = [c_ge_all[rs] for rs in slabs]

    for rs, thr, c_ge in zip(slabs, thrs, c_ges):
        tied = (c_ge > n_sel_f) & (thr > KEY_NEG_INF)
        any_tied = jnp.max(jnp.where(tied, 1.0, 0.0)) > 0.0

        def tie_index_bound(rs=rs, thr=thr, tied=tied):
            need = n_sel_f - count(rs, lambda blk, j: blk > thr)

            def kidx(j):
                return j * tk + lax.broadcasted_iota(I32, (1, tk), 1)

            n_bits = (n_kb_total * tk - 1).bit_length()

            def jbody(b, jb):
                cand = jb | (1 << (n_bits - 1 - b))
                cnt = count(rs, lambda blk, j: (blk == thr) & (kidx(j) < cand))
                return jnp.where(cnt < need, cand, jb)

            jb = lax.fori_loop(0, n_bits, jbody, jnp.zeros((cr, 1), I32))
            return jnp.where(tied, jb, INT_MAX)

        jbound = lax.cond(any_tied, tie_index_bound, lambda: jnp.full((cr, 1), INT_MAX, I32))

        def bias_body(j, _, rs=rs, thr=thr, jbound=jbound):
            ks = pl.multiple_of(j * tk, tk)
            blk = key_s[rs, pl.ds(ks, tk)]
            kpos = ks + lax.broadcasted_iota(I32, (1, tk), 1)
            sel = (blk > thr) | ((blk == thr) & (kpos <= jbound))
            sel = sel & (blk > KEY_NEG_INF)
            bias_s[rs, pl.ds(ks, tk)] = jnp.where(sel, 0.0, NEG_BIG)
            return 0

        lax.fori_loop(0, nkb, bias_body, 0)

    n_grp = H_DSA // KV_DSA
    for p in range(n_grp):
        blk = q_ref[0, :, p * LANES:(p + 1) * LANES]
        zero = jnp.zeros_like(blk)
        q_s[0, p * tq:(p + 1) * tq, :] = jnp.where(lo_half, blk, zero)
        q_s[1, p * tq:(p + 1) * tq, :] = jnp.where(lo_half, zero, blk)

    def att_body(j, carry, width):
        ks = pl.multiple_of(j * tk, tk)
        k = k_ref[0, pl.ds(ks, width), :]
        v = v_ref[0, pl.ds(ks, width), :]
        bias = bias_s[:, pl.ds(ks, width)]
        bias = jnp.concatenate([bias] * n_grp, axis=0)
        return tuple(_softmax_step(_dot_nt(q_s[half], k) + bias, v, *carry[half])
                     for half in range(2))

    rows = n_grp * tq
    init = (jnp.full((rows, 1), M_INIT, F32), jnp.zeros((rows, 1), F32),
            jnp.zeros((rows, LANES), F32))
    (_, l0, a0), (_, l1, a1) = wide_then_single(att_body, (init, init))
    o0, o1 = a0 / l0, a1 / l1
    for p in range(n_grp):
        o_ref[0, :, p * LANES:(p + 1) * LANES] = jnp.where(
            lo_half, o0[p * tq:(p + 1) * tq], o1[p * tq:(p + 1) * tq]).astype(BF16)


def _dsa_attention(q, qi, qi_cb, wi, wi_cb, k, k_cb, v, v_cb, ki, ki_cb,
                   q_pos0, l_valid, n_sel, tq, tk):
    b, t = q.shape[:2]
    l = k.shape[1]
    width = H_DSA * HEAD_DIM
    return pl.pallas_call(
        functools.partial(_dsa_body, tq=tq, tk=tk, q_pos0=q_pos0, l_valid=l_valid,
                          n_kb_total=l // tk, n_sel=n_sel),
        grid=(b, t // tq),
        in_specs=[pl.BlockSpec((1, tq, width), lambda bi, i: (bi, i, 0)),
                  pl.BlockSpec((1, tq, width), lambda bi, i: (bi, i, qi_cb)),
                  pl.BlockSpec((1, tq, LANES), lambda bi, i: (bi, i, wi_cb)),
                  pl.BlockSpec((1, l, LANES), lambda bi, i: (bi, 0, k_cb)),
                  pl.BlockSpec((1, l, LANES), lambda bi, i: (bi, 0, v_cb)),
                  pl.BlockSpec((1, l, LANES), lambda bi, i: (bi, 0, ki_cb))],
        out_specs=pl.BlockSpec((1, tq, width), lambda bi, i: (bi, i, 0)),
        out_shape=jax.ShapeDtypeStruct((b, t, width), BF16),
        scratch_shapes=[pltpu.VMEM((tq, l), I32), pltpu.VMEM((tq, l), F32),
                        pltpu.VMEM((2, IDX_HEADS // 2 * tq, LANES), BF16),
                        pltpu.VMEM((2, H_DSA // KV_DSA * tq, LANES), BF16),
                        pltpu.VMEM((32, tq, l // BIT_GROUP * LANES), I32)],
        compiler_params=_params(("parallel", "parallel")),
        name="dsa_attention",
    )(q, qi, wi, k, v, ki)


def _moe_body(h_ref, g_ref, wr_ref, w1_ref, w3_ref, w2_ref, o_ref,
              xn_s, xg_s, ye_s, rank_s, gate_s, rank_t_s, *, tm, ch, n_fc):
    e = pl.program_id(1)
    fc = pl.program_id(2)
    lane = lax.broadcasted_iota(I32, (1, LANES), 1)

    @pl.when((e == 0) & (fc == 0))
    def _route():
        x = h_ref[...]
        xn = _rms(x, g_ref[...])
        xn_s[...] = xn.astype(BF16)
        o_ref[...] = x
        x3 = _split3(xn)
        logits = jnp.zeros((tm, LANES), F32)
        for a, b in ((2, 0), (0, 2), (1, 1), (1, 0), (0, 1), (0, 0)):
            logits = logits + _dot(x3[a], wr_ref[b])
        lane_f = lane.astype(F32)
        logits = jnp.where(lane < N_EXPERTS, logits, -jnp.inf)
        m1 = jnp.max(logits, axis=1, keepdims=True)
        i1 = jnp.min(jnp.where(logits == m1, lane_f, float(LANES)), axis=1, keepdims=True)
        rest = jnp.where(lane_f == i1, -jnp.inf, logits)
        m2 = jnp.max(rest, axis=1, keepdims=True)
        i2 = jnp.min(jnp.where(rest == m2, lane_f, float(LANES)), axis=1, keepdims=True)
        e2 = jnp.exp(m2 - m1)
        g1 = 1.0 / (1.0 + e2)
        g2 = e2 / (1.0 + e2)
        sel1 = lane_f == i1
        sel2 = lane_f == i2
        gate_s[...] = jnp.where(sel1, g1, 0.0) + jnp.where(sel2, g2, 0.0)
        sel = jnp.where(sel1 | sel2, 1.0, 0.0)
        incl = (lax.broadcasted_iota(I32, (tm, tm), 1) <=
                lax.broadcasted_iota(I32, (tm, tm), 0)).astype(BF16)
        rank = _dot(incl, sel.astype(BF16)) * sel
        rank_s[...] = rank
        rank_t_s[...] = rank.T

    rank_row = rank_t_s[pl.ds(e, 1), :]
    cnt = jnp.max(rank_row).astype(I32)
    nch = (cnt + ch - 1) // ch

    @pl.when(fc == 0)
    def _gather():
        def gbody(c, _):
            base = pl.multiple_of(c * ch, ch)
            slot = (base + 1 + lax.broadcasted_iota(I32, (ch, 1), 0)).astype(F32)
            pick = jnp.where(rank_row == slot, 1.0, 0.0).astype(BF16)
            xg_s[pl.ds(base, ch), :] = _dot(pick, xn_s[...]).astype(BF16)
            return 0
        lax.fori_loop(0, nch, gbody, 0)

    def fbody(c, _):
        base = pl.multiple_of(c * ch, ch)
        xg = xg_s[pl.ds(base, ch), :]
        a = _dot(xg, w1_ref[0])
        b = _dot(xg, w3_ref[0])
        part = _dot((a * _sigmoid(a) * b).astype(BF16), w2_ref[0])

        @pl.when(fc == 0)
        def _():
            ye_s[pl.ds(base, ch), :] = part

        @pl.when(fc != 0)
        def _():
            ye_s[pl.ds(base, ch), :] += part
        return 0

    lax.fori_loop(0, nch, fbody, 0)

    @pl.when(fc == n_fc - 1)
    def _scatter():
        here = lane == e
        rank_col = jnp.sum(jnp.where(here, rank_s[...], 0.0), axis=1, keepdims=True)
        gate_col = jnp.sum(jnp.where(here, gate_s[...], 0.0), axis=1, keepdims=True)

        def sbody(c, _):
            base = pl.multiple_of(c * ch, ch)
            slot = (base + 1 + lax.broadcasted_iota(I32, (1, ch), 1)).astype(F32)
            place = jnp.where(rank_col == slot, 1.0, 0.0).astype(BF16)
            ye = ye_s[pl.ds(base, ch), :].astype(BF16)
            o_ref[...] += gate_col * _dot(place, ye)
            return 0
        lax.fori_loop(0, nch, sbody, 0)


def _moe(h, g, wr3, w1, w3, w2, tm, ch, tf):
    n, d = h.shape
    ne, _, f = w1.shape
    n_fc = f // tf
    return pl.pallas_call(
        functools.partial(_moe_body, tm=tm, ch=ch, n_fc=n_fc),
        grid=(n // tm, ne, n_fc),
        in_specs=[pl.BlockSpec((tm, d), lambda i, e, c: (i, 0), pipeline_mode=pl.Buffered(1)),
                  pl.BlockSpec((1, d), lambda i, e, c: (0, 0)),
                  pl.BlockSpec((3, d, LANES), lambda i, e, c: (0, 0, 0)),
                  pl.BlockSpec((1, d, tf), lambda i, e, c: (e, 0, c)),
                  pl.BlockSpec((1, d, tf), lambda i, e, c: (e, 0, c)),
                  pl.BlockSpec((1, tf, d), lambda i, e, c: (e, c, 0))],
        out_specs=pl.BlockSpec((tm, d), lambda i, e, c: (i, 0)),
        out_shape=jax.ShapeDtypeStruct((n, d), F32),
        scratch_shapes=[pltpu.VMEM((tm, d), BF16), pltpu.VMEM((_round_up(tm, ch), d), BF16),
                        pltpu.VMEM((_round_up(tm, ch), d), F32), pltpu.VMEM((tm, LANES), F32),
                        pltpu.VMEM((tm, LANES), F32), pltpu.VMEM((LANES, tm), F32)],
        compiler_params=_params(("parallel", "arbitrary", "arbitrary")),
        name="moe",
    )(h, g.reshape(1, d), wr3, w1, w3, w2)


def _round_up(x, m):
    return (x + m - 1) // m * m


def _pad_rows(a, l_pad):
    return jnp.pad(a, ((0, 0), (0, l_pad - a.shape[1]), (0, 0)))


def _prep_weights(g_q_dsa, g_k_dsa, g_q_fox, g_k_fox, w_in_even, w_out_even, w_in_odd, w_router):
    d = w_in_even.shape[0]
    perm = jnp.asarray(DSA_HEAD_PERM)
    qa, ka, va, qb, kb, vb, qi, ki, wi = jnp.split(
        w_in_even, [512, 1024, 1536, 2048, 2176, 2304, 2816, 2880], axis=1)
    qb = qb.reshape(d, H_DSA, HEAD_DIM)[:, perm].reshape(d, H_DSA * HEAD_DIM)
    wi = jnp.pad(wi, ((0, 0), (0, LANES - IDX_HEADS)))
    w_even = jnp.concatenate([qb, kb, vb, ki, ki, wi, qi, qa, ka, va], axis=1).astype(BF16)
    ones = lambda k: jnp.ones((k,), F32)
    q_scale = lambda k: jnp.full((k,), Q_SCALE, F32)
    scale_even = jnp.concatenate([ones(E_QA), q_scale(E_KA - E_QA), ones(E_END - E_KA)])
    scale_odd = ones(O_END)
    scale_norm_even = jnp.concatenate([q_scale(E_KB - E_QB), ones(E_VB - E_KB)])
    scale_norm_odd = jnp.concatenate([q_scale(O_K - O_Q), ones(O_V - O_K)])
    w_out_sb = w_out_even[:H_SB * HEAD_DIM].astype(BF16)
    w_out_dsa = w_out_even[H_SB * HEAD_DIM:].reshape(H_DSA, HEAD_DIM, d)[perm]
    w_out_dsa = w_out_dsa.reshape(H_DSA * HEAD_DIM, d).astype(BF16)
    w_odd = jnp.pad(w_in_odd, ((0, 0), (0, O_END - w_in_odd.shape[1]))).astype(BF16)
    gains_even = jnp.concatenate([jnp.tile(g_q_dsa, H_DSA), jnp.tile(g_k_dsa, KV_DSA)])
    gains_odd = jnp.concatenate([jnp.tile(g_q_fox, H_FOX), jnp.tile(g_k_fox, H_FOX)])
    wr = jnp.pad(w_router, ((0, 0), (0, LANES - N_EXPERTS)))
    wr_hi = wr.astype(BF16)
    wr_r1 = wr - wr_hi.astype(F32)
    wr_mid = wr_r1.astype(BF16)
    wr_lo = (wr_r1 - wr_mid.astype(F32)).astype(BF16)
    return dict(w_even=w_even, w_out_sb=w_out_sb, w_out_dsa=w_out_dsa, w_odd=w_odd,
                gains_even=gains_even, gains_odd=gains_odd,
                scale_even=scale_even, scale_odd=scale_odd,
                scale_norm_even=scale_norm_even, scale_norm_odd=scale_norm_odd,
                wr3=jnp.stack([wr_hi, wr_mid, wr_lo]))


def _rotary_tables(pos):
    half = ROT_DIM // 2
    inv = ROPE_THETA ** (-jnp.arange(half, dtype=F32) / half)
    ang = pos.astype(F32)[:, None] * inv[None, :]
    cos, sin = jnp.cos(ang), jnp.sin(ang)
    t = pos.shape[0]
    pad = HEAD_DIM - ROT_DIM
    cos_h = jnp.concatenate([cos, cos, jnp.ones((t, pad), F32)], axis=1)
    sin_h = jnp.concatenate([-sin, sin, jnp.zeros((t, pad), F32)], axis=1)
    return jnp.tile(cos_h, (1, 2)), jnp.tile(sin_h, (1, 2))


def _trunk(x, p, pos0, past_even, past_odd, keep, w, wb):
    b, t, d = x.shape
    n = b * t
    tm = min(ROW_TILE, n)
    has_past = past_even is not None
    past_len = past_even[0].shape[1] if has_past else 0
    l_valid = past_len + t
    l_pad = _round_up(l_valid, KEY_PAD)
    n_sel = min(DSA_TOPK, l_valid // 4)
    h = x.reshape(n, d)
    assert n % tm == 0 and (t % tm == 0 or tm % t == 0), (n, t, tm)
    assert all(t % min(tile, t) == 0 for tile in (SB_Q_TILE, FOX_Q_TILE, DSA_Q_TILE)), t
    assert has_past or t % KEY_PAD == 0, "without a cache the keys are this step's rows, unpadded"
    assert n % min(MOE_TILE, n) == 0 and n_sel <= l_pad

    def keys(cache, new):
        if not has_past:
            return new
        full = jnp.concatenate([cache.reshape(b, past_len, -1).astype(BF16), new], axis=1)
        return _pad_rows(full, l_pad)

    cos, sin = _rotary_tables(pos0 + jnp.arange(t, dtype=I32))
    if t % tm:
        cos, sin = jnp.tile(cos, (tm // t, 1)), jnp.tile(sin, (tm // t, 1))
    pf, pb, qkf, qkb = _proj(h, w["g_mix"][0], wb["w_even"], wb["scale_even"],
                             wb["gains_even"], wb["scale_norm_even"], cos, sin, 0, tm)
    pf3, pb3 = pf.reshape(b, t, E_END), pb.reshape(b, t, E_END)
    qkf3, qkb3 = qkf.reshape(b, t, E_NORM), qkb.reshape(b, t, E_NORM)
    cols = lambda a, c0, width: a[:, :, c0:c0 + width]
    sb_w, kv_w = H_SB * HEAD_DIM, KV_DSA * HEAD_DIM
    if has_past:
        c_sbk, c_sbv, c_dk, c_dv, c_ki = past_even
        ka, ka_cb = keys(c_sbk, cols(pb3, E_KA, sb_w)), 0
        va, va_cb = keys(c_sbv, cols(pb3, E_VA, sb_w)), 0
        kb, kb_cb = keys(c_dk, cols(qkb3, E_KB, kv_w)), 0
        vb, vb_cb = keys(c_dv, cols(pb3, E_VB, kv_w)), 0
        ki2 = jnp.concatenate([c_ki, c_ki], axis=-1)
        ki, ki_cb = keys(ki2, cols(pb3, E_KI, LANES)), 0
    else:
        ka, ka_cb = pb3, E_KA // LANES
        va, va_cb = pb3, E_VA // LANES
        kb, kb_cb = qkb3, E_KB // LANES
        vb, vb_cb = pb3, E_VB // LANES
        ki, ki_cb = pb3, E_KI // LANES
    oa = _sb_attention(pb3, E_QA // LANES, ka, ka_cb, va, va_cb, H_SB // 2, pos0,
                       min(SB_Q_TILE, t), SB_K_TILE)
    ob = _dsa_attention(qkb3, pb3, E_QI // 512, pf3, E_WI // LANES, kb, kb_cb, vb, vb_cb,
                        ki, ki_cb, pos0, l_valid, n_sel, min(DSA_Q_TILE, t), DSA_K_TILE)
    h = _even_channel(h, oa.reshape(n, -1), ob.reshape(n, -1), wb["w_out_sb"], wb["w_out_dsa"],
                      w["g_ffn"][0], wb["w_ff1"], wb["w_ff3"], wb["w_ff2"],
                      w["g_ple"][0], wb["w_ple_gate"][0], p[0].reshape(n, -1), wb["w_ple_in"][0], tm)
    kept = lambda a, c0, width: a[:, t - keep:, c0:c0 + width]
    even_state = (kept(pf3, E_KA, sb_w).reshape(b, keep, H_SB, HEAD_DIM),
                  kept(pf3, E_VA, sb_w).reshape(b, keep, H_SB, HEAD_DIM),
                  kept(qkf3, E_KB, kv_w).reshape(b, keep, KV_DSA, HEAD_DIM),
                  kept(pf3, E_VB, kv_w).reshape(b, keep, KV_DSA, HEAD_DIM),
                  kept(pf3, E_KI, IDX_DIM))

    fox_w = H_FOX * HEAD_DIM
    pf, pb, qkf, qkb = _proj(h, w["g_mix"][1], wb["w_odd"], wb["scale_odd"],
                             wb["gains_odd"], wb["scale_norm_odd"], None, None, O_V, tm)
    r_f = O_F - O_V
    pf3, pb3 = pf.reshape(b, t, O_END - O_V), pb.reshape(b, t, O_END - O_V)
    qkf3, qkb3 = qkf.reshape(b, t, O_V), qkb.reshape(b, t, O_V)
    if has_past:
        c_fk, c_fv, c_lf = past_odd
        kf, kf_cb = keys(c_fk, cols(qkb3, O_K, fox_w)), 0
        vf, vf_cb = keys(c_fv, cols(pb3, 0, fox_w)), 0
        raw = _pad_rows(jnp.concatenate([c_lf, cols(pf3, r_f, H_FOX)], axis=1), l_pad)
        raw, raw_cb = jnp.pad(raw, ((0, 0), (0, 0), (0, LANES - H_FOX))), 0
    else:
        kf, kf_cb = qkb3, O_K // LANES
        vf, vf_cb = pb3, 0
        raw, raw_cb = pf3, r_f // LANES
    logf, cum = _forget_cumsum(raw, raw_cb, w["b_forget"][0], past_len, min(FORGET_TILE, l_pad))
    logf, cum = logf[:, :, :H_FOX], cum[:, :, :H_FOX]
    fk = cum.reshape(b, l_pad, H_FOX // 2, 2).transpose(0, 2, 3, 1)

    normed_max = lambda g: HEAD_DIM ** 0.5 * jnp.max(jnp.abs(g))
    q_max = jnp.full((b, H_FOX), Q_SCALE * normed_max(w["g_q_fox"]), F32)
    if has_past:
        kx = kf.astype(F32).reshape(b, l_pad, H_FOX, HEAD_DIM)
        k_max = jnp.sqrt(jnp.max(jnp.sum(kx * kx, axis=-1), axis=1))
    else:
        k_max = jnp.full((b, H_FOX), normed_max(w["g_k_fox"]), F32)
    decay_cut = EXP2_ZERO + 2.0 * 1.02 * q_max * k_max
    fox_tq = min(FOX_Q_TILE, t)
    j0 = _fox_first_block(cum, decay_cut, pos0, t, fox_tq, FOX_K_TILE, FOX_PAIRS)
    of = _fox_attention(qkb3, O_Q // LANES, kf, kf_cb, vf, vf_cb, fk, j0, pos0, fox_tq, FOX_K_TILE)
    h = _residual_matmul(h, [of.reshape(n, -1)], [wb["w_out_odd"]], tm)
    h = _moe(h, w["g_ffn"][1], wb["wr3"], wb["w_exp1"], wb["w_exp3"], wb["w_exp2"],
             min(MOE_TILE, n), min(MOE_CHUNK, n), MOE_F_TILE)
    h = _ple(h, w["g_ple"][1], wb["w_ple_gate"][1], p[1].reshape(n, -1), wb["w_ple_in"][1], tm)
    odd_state = (kept(qkf3, O_K, fox_w).reshape(b, keep, H_FOX, HEAD_DIM),
                 kept(pf3, 0, fox_w).reshape(b, keep, H_FOX, HEAD_DIM),
                 logf[:, l_valid - keep:l_valid])
    return h.reshape(b, t, d), even_state, odd_state


def kernel(x_prompt, x_sample, p_prompt, p_sample, cache_sb_k, cache_sb_v, cache_dsa_k, cache_dsa_v, cache_dsa_kidx, cache_fox_k, cache_fox_v, cache_fox_logf, g_mix, g_ffn, g_ple, w_in_even, g_q_dsa, g_k_dsa, w_out_even, w_ff1, w_ff3, w_ff2, w_in_odd, b_forget, g_q_fox, g_k_fox, w_out_odd, w_router, w_exp1, w_exp3, w_exp2, w_ple_in, w_ple_gate):
    assert g_mix.shape[0] == 2, "two layers: one even (stick-breaking + DSA), one odd (FoX + experts)"
    past_len = cache_sb_k.shape[2]
    w = dict(g_mix=g_mix, g_ffn=g_ffn, g_ple=g_ple, b_forget=b_forget,
             g_q_fox=g_q_fox[0], g_k_fox=g_k_fox[0])
    wb = _prep_weights(g_q_dsa[0], g_k_dsa[0], g_q_fox[0], g_k_fox[0],
                       w_in_even[0], w_out_even[0], w_in_odd[0], w_router[0])
    wb.update(w_ff1=w_ff1[0].astype(BF16), w_ff3=w_ff3[0].astype(BF16), w_ff2=w_ff2[0].astype(BF16),
              w_out_odd=w_out_odd[0].astype(BF16),
              w_exp1=w_exp1[0].astype(BF16), w_exp3=w_exp3[0].astype(BF16),
              w_exp2=w_exp2[0].astype(BF16),
              w_ple_in=w_ple_in.astype(BF16), w_ple_gate=w_ple_gate.astype(BF16))

    y_p, even_p, odd_p = _trunk(x_prompt, p_prompt, 0, None, None,
                                min(x_prompt.shape[1], past_len), w, wb)
    y_s, even_s, odd_s = _trunk(
        x_sample, p_sample, past_len,
        (cache_sb_k[0], cache_sb_v[0], cache_dsa_k[0], cache_dsa_v[0], cache_dsa_kidx[0]),
        (cache_fox_k[0], cache_fox_v[0], cache_fox_logf[0]), x_sample.shape[1], w, wb)
    layer = lambda a: a[None]
    return (y_p, y_s,
            *(layer(a) for a in even_p), *(layer(a) for a in odd_p),
            *(layer(a) for a in even_s), *(layer(a) for a in odd_s))
```
